```python
import math
import jax
import jax.numpy as jnp
from jax import lax
import numpy as np

D_MODEL = 1024
BATCH = 2
SEQ = 8192
DEPTH = 2

GRID_W = 64
CTX_LEN = 256
HEAD_DIM = 64
GROUP_WIDTH = D_MODEL // 4
MIX_WIDTH = 4 * GROUP_WIDTH
S5_WIDTH = GROUP_WIDTH
S5_CH = 16
S5_GROUPS = S5_WIDTH // S5_CH
S5_STATE = 64
NA_HEADS = GROUP_WIDTH // HEAD_DIM
NA_KR = 8
NA_KC = 16
GQA_Q_HEADS = GROUP_WIDTH // HEAD_DIM
GQA_KV_HEADS = max(GQA_Q_HEADS // 2, 1)
GQA_GROUP = GQA_Q_HEADS // GQA_KV_HEADS
GQA_BLOCK = 128
LRU_WIDTH = GROUP_WIDTH
LRU_BLOCKS = GROUP_WIDTH // HEAD_DIM
LRU_BLOCK_DIM = LRU_WIDTH // LRU_BLOCKS
LRU_CONV = 4
LRU_C = 8.0
D_FF = ((8 * D_MODEL // 3 + 127) // 128) * 128
FFN_RES = 0.5
ROPE_BASE = 10000.0
EPS = 1e-6
N_MOD = 9
IN_SPLITS = (S5_WIDTH,
             NA_HEADS * HEAD_DIM, NA_HEADS * HEAD_DIM, NA_HEADS * HEAD_DIM,
             GQA_Q_HEADS * HEAD_DIM, GQA_KV_HEADS * HEAD_DIM, GQA_KV_HEADS * HEAD_DIM,
             LRU_WIDTH, LRU_WIDTH)
IN_COLS = sum(IN_SPLITS)

kernel_name = 'hybrid_s5_natten_gqa_rglru_macaron_dit'


def rms_norm(x, g):
    xf = x.astype(jnp.float32)
    y = xf * lax.rsqrt(jnp.mean(xf * xf, axis=-1, keepdims=True) + EPS)
    return (y * g.astype(jnp.float32)).astype(x.dtype)


def modulate(x, shift, scale):
    return x * (1 + scale) + shift


def ada_mods(s, w, b, dt):
    m = (s @ w.astype(jnp.float32) + b.astype(jnp.float32)).astype(dt)
    m = m.reshape(m.shape[0], N_MOD, D_MODEL)
    return [m[:, k, None, :] for k in range(N_MOD)]


def ffn_half(h, g, w_i, w_o, shift, scale, gate):
    y = modulate(rms_norm(h, g), shift, scale)
    a, b = jnp.split(y @ w_i, 2, axis=-1)
    return h + FFN_RES * gate * ((jax.nn.silu(a) * b) @ w_o)


def rope_1d(x, pos):
    half = x.shape[-1] // 2
    freqs = ROPE_BASE ** (-jnp.arange(half, dtype=jnp.float32) / half)
    ang = pos.astype(jnp.float32)[:, None] * freqs[None, :]
    cos = jnp.cos(ang)[None, :, None, :]
    sin = jnp.sin(ang)[None, :, None, :]
    xf = x.astype(jnp.float32)
    x1, x2 = xf[..., :half], xf[..., half:]
    return jnp.concatenate([x1 * cos - x2 * sin, x2 * cos + x1 * sin], axis=-1).astype(x.dtype)


def rope_2d(x, row, col):
    half = x.shape[-1] // 2
    return jnp.concatenate([rope_1d(x[..., :half], row), rope_1d(x[..., half:], col)], axis=-1)


def grouped_attn(q, k, v):
    s = jnp.einsum('bqhgd,bkhd->bhgqk', q, k).astype(jnp.float32)
    p = jax.nn.softmax(s, axis=-1).astype(v.dtype)
    return jnp.einsum('bhgqk,bkhd->bqhgd', p, v)


def _combine_real(left, right):
    a1, b1 = left
    a2, b2 = right
    return a1 * a2, a2 * b1 + b2


def _combine_complex(left, right):
    ar1, ai1, br1, bi1 = left
    ar2, ai2, br2, bi2 = right
    return (ar1 * ar2 - ai1 * ai2, ar1 * ai2 + ai1 * ar2,
            ar2 * br1 - ai2 * bi1 + br2, ar2 * bi1 + ai2 * br1 + bi2)


def real_scan(a, b, h0, reverse):
    if h0 is not None:
        end = -1 if reverse else 0
        b = b.at[:, end].add(a[:, end] * h0)
    _, h = lax.associative_scan(_combine_real, (a, b), reverse=reverse, axis=1)
    return h


def s5_discretize(lam_re, lam_im, log_step, b_re, b_im):
    f32 = jnp.float32
    lr, li = lam_re.astype(f32), lam_im.astype(f32)
    dt = jnp.exp(log_step.astype(f32))[:, None]
    mag = jnp.exp(lr * dt)
    ar, ai = mag * jnp.cos(li * dt), mag * jnp.sin(li * dt)
    den = lr * lr + li * li
    fr = ((ar - 1) * lr + ai * li) / den
    fi = (ai * lr - (ar - 1) * li) / den
    br, bi = b_re.astype(f32), b_im.astype(f32)
    bbr = fr[..., None] * br - fi[..., None] * bi
    bbi = fr[..., None] * bi + fi[..., None] * br
    return ar, ai, bbr, bbi


def s5_scan(u, ar, ai, bbr, bbi, h0, reverse):
    xr = jnp.einsum('btgh,gph->btgp', u, bbr)
    xi = jnp.einsum('btgh,gph->btgp', u, bbi)
    if h0 is not None:
        end = -1 if reverse else 0
        h0r, h0i = h0
        xr = xr.at[:, end].add(ar * h0r - ai * h0i)
        xi = xi.at[:, end].add(ar * h0i + ai * h0r)
    a_r = jnp.broadcast_to(ar, xr.shape)
    a_i = jnp.broadcast_to(ai, xr.shape)
    _, _, hr, hi = lax.associative_scan(_combine_complex, (a_r, a_i, xr, xi), reverse=reverse, axis=1)
    return hr, hi


def s5_readout(hr, hi, c_re, c_im):
    return (jnp.einsum('btgp,ghp->btgh', hr, c_re.astype(jnp.float32))
            - jnp.einsum('btgp,ghp->btgh', hi, c_im.astype(jnp.float32)))


def s5_mixer(u_c, u_l, lam_re, lam_im, log_step, b_re, b_im, c_re, c_im, d_skip, w_glu, need_ctx):
    def groups(u):
        return u.astype(jnp.float32).reshape(u.shape[0], u.shape[1], S5_GROUPS, S5_CH)

    def glu(y, dt):
        y = jax.nn.gelu(y.reshape(y.shape[0], y.shape[1], S5_WIDTH)).astype(dt)
        return y * jax.nn.sigmoid(y @ w_glu)

    d = d_skip.astype(jnp.float32).reshape(S5_GROUPS, S5_CH)
    uc, ul = groups(u_c), groups(u_l)
    y_l = d * ul
    y_c = d * uc if need_ctx else None
    for dr, reverse in ((0, False), (1, True)):
        ar, ai, bbr, bbi = s5_discretize(lam_re[dr], lam_im[dr], log_step[dr], b_re[dr], b_im[dr])
        hcr, hci = s5_scan(uc, ar, ai, bbr, bbi, None, reverse)
        end = 0 if reverse else -1
        hlr, hli = s5_scan(ul, ar, ai, bbr, bbi, (hcr[:, end], hci[:, end]), reverse)
        y_l = y_l + s5_readout(hlr, hli, c_re[dr], c_im[dr])
        if need_ctx:
            y_c = y_c + s5_readout(hcr, hci, c_re[dr], c_im[dr])
    return glu(y_l, u_l.dtype), (glu(y_c, u_c.dtype) if need_ctx else None)


def natten_mixer(q_c, k_c, v_c, q_l, k_l, v_l, rpb, need_ctx):
    B_, T, _ = q_l.shape
    rows = T // GRID_W
    kr = min(NA_KR, rows)
    nw = kr * NA_KC
    scale = HEAD_DIM ** -0.5

    def heads(t):
        return t.reshape(t.shape[0], t.shape[1], NA_HEADS, HEAD_DIM)

    qc, kc, vc = heads(q_c), heads(k_c), heads(v_c)
    qg = (heads(q_l) * scale).reshape(B_, rows, GRID_W, NA_HEADS, HEAD_DIM)
    kg = heads(k_l).reshape(B_, rows, GRID_W, NA_HEADS, HEAD_DIM)
    vg = heads(v_l).reshape(B_, rows, GRID_W, NA_HEADS, HEAD_DIM)
    col = jnp.arange(GRID_W)
    col_start = jnp.clip(col - NA_KC // 2, 0, GRID_W - NA_KC)
    col_idx = col_start[:, None] + jnp.arange(NA_KC)[None, :]
    col_off = col_idx - col[:, None] + (NA_KC - 1)
    rpb_f = rpb.astype(jnp.float32)

    def row_block(r):
        rs = jnp.clip(r - NA_KR // 2, 0, rows - kr)
        q_r = lax.dynamic_index_in_dim(qg, r, axis=1, keepdims=False)
        kb = lax.dynamic_slice_in_dim(kg, rs, kr, axis=1)[:, :, col_idx]
        vb = lax.dynamic_slice_in_dim(vg, rs, kr, axis=1)[:, :, col_idx]
        row_off = rs + jnp.arange(kr) - r + (NA_KR - 1)
        bias = rpb_f[:, row_off][:, :, col_off]
        s_win = (jnp.einsum('bwhd,brwkhd->bhwrk', q_r, kb).astype(jnp.float32)
                 + jnp.transpose(bias, (0, 2, 1, 3))[None])
        s_ctx = jnp.einsum('bwhd,bnhd->bhwn', q_r, kc).astype(jnp.float32)
        s = jnp.concatenate([s_win.reshape(B_, NA_HEADS, GRID_W, nw), s_ctx], axis=-1)
        p = jax.nn.softmax(s, axis=-1).astype(v_l.dtype)
        p_win = p[..., :nw].reshape(B_, NA_HEADS, GRID_W, kr, NA_KC)
        return (jnp.einsum('bhwrk,brwkhd->bwhd', p_win, vb)
                + jnp.einsum('bhwn,bnhd->bwhd', p[..., nw:], vc))

    o = lax.map(row_block, jnp.arange(rows))
    y_l = jnp.transpose(o, (1, 0, 2, 3, 4)).reshape(B_, T, NA_HEADS * HEAD_DIM)
    y_c = None
    if need_ctx:
        y_c = grouped_attn((qc * scale)[:, :, :, None, :], kc, vc).reshape(B_, qc.shape[1], NA_HEADS * HEAD_DIM)
    return y_l, y_c


def gqa_mixer(q_c, k_c, v_c, q_l, k_l, v_l, gq, gk, need_ctx):
    B_, T, _ = q_l.shape
    Tc = q_c.shape[1]
    scale = HEAD_DIM ** -0.5

    def prep(q, k, v):
        b, t, _ = q.shape
        q = rms_norm(q.reshape(b, t, GQA_Q_HEADS, HEAD_DIM), gq)
        k = rms_norm(k.reshape(b, t, GQA_KV_HEADS, HEAD_DIM), gk)
        return q, k, v.reshape(b, t, GQA_KV_HEADS, HEAD_DIM)

    qc, kc, vc = prep(q_c, k_c, v_c)
    ql, kl, vl = prep(q_l, k_l, v_l)
    t = jnp.arange(T)
    row, col = t // GRID_W, t % GRID_W
    ql = rope_2d(ql, row, col)
    kl = rope_2d(kl, row, col)
    k_all = jnp.concatenate([kc, kl], axis=1)
    v_all = jnp.concatenate([vc, vl], axis=1)
    nb = T // GQA_BLOCK
    qb = jnp.moveaxis((ql * scale).reshape(B_, nb, GQA_BLOCK, GQA_KV_HEADS, GQA_GROUP, HEAD_DIM), 1, 0)
    ob = lax.map(lambda qblk: grouped_attn(qblk, k_all, v_all), qb)
    y_l = jnp.moveaxis(ob, 0, 1).reshape(B_, T, GQA_Q_HEADS * HEAD_DIM)
    y_c = None
    if need_ctx:
        qcg = (qc * scale).reshape(B_, Tc, GQA_KV_HEADS, GQA_GROUP, HEAD_DIM)
        y_c = grouped_attn(qcg, kc, vc).reshape(B_, Tc, GQA_Q_HEADS * HEAD_DIM)
    return y_l, y_c


def centred_dwconv(x, w, b):
    cdim = x.shape[-1]
    y = lax.conv_general_dilated(
        x.astype(jnp.float32), w.astype(jnp.float32)[:, None, :], window_strides=(1,),
        padding=[(LRU_CONV // 2, LRU_CONV - 1 - LRU_CONV // 2)],
        dimension_numbers=('NWC', 'WIO', 'NWC'), feature_group_count=cdim)
    return y + b.astype(jnp.float32)


def rglru_coeffs(x, w_a, b_a, w_x, b_x, lam):
    f32 = jnp.float32
    B_, T, _ = x.shape
    xs = x.reshape(B_, T, LRU_BLOCKS, LRU_BLOCK_DIM)
    r = jax.nn.sigmoid(jnp.einsum('btnd,nde->btne', xs, w_a.astype(f32)).reshape(B_, T, LRU_WIDTH) + b_a.astype(f32))
    i = jax.nn.sigmoid(jnp.einsum('btnd,nde->btne', xs, w_x.astype(f32)).reshape(B_, T, LRU_WIDTH) + b_x.astype(f32))
    log_a = -LRU_C * r * jax.nn.softplus(-lam.astype(f32))
    return jnp.exp(log_a), jnp.sqrt(-jnp.expm1(2.0 * log_a)) * (i * x)


def rglru_mixer(x_c, g_c, x_l, g_l, conv_w, conv_b, w_a, b_a, w_x, b_x, lam, need_ctx):
    xc = centred_dwconv(x_c, conv_w, conv_b)
    xl = centred_dwconv(x_l, conv_w, conv_b)
    y_l = 0.0
    y_c = 0.0
    for dr, reverse in ((0, False), (1, True)):
        a_c, b_c = rglru_coeffs(xc, w_a[dr], b_a[dr], w_x[dr], b_x[dr], lam[dr])
        h_c = real_scan(a_c, b_c, None, reverse)
        a_l, b_l = rglru_coeffs(xl, w_a[dr], b_a[dr], w_x[dr], b_x[dr], lam[dr])
        h_l = real_scan(a_l, b_l, h_c[:, 0 if reverse else -1], reverse)
        y_l = y_l + h_l
        if need_ctx:
            y_c = y_c + h_c
    out_l = y_l.astype(x_l.dtype) * jax.nn.gelu(g_l)
    out_c = (y_c.astype(x_c.dtype) * jax.nn.gelu(g_c)) if need_ctx else None
    return out_l, out_c


def setup_inputs(seed: int = 0) -> dict:
    key = jax.random.key(seed)
    ks = iter(jax.random.split(key, 48))
    f32 = jnp.float32
    L, D = DEPTH, D_MODEL

    def nrm(shape, std):
        return std * jax.random.normal(next(ks), shape, f32)

    def gain(shape):
        return 1.0 + nrm(shape, 0.1)

    n = jnp.arange(S5_STATE, dtype=f32)
    a0 = jax.random.uniform(next(ks), (L, 2, LRU_WIDTH), f32, 0.9, 0.999)
    a_root = a0 ** (1.0 / LRU_C)
    lru_lambda = jnp.log(a_root) - jnp.log1p(-a_root)
    s5_log_step = jax.random.uniform(next(ks), (L, 2, S5_GROUPS), f32, math.log(1e-3), math.log(1e-1))
    return {
        'x': nrm((BATCH, SEQ, D), 1.0),
        'c': nrm((BATCH, D), 1.0),
        'ctx': nrm((BATCH, CTX_LEN, D), 1.0),
        'c_ctx': nrm((D,), 1.0),
        'w_ada': nrm((L, D, N_MOD * D), 0.5 * D ** -0.5),
        'b_ada': nrm((L, N_MOD * D), 0.02),
        'g_ffn1': gain((L, D)),
        'w_ffn1_in': nrm((L, D, 2 * D_FF), D ** -0.5),
        'w_ffn1_out': nrm((L, D_FF, D), D_FF ** -0.5),
        'g_mix': gain((L, D)),
        'w_in': nrm((L, D, IN_COLS), D ** -0.5),
        'w_out': nrm((L, MIX_WIDTH, D), MIX_WIDTH ** -0.5),
        's5_lambda_re': -0.5 + nrm((L, 2, S5_GROUPS, S5_STATE), 0.01),
        's5_lambda_im': math.pi * n + nrm((L, 2, S5_GROUPS, S5_STATE), 0.01),
        's5_log_step': s5_log_step,
        's5_b_re': nrm((L, 2, S5_GROUPS, S5_STATE, S5_CH), (2 * S5_CH) ** -0.5),
        's5_b_im': nrm((L, 2, S5_GROUPS, S5_STATE, S5_CH), (2 * S5_CH) ** -0.5),
        's5_c_re': nrm((L, 2, S5_GROUPS, S5_CH, S5_STATE), S5_STATE ** -0.5),
        's5_c_im': nrm((L, 2, S5_GROUPS, S5_CH, S5_STATE), S5_STATE ** -0.5),
        's5_d': nrm((L, S5_WIDTH), 0.5),
        's5_w_glu': nrm((L, S5_WIDTH, S5_WIDTH), S5_WIDTH ** -0.5),
        'na_rpb': nrm((L, NA_HEADS, 2 * NA_KR - 1, 2 * NA_KC - 1), 0.02),
        'gqa_q_norm': gain((L, HEAD_DIM)),
        'gqa_k_norm': gain((L, HEAD_DIM)),
        'lru_conv_w': nrm((L, LRU_CONV, LRU_WIDTH), LRU_CONV ** -0.5),
        'lru_conv_b': nrm((L, LRU_WIDTH), 0.02),
        'lru_w_a': nrm((L, 2, LRU_BLOCKS, LRU_BLOCK_DIM, LRU_BLOCK_DIM), LRU_BLOCK_DIM ** -0.5),
        'lru_b_a': nrm((L, 2, LRU_WIDTH), 0.02),
        'lru_w_x': nrm((L, 2, LRU_BLOCKS, LRU_BLOCK_DIM, LRU_BLOCK_DIM), LRU_BLOCK_DIM ** -0.5),
        'lru_b_x': nrm((L, 2, LRU_WIDTH), 0.02),
        'lru_lambda': lru_lambda,
        'g_ffn2': gain((L, D)),
        'w_ffn2_in': nrm((L, D, 2 * D_FF), D ** -0.5),
        'w_ffn2_out': nrm((L, D_FF, D), D_FF ** -0.5),
        'g_final': gain((D,)),
    }


def reference(x, c, ctx, c_ctx, w_ada, b_ada, g_ffn1, w_ffn1_in, w_ffn1_out, g_mix, w_in, w_out,
              s5_lambda_re, s5_lambda_im, s5_log_step, s5_b_re, s5_b_im, s5_c_re, s5_c_im, s5_d, s5_w_glu,
              na_rpb, gqa_q_norm, gqa_k_norm,
              lru_conv_w, lru_conv_b, lru_w_a, lru_b_a, lru_w_x, lru_b_x, lru_lambda,
              g_ffn2, w_ffn2_in, w_ffn2_out, g_final):
    dt = x.dtype
    split_at = [int(o) for o in np.cumsum(IN_SPLITS)[:-1]]
    s_lat = jax.nn.silu(c.astype(jnp.float32))
    s_ctx = jax.nn.silu(c_ctx.astype(jnp.float32))[None]
    h_lat, h_ctx = x, ctx
    for l in range(DEPTH):
        need_ctx = l < DEPTH - 1
        ml = ada_mods(s_lat, w_ada[l], b_ada[l], dt)
        mc = ada_mods(s_ctx, w_ada[l], b_ada[l], dt)
        h_lat = ffn_half(h_lat, g_ffn1[l], w_ffn1_in[l], w_ffn1_out[l], ml[0], ml[1], ml[2])
        h_ctx = ffn_half(h_ctx, g_ffn1[l], w_ffn1_in[l], w_ffn1_out[l], mc[0], mc[1], mc[2])
        p_l = jnp.split(modulate(rms_norm(h_lat, g_mix[l]), ml[3], ml[4]) @ w_in[l], split_at, axis=-1)
        p_c = jnp.split(modulate(rms_norm(h_ctx, g_mix[l]), mc[3], mc[4]) @ w_in[l], split_at, axis=-1)
        ya_l, ya_c = s5_mixer(p_c[0], p_l[0], s5_lambda_re[l], s5_lambda_im[l], s5_log_step[l],
                              s5_b_re[l], s5_b_im[l], s5_c_re[l], s5_c_im[l], s5_d[l], s5_w_glu[l], need_ctx)
        yb_l, yb_c = natten_mixer(p_c[1], p_c[2], p_c[3], p_l[1], p_l[2], p_l[3], na_rpb[l], need_ctx)
        yc_l, yc_c = gqa_mixer(p_c[4], p_c[5], p_c[6], p_l[4], p_l[5], p_l[6],
                               gqa_q_norm[l], gqa_k_norm[l], need_ctx)
        yd_l, yd_c = rglru_mixer(p_c[7], p_c[8], p_l[7], p_l[8], lru_conv_w[l], lru_conv_b[l],
                                 lru_w_a[l], lru_b_a[l], lru_w_x[l], lru_b_x[l], lru_lambda[l], need_ctx)
        h_lat = h_lat + ml[5] * (jnp.concatenate([ya_l, yb_l, yc_l, yd_l], axis=-1) @ w_out[l])
        h_lat = ffn_half(h_lat, g_ffn2[l], w_ffn2_in[l], w_ffn2_out[l], ml[6], ml[7], ml[8])
        if need_ctx:
            h_ctx = h_ctx + mc[5] * (jnp.concatenate([ya_c, yb_c, yc_c, yd_c], axis=-1) @ w_out[l])
            h_ctx = ffn_half(h_ctx, g_ffn2[l], w_ffn2_in[l], w_ffn2_out[l], mc[6], mc[7], mc[8])
    return rms_norm(h_lat, g_final)
```

```python
import functools
import math

import jax
import jax.numpy as jnp
from jax import lax
from jax.experimental import pallas as pl
from jax.experimental.pallas import tpu as pltpu

F32 = jnp.float32
BF16 = jnp.bfloat16

D_MODEL = 1024
GRID_W = 64
HEAD_DIM = 64
GROUP_WIDTH = D_MODEL // 4
S5_CH = 16
S5_GROUPS = GROUP_WIDTH // S5_CH
S5_STATE = 64
NA_HEADS = GROUP_WIDTH // HEAD_DIM
NA_KR = 8
NA_KC = 16
GQA_Q_HEADS = GROUP_WIDTH // HEAD_DIM
GQA_KV_HEADS = GQA_Q_HEADS // 2
GQA_GROUP = GQA_Q_HEADS // GQA_KV_HEADS
LRU_BLOCKS = GROUP_WIDTH // HEAD_DIM
LRU_CONV = 4
LRU_C = 8.0
D_FF = ((8 * D_MODEL // 3 + 127) // 128) * 128
FFN_RES = 0.5
ROPE_BASE = 10000.0
EPS = 1e-6
N_MOD = 9

TOKEN_TILE = 512
SEQ_TILE = 256
S5_CHUNK = 32
FF_CHUNK = D_FF // 2
FLASH_TQ = 512
NEG_BIG = -1e30
VMEM_LIMIT = 56 * 1024 * 1024


def _cparams(sem):
    return pltpu.CompilerParams(dimension_semantics=sem, vmem_limit_bytes=VMEM_LIMIT)


def _dot(a, b):
    return jnp.dot(a, b, preferred_element_type=F32)


def _dot_nt(a, b):
    return lax.dot_general(a, b, (((1,), (1,)), ((), ())), preferred_element_type=F32)


def _ada_kernel(c_ref, w_ref, b_ref, o_ref):
    c = c_ref[...]
    s = c * jax.nn.sigmoid(c)
    o_ref[...] = _dot(s.astype(BF16), w_ref[...].astype(BF16)) + b_ref[...]


def ada_mods(c8, w_ada, b_ada):
    L = w_ada.shape[0]
    tn = D_MODEL
    return pl.pallas_call(
        _ada_kernel,
        grid=(L, N_MOD * D_MODEL // tn),
        in_specs=[
            pl.BlockSpec((8, D_MODEL), lambda l, j: (0, 0)),
            pl.BlockSpec((None, D_MODEL, tn), lambda l, j: (l, 0, j)),
            pl.BlockSpec((None, 1, tn), lambda l, j: (l, 0, j)),
        ],
        out_specs=pl.BlockSpec((None, 8, tn), lambda l, j: (l, 0, j)),
        out_shape=jax.ShapeDtypeStruct((L, 8, N_MOD * D_MODEL), F32),
        compiler_params=_cparams(("parallel", "parallel")),
        name="ada_mods",
    )(c8, w_ada, b_ada.reshape(L, 1, N_MOD * D_MODEL))


def _norm_mod(x, g, shift, scale):
    ms = jnp.mean(x * x, axis=-1, keepdims=True)
    y = x * lax.rsqrt(ms + EPS) * g
    return y * (1.0 + scale) + shift


def _ffn_kernel(x_ref, mod_ref, g_ref, wi_ref, wo_ref, *rest, k0, final):
    if final:
        gf_ref, o_ref = rest
    else:
        (o_ref,) = rest
    x = x_ref[...]
    y = _norm_mod(x, g_ref[...], mod_ref[k0:k0 + 1, :], mod_ref[k0 + 1:k0 + 2, :]).astype(BF16)
    acc = jnp.zeros(x.shape, F32)
    for c in range(D_FF // FF_CHUNK):
        lo = c * FF_CHUNK
        a = _dot(y, wi_ref[:, lo:lo + FF_CHUNK])
        b = _dot(y, wi_ref[:, D_FF + lo:D_FF + lo + FF_CHUNK])
        h = (a * jax.nn.sigmoid(a) * b).astype(BF16)
        acc = acc + _dot(h, wo_ref[lo:lo + FF_CHUNK, :])
    out = x + FFN_RES * mod_ref[k0 + 2:k0 + 3, :] * acc
    if final:
        ms = jnp.mean(out * out, axis=-1, keepdims=True)
        out = out * lax.rsqrt(ms + EPS) * gf_ref[...]
    o_ref[...] = out


def ffn_half(x, mods, g, w_i, w_o, k0, n_rows, tiles_per_type, g_final=None):
    final = g_final is not None
    nt = n_rows // TOKEN_TILE
    in_specs = [
        pl.BlockSpec((TOKEN_TILE, D_MODEL), lambda i: (i, 0)),
        pl.BlockSpec((None, N_MOD, D_MODEL), lambda i: (i // tiles_per_type, 0, 0)),
        pl.BlockSpec((1, D_MODEL), lambda i: (0, 0)),
        pl.BlockSpec((D_MODEL, 2 * D_FF), lambda i: (0, 0)),
        pl.BlockSpec((D_FF, D_MODEL), lambda i: (0, 0)),
    ]
    args = [x, mods, g.reshape(1, D_MODEL), w_i, w_o]
    if final:
        in_specs.append(pl.BlockSpec((1, D_MODEL), lambda i: (0, 0)))
        args.append(g_final.reshape(1, D_MODEL))
    return pl.pallas_call(
        functools.partial(_ffn_kernel, k0=k0, final=final),
        grid=(nt,),
        in_specs=in_specs,
        out_specs=pl.BlockSpec((TOKEN_TILE, D_MODEL), lambda i: (i, 0)),
        out_shape=jax.ShapeDtypeStruct((n_rows, D_MODEL), F32),
        compiler_params=_cparams(("parallel",)),
        name="ffn_half",
    )(*args)


def _inproj_kernel(x_ref, mod_ref, g_ref, w_ref, ps_ref, pa_ref):
    y = _norm_mod(x_ref[...], g_ref[...], mod_ref[3:4, :], mod_ref[4:5, :]).astype(BF16)
    p = _dot(y, w_ref[...])
    ns = ps_ref.shape[1]
    ps_ref[...] = p[:, :ns]
    pa_ref[...] = p[:, ns:].astype(BF16)


def in_projection(x, mods, g, w_in_perm, tiles_per_type):
    n = x.shape[0]
    n_scan = 3 * GROUP_WIDTH
    n_attn = w_in_perm.shape[1] - n_scan
    return pl.pallas_call(
        _inproj_kernel,
        grid=(n // TOKEN_TILE,),
        in_specs=[
            pl.BlockSpec((TOKEN_TILE, D_MODEL), lambda i: (i, 0)),
            pl.BlockSpec((None, N_MOD, D_MODEL), lambda i: (i // tiles_per_type, 0, 0)),
            pl.BlockSpec((1, D_MODEL), lambda i: (0, 0)),
            pl.BlockSpec((D_MODEL, n_scan + n_attn), lambda i: (0, 0)),
        ],
        out_specs=[
            pl.BlockSpec((TOKEN_TILE, n_scan), lambda i: (i, 0)),
            pl.BlockSpec((TOKEN_TILE, n_attn), lambda i: (i, 0)),
        ],
        out_shape=[
            jax.ShapeDtypeStruct((n, n_scan), F32),
            jax.ShapeDtypeStruct((n, n_attn), BF16),
        ],
        compiler_params=_cparams(("parallel",)),
        name="in_projection",
    )(x, mods, g.reshape(1, D_MODEL), w_in_perm)


def _outproj_kernel(h_ref, mod_ref, ys5_ref, u_ref, hf_ref, hr_ref, gl_ref, yb_ref, yc_ref,
                    dsk_ref, wglu_ref, wo_ref, o_ref):
    gw = GROUP_WIDTH
    ya = jax.nn.gelu(ys5_ref[...] + dsk_ref[...] * u_ref[...])
    ya = ya * jax.nn.sigmoid(_dot(ya.astype(BF16), wglu_ref[...]))
    yd = (hf_ref[...] + hr_ref[...]) * jax.nn.gelu(gl_ref[...])
    acc = _dot(ya.astype(BF16), wo_ref[0:gw, :])
    acc = acc + _dot(yb_ref[...].astype(BF16), wo_ref[gw:2 * gw, :])
    acc = acc + _dot(yc_ref[...].astype(BF16), wo_ref[2 * gw:3 * gw, :])
    acc = acc + _dot(yd.astype(BF16), wo_ref[3 * gw:4 * gw, :])
    o_ref[...] = h_ref[...] + mod_ref[5:6, :] * acc


def out_projection(h, mods, ys5, pscan, hf, hr, yb, yc, d_skip, w_glu, w_out, n_rows, tiles_per_type):
    gw = GROUP_WIDTH
    row = lambda i: (i, 0)
    return pl.pallas_call(
        _outproj_kernel,
        grid=(n_rows // TOKEN_TILE,),
        in_specs=[
            pl.BlockSpec((TOKEN_TILE, D_MODEL), row),
            pl.BlockSpec((None, N_MOD, D_MODEL), lambda i: (i // tiles_per_type, 0, 0)),
            pl.BlockSpec((TOKEN_TILE, gw), row),
            pl.BlockSpec((TOKEN_TILE, gw), lambda i: (i, 0)),
            pl.BlockSpec((TOKEN_TILE, gw), row),
            pl.BlockSpec((TOKEN_TILE, gw), row),
            pl.BlockSpec((TOKEN_TILE, gw), lambda i: (i, 2)),
            pl.BlockSpec((TOKEN_TILE, gw), row),
            pl.BlockSpec((TOKEN_TILE, gw), row),
            pl.BlockSpec((1, gw), lambda i: (0, 0)),
            pl.BlockSpec((gw, gw), lambda i: (0, 0)),
            pl.BlockSpec((D_MODEL, D_MODEL), lambda i: (0, 0)),
        ],
        out_specs=pl.BlockSpec((TOKEN_TILE, D_MODEL), row),
        out_shape=jax.ShapeDtypeStruct((n_rows, D_MODEL), F32),
        compiler_params=_cparams(("parallel",)),
        name="out_projection",
    )(h, mods, ys5, pscan, hf, hr, pscan, yb, yc, d_skip.reshape(1, gw), w_glu, w_out)


def _s5_tables(lam_re, lam_im, log_step, b_re, b_im, c_re, c_im):
    L = S5_CHUNK
    G, P, H = S5_GROUPS, S5_STATE, S5_CH
    hp = lax.Precision.HIGHEST
    outs = []
    for dr in range(2):
        lr, li = lam_re[dr].astype(F32), lam_im[dr].astype(F32)
        dt = jnp.exp(log_step[dr].astype(F32))[:, None]
        mag = jnp.exp(lr * dt)
        ar, ai = mag * jnp.cos(li * dt), mag * jnp.sin(li * dt)
        den = lr * lr + li * li
        fr = ((ar - 1) * lr + ai * li) / den
        fi = (ai * lr - (ar - 1) * li) / den
        br, bi = b_re[dr].astype(F32), b_im[dr].astype(F32)
        bbr = fr[..., None] * br - fi[..., None] * bi
        bbi = fr[..., None] * bi + fi[..., None] * br
        cr, ci = c_re[dr].astype(F32), c_im[dr].astype(F32)
        pr, pi = [jnp.ones_like(ar)], [jnp.zeros_like(ar)]
        for _ in range(L):
            pr.append(pr[-1] * ar - pi[-1] * ai)
            pi.append(pr[-2] * ai + pi[-1] * ar)
        pw_r, pw_i = jnp.stack(pr), jnp.stack(pi)
        abr = pw_r[:L, :, :, None] * bbr[None] - pw_i[:L, :, :, None] * bbi[None]
        abi = pw_r[:L, :, :, None] * bbi[None] + pw_i[:L, :, :, None] * bbr[None]
        kk = (jnp.einsum('gop,lgpi->lgoi', cr, abr, precision=hp)
              - jnp.einsum('gop,lgpi->lgoi', ci, abi, precision=hp))
        tt = jnp.arange(L)
        lag = (tt[:, None] - tt[None, :]) if dr == 0 else (tt[None, :] - tt[:, None])
        kt = jnp.where((lag >= 0)[:, :, None, None, None], kk[jnp.clip(lag, 0, L - 1)], 0.0)
        mt = jnp.transpose(kt, (2, 1, 4, 0, 3)).reshape(G, L * H, L * H)
        sel = (L - 1 - tt) if dr == 0 else tt
        bsr = jnp.transpose(abr[sel], (1, 0, 3, 2)).reshape(G, L * H, P)
        bsi = jnp.transpose(abi[sel], (1, 0, 3, 2)).reshape(G, L * H, P)
        kpow = (tt + 1) if dr == 0 else (L - tt)
        qr, qi = pw_r[kpow], pw_i[kpow]
        ccr = cr[None] * qr[:, :, None, :] - ci[None] * qi[:, :, None, :]
        cci = -cr[None] * qi[:, :, None, :] - ci[None] * qr[:, :, None, :]
        ccr = jnp.transpose(ccr, (1, 3, 0, 2)).reshape(G, P, L * H)
        cci = jnp.transpose(cci, (1, 3, 0, 2)).reshape(G, P, L * H)
        outs.append((mt, bsr, bsi, ccr, cci, pw_r[L][:, None, :], pw_i[L][:, None, :]))
    stk = [jnp.stack([outs[0][k], outs[1][k]]) for k in range(7)]
    return (stk[0].astype(BF16), stk[1].astype(BF16), stk[2].astype(BF16),
            stk[3], stk[4], stk[5], stk[6])


def _s5_kernel(u_ref, mt_ref, bsr_ref, bsi_ref, ccr_ref, cci_ref, alr_ref, ali_ref, o_ref,
               sr_s, si_s, hr_s, hi_s, *, n_batch, ncl, ncc):
    nc = ncl + ncc
    u = u_ref[...]
    total = None
    for dr in range(2):
        y = _dot(u, mt_ref[dr])
        sr_s[...] = _dot(u, bsr_ref[dr])
        si_s[...] = _dot(u, bsi_ref[dr])
        alr = alr_ref[dr]
        ali = ali_ref[dr]

        def step(j, carry, dr=dr, alr=alr, ali=ali):
            if dr == 0:
                idx = jnp.where(j < ncc, ncl + j, j - ncc)
            else:
                idx = jnp.where(j < ncc, nc - 1 - j, nc - 1 - j)
            new = []
            for b in range(n_batch):
                hr, hi = carry[2 * b], carry[2 * b + 1]
                row = b * nc + idx
                hr_s[pl.ds(row, 1), :] = hr
                hi_s[pl.ds(row, 1), :] = hi
                sr = sr_s[pl.ds(row, 1), :]
                si = si_s[pl.ds(row, 1), :]
                new.append(alr * hr - ali * hi + sr)
                new.append(alr * hi + ali * hr + si)
            return tuple(new)

        zero = jnp.zeros((1, S5_STATE), F32)
        lax.fori_loop(0, nc, step, (zero,) * (2 * n_batch))
        y = y + _dot(hr_s[...], ccr_ref[dr]) + _dot(hi_s[...], cci_ref[dr])
        total = y if total is None else total + y
    o_ref[...] = total


def s5_scan(u_t, tables, n_batch, ncl, ncc):
    mt, bsr, bsi, ccr, cci, alr, ali = tables
    G, R, W = u_t.shape
    P = S5_STATE
    g4 = lambda g: (0, g, 0, 0)
    return pl.pallas_call(
        functools.partial(_s5_kernel, n_batch=n_batch, ncl=ncl, ncc=ncc),
        grid=(G,),
        in_specs=[
            pl.BlockSpec((None, R, W), lambda g: (g, 0, 0)),
            pl.BlockSpec((2, None, W, W), g4),
            pl.BlockSpec((2, None, W, P), g4),
            pl.BlockSpec((2, None, W, P), g4),
            pl.BlockSpec((2, None, P, W), g4),
            pl.BlockSpec((2, None, P, W), g4),
            pl.BlockSpec((2, None, 1, P), g4),
            pl.BlockSpec((2, None, 1, P), g4),
        ],
        out_specs=pl.BlockSpec((None, R, W), lambda g: (g, 0, 0)),
        out_shape=jax.ShapeDtypeStruct((G, R, W), F32),
        scratch_shapes=[pltpu.VMEM((R, P), F32)] * 4,
        compiler_params=_cparams(("parallel",)),
        name="s5_scan",
    )(u_t, mt, bsr, bsi, ccr, cci, alr, ali)


def _s5_to_chunks(u, n_batch, t_lat, t_ctx):
    L, G, H = S5_CHUNK, S5_GROUPS, S5_CH
    nl = n_batch * t_lat
    lat = u[:nl].reshape(n_batch, t_lat // L, L, G, H)
    ctx = u[nl:].reshape(n_batch, t_ctx // L, L, G, H)
    x = jnp.concatenate([lat, ctx], axis=1)
    x = jnp.transpose(x, (3, 0, 1, 2, 4))
    return x.reshape(G, n_batch * (t_lat + t_ctx) // L, L * H).astype(BF16)


def _s5_from_chunks(y_t, n_batch, t_lat, t_ctx, need_ctx):
    L, G, H = S5_CHUNK, S5_GROUPS, S5_CH
    ncl = t_lat // L
    y = y_t.reshape(G, n_batch, (t_lat + t_ctx) // L, L, H)
    y = jnp.transpose(y, (1, 2, 3, 0, 4))
    lat = y[:, :ncl].reshape(n_batch * t_lat, G * H)
    if not need_ctx:
        return lat
    ctx = y[:, ncl:].reshape(n_batch * t_ctx, G * H)
    return jnp.concatenate([lat, ctx], axis=0)


def _lru_kernel(xp_ref, xc_ref, xn_ref, cw_ref, cb_ref, wa_ref, ba_ref, wx_ref, bx_ref, lam_ref,
                o_ref, xs_s, a_s, b_s, h_s, *, reverse, ntl, ntc):
    tt = SEQ_TILE
    j = pl.program_id(1)
    is_ctx = j < ntc
    if reverse:
        k = jnp.where(is_ctx, ntc - 1 - j, ntl - 1 - (j - ntc))
    else:
        k = jnp.where(is_ctx, j, j - ntc)
    seg = jnp.where(is_ctx, ntc, ntl)
    has_prev = (k > 0).astype(F32)
    has_next = (k < seg - 1).astype(F32)

    @pl.when(j == 0)
    def _():
        h_s[...] = jnp.zeros_like(h_s)

    xs_s[0:8, :] = xp_ref[tt - 8:tt, :] * has_prev
    xs_s[8:8 + tt, :] = xc_ref[...]
    xs_s[8 + tt:16 + tt, :] = xn_ref[0:8, :] * has_next
    xc = cb_ref[...] + jnp.zeros((tt, GROUP_WIDTH), F32)
    for tap in range(LRU_CONV):
        xc = xc + cw_ref[tap:tap + 1, :] * xs_s[pl.ds(6 + tap, tt), :]
    xb = xc.astype(BF16)
    r = jax.nn.sigmoid(_dot(xb, wa_ref[...]) + ba_ref[...])
    i = jax.nn.sigmoid(_dot(xb, wx_ref[...]) + bx_ref[...])
    z = -lam_ref[...]
    softplus = jnp.maximum(z, 0.0) + jnp.log1p(jnp.exp(-jnp.abs(z)))
    log_a = -LRU_C * r * softplus
    a_s[...] = jnp.exp(log_a)
    th = jnp.tanh(log_a)
    b_s[...] = jnp.sqrt(-2.0 * th / (1.0 - th)) * (i * xc)

    rows = lax.broadcasted_iota(jnp.int32, (8, GROUP_WIDTH), 0)
    nblk = tt // 8

    def block(q, h):
        blk = (nblk - 1 - q) if reverse else q
        off = pl.multiple_of(blk * 8, 8)
        a = a_s[pl.ds(off, 8), :]
        b = b_s[pl.ds(off, 8), :]
        for sh in (1, 2, 4):
            if reverse:
                keep = rows < 8 - sh
                a_sh = jnp.where(keep, pltpu.roll(a, 8 - sh, 0), 1.0)
                b_sh = jnp.where(keep, pltpu.roll(b, 8 - sh, 0), 0.0)
            else:
                keep = rows >= sh
                a_sh = jnp.where(keep, pltpu.roll(a, sh, 0), 1.0)
                b_sh = jnp.where(keep, pltpu.roll(b, sh, 0), 0.0)
            b = a * b_sh + b
            a = a * a_sh
        hb = a * h + b
        o_ref[pl.ds(off, 8), :] = hb
        last = hb[0:1, :] if reverse else hb[7:8, :]
        return jnp.broadcast_to(last, (8, GROUP_WIDTH))

    h_s[...] = lax.fori_loop(0, nblk, block, h_s[...])


def lru_scan(pscan, conv_w, conv_b, w_a, b_a, w_x, b_x, lam, reverse, n_batch, t_lat, t_ctx):
    tt = SEQ_TILE
    gw = GROUP_WIDTH
    ntl, ntc = t_lat // tt, t_ctx // tt
    n = pscan.shape[0]

    def tile(b, j, d):
        is_ctx = j < ntc
        if reverse:
            k = jnp.where(is_ctx, ntc - 1 - j, ntl - 1 - (j - ntc))
        else:
            k = jnp.where(is_ctx, j, j - ntc)
        seg = jnp.where(is_ctx, ntc, ntl)
        k = jnp.clip(k + d, 0, seg - 1)
        return jnp.where(is_ctx, n_batch * ntl + b * ntc + k, b * ntl + k)

    vec = pl.BlockSpec((1, gw), lambda b, j: (0, 0))
    mat = pl.BlockSpec((gw, gw), lambda b, j: (0, 0))
    return pl.pallas_call(
        functools.partial(_lru_kernel, reverse=reverse, ntl=ntl, ntc=ntc),
        grid=(n_batch, ntl + ntc),
        in_specs=[
            pl.BlockSpec((tt, gw), lambda b, j: (tile(b, j, -1), 1)),
            pl.BlockSpec((tt, gw), lambda b, j: (tile(b, j, 0), 1)),
            pl.BlockSpec((tt, gw), lambda b, j: (tile(b, j, 1), 1)),
            pl.BlockSpec((LRU_CONV, gw), lambda b, j: (0, 0)),
            vec, mat, vec, mat, vec, vec,
        ],
        out_specs=pl.BlockSpec((tt, gw), lambda b, j: (tile(b, j, 0), 0)),
        out_shape=jax.ShapeDtypeStruct((n, gw), F32),
        scratch_shapes=[
            pltpu.VMEM((tt + 16, gw), F32),
            pltpu.VMEM((tt, gw), F32),
            pltpu.VMEM((tt, gw), F32),
            pltpu.VMEM((8, gw), F32),
        ],
        compiler_params=_cparams(("parallel", "arbitrary")),
        name="lru_rev" if reverse else "lru_fwd",
    )(pscan, pscan, pscan, conv_w, conv_b.reshape(1, gw), w_a, b_a.reshape(1, gw), w_x, b_x.reshape(1, gw),
      lam.reshape(1, gw))


def _block_diag(w):
    n, d, e = w.shape
    eye = jnp.eye(n, dtype=w.dtype)
    return (eye[:, None, :, None] * w[:, :, None, :]).reshape(n * d, n * e)


def _natten_bias(rpb):
    W, KR, KC = GRID_W, NA_KR, NA_KC
    col = jnp.arange(W)
    cs = jnp.clip(col - KC // 2, 0, W - KC)
    wk = jnp.arange(W)
    inwin = (wk[None, :] >= cs[:, None]) & (wk[None, :] < cs[:, None] + KC)
    coff = jnp.clip(wk[None, :] - col[:, None] + (KC - 1), 0, 2 * KC - 2)
    v = jnp.arange(KR)
    i = jnp.arange(KR)
    roff = -v[:, None] + i[None, :] + (KR - 1)
    b = rpb.astype(F32)[:, roff]
    b = b[:, :, :, coff]
    b = jnp.where(inwin[None, None, None], b, NEG_BIG)
    b = jnp.transpose(b, (1, 0, 3, 2, 4))
    return b.reshape(KR, rpb.shape[0], W, KR * W)


def _natten_kernel(q_ref, k_ref, v_ref, kc_ref, vc_ref, bias_ref, o_ref, *, rows):
    W, KR, hd = GRID_W, NA_KR, HEAD_DIM
    r = pl.program_id(1)
    rs = jnp.clip(r - KR // 2, 0, rows - KR)
    var = r - rs
    start = pl.multiple_of(rs * W, W)
    q = q_ref[...]
    kw = k_ref[pl.ds(start, KR * W), :]
    vw = v_ref[pl.ds(start, KR * W), :]
    kc = kc_ref[...]
    vc = vc_ref[...]
    scale = hd ** -0.5
    for h in range(NA_HEADS):
        sl = slice(h * hd, (h + 1) * hd)
        qh = q[:, sl]
        s = _dot_nt(qh, kw[:, sl]) * scale + bias_ref[var, h]
        sc = _dot_nt(qh, kc[:, sl]) * scale
        m = jnp.maximum(jnp.max(s, axis=-1, keepdims=True), jnp.max(sc, axis=-1, keepdims=True))
        p = jnp.exp(s - m)
        pc = jnp.exp(sc - m)
        l = jnp.sum(p, axis=-1, keepdims=True) + jnp.sum(pc, axis=-1, keepdims=True)
        o = _dot(p.astype(BF16), vw[:, sl]) + _dot(pc.astype(BF16), vc[:, sl])
        o_ref[:, sl] = o / l


def natten_latent(pattn, bias, n_batch, t_lat, t_ctx):
    gw = GROUP_WIDTH
    rows = t_lat // GRID_W
    assert rows >= NA_KR
    ctx0 = n_batch * t_lat // t_ctx
    return pl.pallas_call(
        functools.partial(_natten_kernel, rows=rows),
        grid=(n_batch, rows),
        in_specs=[
            pl.BlockSpec((GRID_W, gw), lambda b, r: (b * rows + r, 2)),
            pl.BlockSpec((t_lat, gw), lambda b, r: (b, 3)),
            pl.BlockSpec((t_lat, gw), lambda b, r: (b, 4)),
            pl.BlockSpec((t_ctx, gw), lambda b, r: (ctx0 + b, 3)),
            pl.BlockSpec((t_ctx, gw), lambda b, r: (ctx0 + b, 4)),
            pl.BlockSpec(bias.shape, lambda b, r: (0, 0, 0, 0)),
        ],
        out_specs=pl.BlockSpec((GRID_W, gw), lambda b, r: (b * rows + r, 0)),
        out_shape=jax.ShapeDtypeStruct((n_batch * t_lat, gw), F32),
        compiler_params=_cparams(("parallel", "arbitrary")),
        name="natten_latent",
    )(pattn, pattn, pattn, pattn, pattn, bias)


def _ctx_attn_kernel(q_ref, k_ref, v_ref, o_ref):
    hd = HEAD_DIM
    q, k, v = q_ref[...], k_ref[...], v_ref[...]
    for h in range(NA_HEADS):
        sl = slice(h * hd, (h + 1) * hd)
        s = _dot_nt(q[:, sl], k[:, sl]) * (hd ** -0.5)
        m = jnp.max(s, axis=-1, keepdims=True)
        p = jnp.exp(s - m)
        l = jnp.sum(p, axis=-1, keepdims=True)
        o_ref[:, sl] = _dot(p.astype(BF16), v[:, sl]) / l


def natten_context(pattn, n_batch, t_lat, t_ctx):
    gw = GROUP_WIDTH
    ctx0 = n_batch * t_lat // t_ctx
    return pl.pallas_call(
        _ctx_attn_kernel,
        grid=(n_batch,),
        in_specs=[pl.BlockSpec((t_ctx, gw), lambda b, c=c: (ctx0 + b, c)) for c in (2, 3, 4)],
        out_specs=pl.BlockSpec((t_ctx, gw), lambda b: (b, 0)),
        out_shape=jax.ShapeDtypeStruct((n_batch * t_ctx, gw), F32),
        compiler_params=_cparams(("parallel",)),
        name="natten_context",
    )(pattn, pattn, pattn)


def _rope_tables(t_lat, t_ctx):
    half = HEAD_DIM // 4
    freqs = ROPE_BASE ** (-jnp.arange(half, dtype=F32) / half)
    t = jnp.arange(t_lat)
    ang_r = (t // GRID_W).astype(F32)[:, None] * freqs[None, :]
    ang_c = (t % GRID_W).astype(F32)[:, None] * freqs[None, :]
    ang = jnp.concatenate([ang_r, ang_r, ang_c, ang_c], axis=1)
    cos = jnp.concatenate([jnp.cos(ang), jnp.ones((t_ctx, HEAD_DIM), F32)], axis=0)
    sin = jnp.concatenate([jnp.sin(ang), jnp.zeros((t_ctx, HEAD_DIM), F32)], axis=0)
    return jnp.tile(cos, (1, GQA_Q_HEADS)), jnp.tile(sin, (1, GQA_Q_HEADS))


def _gqa_prep_kernel(x_ref, cos_ref, sin_ref, gq_ref, gk_ref, gm_ref, q_ref, k_ref, v_ref):
    hd = HEAD_DIM
    nq, nk = GQA_Q_HEADS * hd, GQA_KV_HEADS * hd
    x = x_ref[...].astype(F32)
    cos, sin = cos_ref[...], sin_ref[...]
    gm = gm_ref[...]

    def norm_rope(t, gain, width):
        t2 = t * t
        hi = t2.astype(BF16)
        lo = (t2 - hi.astype(F32)).astype(BF16)
        ms = _dot(hi, gm[:width, :width]) + _dot(lo, gm[:width, :width])
        tn = t * lax.rsqrt(ms + EPS) * gain
        lane = lax.broadcasted_iota(jnp.int32, tn.shape, 1)
        first = (lane % (hd // 2)) < (hd // 4)
        rot = jnp.where(first, -pltpu.roll(tn, width - hd // 4, 1), pltpu.roll(tn, hd // 4, 1))
        return tn * cos[:, :width] + rot * sin[:, :width]

    q = norm_rope(x[:, :nq], gq_ref[...], nq) * (hd ** -0.5)
    k = norm_rope(x[:, nq:nq + nk], gk_ref[...], nk)
    v = x_ref[:, nq + nk:nq + 2 * nk]
    for h in range(GQA_Q_HEADS):
        q_ref[h] = q[:, h * hd:(h + 1) * hd].astype(BF16)
    for h in range(GQA_KV_HEADS):
        k_ref[h] = k[:, h * hd:(h + 1) * hd].astype(BF16)
        v_ref[h] = v[:, h * hd:(h + 1) * hd]


def gqa_prep(pattn, gq, gk, n_batch, t_lat, t_ctx):
    tt = SEQ_TILE
    hd = HEAD_DIM
    ntl, ntc = t_lat // tt, t_ctx // tt
    s_len = t_lat + t_ctx
    cos, sin = _rope_tables(t_lat, t_ctx)
    nq = GQA_Q_HEADS * hd
    gmat = _block_diag(jnp.full((GQA_Q_HEADS, hd, hd), 1.0 / hd, F32)).astype(BF16)

    def pos(i):
        is_ctx = i >= n_batch * ntl
        c = i - n_batch * ntl
        return jnp.where(is_ctx, ntl + c % ntc, i % ntl)

    def bat(i):
        is_ctx = i >= n_batch * ntl
        return jnp.where(is_ctx, (i - n_batch * ntl) // ntc, i // ntl)

    return pl.pallas_call(
        _gqa_prep_kernel,
        grid=(n_batch * (ntl + ntc),),
        in_specs=[
            pl.BlockSpec((tt, 2 * nq), lambda i: (i, 0)),
            pl.BlockSpec((tt, nq), lambda i: (pos(i), 0)),
            pl.BlockSpec((tt, nq), lambda i: (pos(i), 0)),
            pl.BlockSpec((1, nq), lambda i: (0, 0)),
            pl.BlockSpec((1, nq // 2), lambda i: (0, 0)),
            pl.BlockSpec((nq, nq), lambda i: (0, 0)),
        ],
        out_specs=[
            pl.BlockSpec((None, GQA_Q_HEADS, tt, hd), lambda i: (bat(i), 0, pos(i), 0)),
            pl.BlockSpec((None, GQA_KV_HEADS, tt, hd), lambda i: (bat(i), 0, pos(i), 0)),
            pl.BlockSpec((None, GQA_KV_HEADS, tt, hd), lambda i: (bat(i), 0, pos(i), 0)),
        ],
        out_shape=[
            jax.ShapeDtypeStruct((n_batch, GQA_Q_HEADS, s_len, hd), BF16),
            jax.ShapeDtypeStruct((n_batch, GQA_KV_HEADS, s_len, hd), BF16),
            jax.ShapeDtypeStruct((n_batch, GQA_KV_HEADS, s_len, hd), BF16),
        ],
        compiler_params=_cparams(("parallel",)),
        name="gqa_prep",
    )(pattn, cos, sin, jnp.tile(gq.astype(F32), GQA_Q_HEADS).reshape(1, nq),
      jnp.tile(gk.astype(F32), GQA_KV_HEADS).reshape(1, nq // 2), gmat)


def _flash_kernel(q_ref, k_ref, v_ref, o_ref, m_s, l_s, acc_s, *, nk):
    ki = pl.program_id(3)
    g, tq, hd = q_ref.shape

    @pl.when(ki == 0)
    def _():
        m_s[...] = jnp.full_like(m_s, -jnp.inf)
        l_s[...] = jnp.zeros_like(l_s)
        acc_s[...] = jnp.zeros_like(acc_s)

    q = q_ref[...].reshape(g * tq, hd)
    s = _dot_nt(q, k_ref[...])
    m_prev = m_s[...]
    m_new = jnp.maximum(m_prev, jnp.max(s, axis=-1, keepdims=True))
    alpha = jnp.exp(m_prev - m_new)
    p = jnp.exp(s - m_new)
    l_s[...] = alpha * l_s[...] + jnp.sum(p, axis=-1, keepdims=True)
    acc_s[...] = alpha * acc_s[...] + _dot(p.astype(BF16), v_ref[...])
    m_s[...] = m_new

    @pl.when(ki == nk - 1)
    def _():
        o = acc_s[...] / l_s[...]
        o_ref[...] = jnp.concatenate([o[i * tq:(i + 1) * tq] for i in range(g)], axis=1)


def gqa_flash(q, k, v, tq, q_blk0, nq, tk, k_blk0, nk):
    n_batch = q.shape[0]
    hd = HEAD_DIM
    g = GQA_GROUP
    return pl.pallas_call(
        functools.partial(_flash_kernel, nk=nk),
        grid=(n_batch, GQA_KV_HEADS, nq, nk),
        in_specs=[
            pl.BlockSpec((None, g, tq, hd), lambda b, h, i, j: (b, h, q_blk0 + i, 0)),
            pl.BlockSpec((None, None, tk, hd), lambda b, h, i, j: (b, h, k_blk0 + j, 0)),
            pl.BlockSpec((None, None, tk, hd), lambda b, h, i, j: (b, h, k_blk0 + j, 0)),
        ],
        out_specs=pl.BlockSpec((tq, g * hd), lambda b, h, i, j: (b * nq + i, h)),
        out_shape=jax.ShapeDtypeStruct((n_batch * nq * tq, GQA_Q_HEADS * hd), F32),
        scratch_shapes=[
            pltpu.VMEM((g * tq, 1), F32),
            pltpu.VMEM((g * tq, 1), F32),
            pltpu.VMEM((g * tq, hd), F32),
        ],
        compiler_params=_cparams(("parallel", "parallel", "parallel", "arbitrary")),
        name="gqa_flash",
    )(q, k, v)


def _key_tile(s_len):
    n = s_len // SEQ_TILE
    best = 1
    for d in range(1, 5):
        if n % d == 0:
            best = d
    return best * SEQ_TILE


def kernel(x, c, ctx, c_ctx, w_ada, b_ada, g_ffn1, w_ffn1_in, w_ffn1_out, g_mix, w_in, w_out, s5_lambda_re, s5_lambda_im, s5_log_step, s5_b_re, s5_b_im, s5_c_re, s5_c_im, s5_d, s5_w_glu, na_rpb, gqa_q_norm, gqa_k_norm, lru_conv_w, lru_conv_b, lru_w_a, lru_b_a, lru_w_x, lru_b_x, lru_lambda, g_ffn2, w_ffn2_in, w_ffn2_out, g_final):
    n_batch, t_lat, d = x.shape
    t_ctx = ctx.shape[1]
    depth = w_ada.shape[0]
    assert d == D_MODEL and t_lat % TOKEN_TILE == 0 and (n_batch * t_ctx) % TOKEN_TILE == 0
    assert t_lat % t_ctx == 0 and t_ctx % SEQ_TILE == 0 and t_lat % FLASH_TQ == 0
    n_lat = n_batch * t_lat
    n_all = n_lat + n_batch * t_ctx
    tiles_per_type = t_lat // TOKEN_TILE
    assert (n_batch * t_ctx) // TOKEN_TILE <= tiles_per_type

    c8 = jnp.zeros((8, D_MODEL), F32).at[:n_batch].set(c.astype(F32)).at[n_batch].set(c_ctx.astype(F32))
    mods_all = ada_mods(c8, w_ada, b_ada).reshape(depth, 8, N_MOD, D_MODEL)

    h = jnp.concatenate([x.reshape(n_lat, D_MODEL), ctx.reshape(n_batch * t_ctx, D_MODEL)], axis=0)
    s_len = t_lat + t_ctx
    tk = _key_tile(s_len)
    ncl, ncc = t_lat // S5_CHUNK, t_ctx // S5_CHUNK

    for l in range(depth):
        need_ctx = l < depth - 1
        mods = mods_all[l]
        w1i, w1o = w_ffn1_in[l].astype(BF16), w_ffn1_out[l].astype(BF16)
        w2i, w2o = w_ffn2_in[l].astype(BF16), w_ffn2_out[l].astype(BF16)
        wl = w_in[l]
        w_in_perm = jnp.concatenate([wl[:, 0:256], wl[:, 1536:2048], wl[:, 1024:1536], wl[:, 256:1024]],
                                    axis=1).astype(BF16)

        h = ffn_half(h, mods, g_ffn1[l], w1i, w1o, 0, n_all, tiles_per_type)
        pscan, pattn = in_projection(h, mods, g_mix[l], w_in_perm, tiles_per_type)

        tables = _s5_tables(s5_lambda_re[l], s5_lambda_im[l], s5_log_step[l], s5_b_re[l], s5_b_im[l],
                            s5_c_re[l], s5_c_im[l])
        u_t = _s5_to_chunks(pscan[:, :GROUP_WIDTH], n_batch, t_lat, t_ctx)
        ys5 = _s5_from_chunks(s5_scan(u_t, tables, n_batch, ncl, ncc), n_batch, t_lat, t_ctx, need_ctx)

        lru_args = lambda dr: (lru_conv_w[l], lru_conv_b[l], _block_diag(lru_w_a[l, dr]).astype(BF16), lru_b_a[l, dr],
                               _block_diag(lru_w_x[l, dr]).astype(BF16), lru_b_x[l, dr], lru_lambda[l, dr])
        hf = lru_scan(pscan, *lru_args(0), False, n_batch, t_lat, t_ctx)
        hr = lru_scan(pscan, *lru_args(1), True, n_batch, t_lat, t_ctx)

        yb = natten_latent(pattn, _natten_bias(na_rpb[l]), n_batch, t_lat, t_ctx)

        qh, kh, vh = gqa_prep(pattn, gqa_q_norm[l], gqa_k_norm[l], n_batch, t_lat, t_ctx)
        yc = gqa_flash(qh, kh, vh, FLASH_TQ, 0, t_lat // FLASH_TQ, tk, 0, s_len // tk)

        if need_ctx:
            yb = jnp.concatenate([yb, natten_context(pattn, n_batch, t_lat, t_ctx)], axis=0)
            yc_ctx = gqa_flash(qh, kh, vh, t_ctx, t_lat // t_ctx, 1, t_ctx, t_lat // t_ctx, 1)
            yc = jnp.concatenate([yc, yc_ctx], axis=0)

        n_rows = n_all if need_ctx else n_lat
        h = out_projection(h, mods, ys5, pscan, hf, hr, yb, yc, s5_d[l], s5_w_glu[l].astype(BF16),
                           w_out[l].astype(BF16), n_rows, tiles_per_type)
        h = ffn_half(h, mods, g_ffn2[l], w2i, w2o, 6, n_rows, tiles_per_type,
                     g_final=None if need_ctx else g_final)
    return h.reshape(n_batch, t_lat, D_MODEL)
```

```python
import functools
import math

import numpy as np
import jax
import jax.numpy as jnp
from jax import lax
from jax.experimental import pallas as pl
from jax.experimental.pallas import tpu as pltpu

F32 = jnp.float32
BF16 = jnp.bfloat16

D_MODEL = 1024
GRID_W = 64
HEAD_DIM = 64
GROUP_WIDTH = D_MODEL // 4
S5_CH = 16
S5_GROUPS = GROUP_WIDTH // S5_CH
S5_STATE = 64
NA_HEADS = GROUP_WIDTH // HEAD_DIM
NA_KR = 8
NA_KC = 16
GQA_Q_HEADS = GROUP_WIDTH // HEAD_DIM
GQA_KV_HEADS = GQA_Q_HEADS // 2
GQA_GROUP = GQA_Q_HEADS // GQA_KV_HEADS
LRU_BLOCKS = GROUP_WIDTH // HEAD_DIM
LRU_CONV = 4
LRU_C = 8.0
D_FF = ((8 * D_MODEL // 3 + 127) // 128) * 128
FFN_RES = 0.5
ROPE_BASE = 10000.0
EPS = 1e-6
N_MOD = 9

TOKEN_TILE = 512
SEQ_TILE = 256
S5_CHUNK = 32
S5_OCT = 8
MXU_DEPTH = 256
FF_CHUNKS = ((0, 6 * MXU_DEPTH), (6 * MXU_DEPTH, D_FF))
FLASH_TQ = 512
KEY_CHUNK = 256
NA_QROWS = NA_KR // 2
NA_BAND = NA_KR + NA_QROWS
LOG2E = math.log2(math.e)
NEG_BIG = -1e30
VMEM_LIMIT = 56 * 1024 * 1024


def _cparams(sem):
    return pltpu.CompilerParams(dimension_semantics=sem, vmem_limit_bytes=VMEM_LIMIT)


def _dot(a, b):
    return jnp.dot(a, b, preferred_element_type=F32)


def _dot_nt(a, b):
    return lax.dot_general(a, b, (((1,), (1,)), ((), ())), preferred_element_type=F32)


def _ada_kernel(c_ref, w_ref, b_ref, o_ref):
    c = c_ref[...]
    s = c * jax.nn.sigmoid(c)
    o_ref[...] = _dot(s.astype(BF16), w_ref[...].astype(BF16)) + b_ref[...]


def ada_mods(c8, w_ada, b_ada):
    L = w_ada.shape[0]
    tn = D_MODEL
    return pl.pallas_call(
        _ada_kernel,
        grid=(L, N_MOD * D_MODEL // tn),
        in_specs=[
            pl.BlockSpec((8, D_MODEL), lambda l, j: (0, 0)),
            pl.BlockSpec((None, D_MODEL, tn), lambda l, j: (l, 0, j)),
            pl.BlockSpec((None, 1, tn), lambda l, j: (l, 0, j)),
        ],
        out_specs=pl.BlockSpec((None, 8, tn), lambda l, j: (l, 0, j)),
        out_shape=jax.ShapeDtypeStruct((L, 8, N_MOD * D_MODEL), F32),
        compiler_params=_cparams(("parallel", "parallel")),
        name="ada_mods",
    )(c8, w_ada, b_ada.reshape(L, 1, N_MOD * D_MODEL))


def _norm_mod(x, g, shift, scale):
    ms = jnp.mean(x * x, axis=-1, keepdims=True)
    y = x * lax.rsqrt(ms + EPS) * g
    return y * (1.0 + scale) + shift


def _ffn_kernel(x_ref, mod_ref, g_ref, wi_ref, wo_ref, *rest, k0, final):
    if final:
        gf_ref, o_ref = rest
    else:
        (o_ref,) = rest
    x = x_ref[...]
    y = _norm_mod(x, g_ref[...], mod_ref[k0:k0 + 1, :], mod_ref[k0 + 1:k0 + 2, :]).astype(BF16)
    acc = jnp.zeros(x.shape, F32)
    for lo, hi in FF_CHUNKS:
        a = _dot(y, wi_ref[:, lo:hi])
        b = _dot(y, wi_ref[:, D_FF + lo:D_FF + hi])
        h = (a * jax.nn.sigmoid(a) * b).astype(BF16)
        acc = acc + _dot(h, wo_ref[lo:hi, :])
    out = x + FFN_RES * mod_ref[k0 + 2:k0 + 3, :] * acc
    if final:
        ms = jnp.mean(out * out, axis=-1, keepdims=True)
        out = out * lax.rsqrt(ms + EPS) * gf_ref[...]
    o_ref[...] = out


def ffn_half(x, mods, g, w_i, w_o, k0, n_rows, tiles_per_type, g_final=None):
    final = g_final is not None
    nt = n_rows // TOKEN_TILE
    in_specs = [
        pl.BlockSpec((TOKEN_TILE, D_MODEL), lambda i: (i, 0)),
        pl.BlockSpec((None, N_MOD, D_MODEL), lambda i: (i // tiles_per_type, 0, 0)),
        pl.BlockSpec((1, D_MODEL), lambda i: (0, 0)),
        pl.BlockSpec((D_MODEL, 2 * D_FF), lambda i: (0, 0)),
        pl.BlockSpec((D_FF, D_MODEL), lambda i: (0, 0)),
    ]
    args = [x, mods, g.reshape(1, D_MODEL), w_i, w_o]
    if final:
        in_specs.append(pl.BlockSpec((1, D_MODEL), lambda i: (0, 0)))
        args.append(g_final.reshape(1, D_MODEL))
    return pl.pallas_call(
        functools.partial(_ffn_kernel, k0=k0, final=final),
        grid=(nt,),
        in_specs=in_specs,
        out_specs=pl.BlockSpec((TOKEN_TILE, D_MODEL), lambda i: (i, 0)),
        out_shape=jax.ShapeDtypeStruct((n_rows, D_MODEL), F32),
        compiler_params=_cparams(("parallel",)),
        name="ffn_half",
    )(*args)


def _inproj_kernel(x_ref, mod_ref, g_ref, w_ref, ps_ref, pa_ref):
    y = _norm_mod(x_ref[...], g_ref[...], mod_ref[3:4, :], mod_ref[4:5, :]).astype(BF16)
    p = _dot(y, w_ref[...])
    ns = ps_ref.shape[1]
    ps_ref[...] = p[:, :ns]
    pa_ref[...] = p[:, ns:].astype(BF16)


def in_projection(x, mods, g, w_in_perm, tiles_per_type):
    n = x.shape[0]
    n_scan = 3 * GROUP_WIDTH
    n_attn = w_in_perm.shape[1] - n_scan
    return pl.pallas_call(
        _inproj_kernel,
        grid=(n // TOKEN_TILE,),
        in_specs=[
            pl.BlockSpec((TOKEN_TILE, D_MODEL), lambda i: (i, 0)),
            pl.BlockSpec((None, N_MOD, D_MODEL), lambda i: (i // tiles_per_type, 0, 0)),
            pl.BlockSpec((1, D_MODEL), lambda i: (0, 0)),
            pl.BlockSpec((D_MODEL, n_scan + n_attn), lambda i: (0, 0)),
        ],
        out_specs=[
            pl.BlockSpec((TOKEN_TILE, n_scan), lambda i: (i, 0)),
            pl.BlockSpec((TOKEN_TILE, n_attn), lambda i: (i, 0)),
        ],
        out_shape=[
            jax.ShapeDtypeStruct((n, n_scan), F32),
            jax.ShapeDtypeStruct((n, n_attn), BF16),
        ],
        compiler_params=_cparams(("parallel",)),
        name="in_projection",
    )(x, mods, g.reshape(1, D_MODEL), w_in_perm)


def _outproj_kernel(h_ref, mod_ref, ys5_ref, u_ref, hf_ref, hr_ref, gl_ref, yb_ref, yc_ref,
                    dsk_ref, wglu_ref, wo_ref, o_ref):
    gw = GROUP_WIDTH
    ya = jax.nn.gelu(ys5_ref[...] + dsk_ref[...] * u_ref[...])
    ya = ya * jax.nn.sigmoid(_dot(ya.astype(BF16), wglu_ref[...]))
    yd = (hf_ref[...] + hr_ref[...]) * jax.nn.gelu(gl_ref[...])
    acc = _dot(ya.astype(BF16), wo_ref[0:gw, :])
    acc = acc + _dot(yb_ref[...].astype(BF16), wo_ref[gw:2 * gw, :])
    acc = acc + _dot(yc_ref[...].astype(BF16), wo_ref[2 * gw:3 * gw, :])
    acc = acc + _dot(yd.astype(BF16), wo_ref[3 * gw:4 * gw, :])
    o_ref[...] = h_ref[...] + mod_ref[5:6, :] * acc


def out_projection(h, mods, ys5, pscan, hf, hr, yb, yc, d_skip, w_glu, w_out, n_rows, tiles_per_type):
    gw = GROUP_WIDTH
    row = lambda i: (i, 0)
    return pl.pallas_call(
        _outproj_kernel,
        grid=(n_rows // TOKEN_TILE,),
        in_specs=[
            pl.BlockSpec((TOKEN_TILE, D_MODEL), row),
            pl.BlockSpec((None, N_MOD, D_MODEL), lambda i: (i // tiles_per_type, 0, 0)),
            pl.BlockSpec((TOKEN_TILE, gw), row),
            pl.BlockSpec((TOKEN_TILE, gw), lambda i: (i, 0)),
            pl.BlockSpec((TOKEN_TILE, gw), row),
            pl.BlockSpec((TOKEN_TILE, gw), row),
            pl.BlockSpec((TOKEN_TILE, gw), lambda i: (i, 2)),
            pl.BlockSpec((TOKEN_TILE, gw), row),
            pl.BlockSpec((TOKEN_TILE, gw), row),
            pl.BlockSpec((1, gw), lambda i: (0, 0)),
            pl.BlockSpec((gw, gw), lambda i: (0, 0)),
            pl.BlockSpec((D_MODEL, D_MODEL), lambda i: (0, 0)),
        ],
        out_specs=pl.BlockSpec((TOKEN_TILE, D_MODEL), row),
        out_shape=jax.ShapeDtypeStruct((n_rows, D_MODEL), F32),
        compiler_params=_cparams(("parallel",)),
        name="out_projection",
    )(h, mods, ys5, pscan, hf, hr, pscan, yb, yc, d_skip.reshape(1, gw), w_glu, w_out)


def _s5_tables(lam_re, lam_im, log_step, b_re, b_im, c_re, c_im):
    L = S5_CHUNK
    G, P, H = S5_GROUPS, S5_STATE, S5_CH
    hp = lax.Precision.HIGHEST
    outs = []
    for dr in range(2):
        lr, li = lam_re[dr].astype(F32), lam_im[dr].astype(F32)
        dt = jnp.exp(log_step[dr].astype(F32))[:, None]
        mag = jnp.exp(lr * dt)
        ar, ai = mag * jnp.cos(li * dt), mag * jnp.sin(li * dt)
        den = lr * lr + li * li
        fr = ((ar - 1) * lr + ai * li) / den
        fi = (ai * lr - (ar - 1) * li) / den
        br, bi = b_re[dr].astype(F32), b_im[dr].astype(F32)
        bbr = fr[..., None] * br - fi[..., None] * bi
        bbi = fr[..., None] * bi + fi[..., None] * br
        cr, ci = c_re[dr].astype(F32), c_im[dr].astype(F32)
        pr, pi = [jnp.ones_like(ar)], [jnp.zeros_like(ar)]
        for _ in range(L):
            pr.append(pr[-1] * ar - pi[-1] * ai)
            pi.append(pr[-2] * ai + pi[-1] * ar)
        pw_r, pw_i = jnp.stack(pr), jnp.stack(pi)
        abr = pw_r[:L, :, :, None] * bbr[None] - pw_i[:L, :, :, None] * bbi[None]
        abi = pw_r[:L, :, :, None] * bbi[None] + pw_i[:L, :, :, None] * bbr[None]
        kk = (jnp.einsum('gop,lgpi->lgoi', cr, abr, precision=hp)
              - jnp.einsum('gop,lgpi->lgoi', ci, abi, precision=hp))
        tt = np.arange(L)
        lag = (tt[None, :] - tt[:, None]) if dr == 0 else (tt[:, None] - tt[None, :])
        onehot = (lag[None] == tt[:, None, None]).astype(np.float32)
        mt = jnp.einsum('lgoi,lst->gsito', kk, onehot, precision=hp).reshape(G, L * H, L * H)
        abr_s, abi_s = (abr[::-1], abi[::-1]) if dr == 0 else (abr, abi)
        bsr = jnp.transpose(abr_s, (1, 0, 3, 2)).reshape(G, L * H, P)
        bsi = jnp.transpose(abi_s, (1, 0, 3, 2)).reshape(G, L * H, P)
        qr, qi = pw_r[1:], pw_i[1:]
        if dr == 1:
            qr, qi = qr[::-1], qi[::-1]
        ccr = cr[None] * qr[:, :, None, :] - ci[None] * qi[:, :, None, :]
        cci = -cr[None] * qi[:, :, None, :] - ci[None] * qr[:, :, None, :]
        ccr = jnp.transpose(ccr, (1, 3, 0, 2)).reshape(G, P, L * H)
        cci = jnp.transpose(cci, (1, 3, 0, 2)).reshape(G, P, L * H)
        outs.append((mt, bsr, bsi, ccr, cci, pw_r[L], pw_i[L]))
    (mt0, bsr0, bsi0, ccr0, cci0, alr0, ali0), (mt1, bsr1, bsi1, ccr1, cci1, alr1, ali1) = outs
    mt = (mt0 + mt1).astype(BF16)
    bs = jnp.concatenate([bsr0, bsi0, bsr1, bsi1], axis=-1).astype(BF16)
    cc = jnp.stack([ccr0, cci0, ccr1, cci1], axis=1).astype(BF16)
    al = jnp.stack([alr0, ali0, alr1, ali1])
    return mt, bs, cc, al


def _s5_select_tables():
    H, O = S5_CH, S5_OCT
    pack = np.zeros((O // 2, O * 128, 256), np.float32)
    unpack = np.zeros((O // 2, O * 128, 256), np.float32)
    for j in range(8):
        for g in range(O):
            for h in range(H):
                pack[g // 2, j * 128 + g * H + h, (g % 2) * 128 + j * H + h] = 1.0
                unpack[j // 2, g * 128 + j * H + h, (j % 2) * 128 + g * H + h] = 1.0
    return jnp.asarray(pack, BF16), jnp.asarray(unpack, BF16)


def _s5_kernel(ul_ref, uc_ref, pack_ref, unpack_ref, mt_ref, bs_ref, cc_ref, al_ref, ol_ref, oc_ref,
               ug_s, y_s, s_s, h_s, *, ncl, ncc):
    L, P, O = S5_CHUNK, S5_STATE, S5_OCT
    nc = ncl + ncc
    nblk = L // 8

    for tb in range(nblk):
        zs = [jnp.concatenate([ul_ref[pl.ds(8 * tb + j, ncl, stride=L), :],
                               uc_ref[pl.ds(8 * tb + j, ncc, stride=L), :]], axis=0) for j in range(8)]
        zc = jnp.concatenate(zs, axis=1).astype(BF16)
        for gp in range(O // 2):
            r = _dot(zc, pack_ref[gp])
            ug_s[2 * gp, :, tb * 128:(tb + 1) * 128] = r[:, :128]
            ug_s[2 * gp + 1, :, tb * 128:(tb + 1) * 128] = r[:, 128:]

    for g in range(O):
        ug = ug_s[g].astype(BF16)
        y_s[g] = _dot(ug, mt_ref[g])
        st = _dot(ug, bs_ref[g])
        for k in range(4):
            s_s[k, :, g, :] = st[:, k * P:(k + 1) * P]

    afr, afi, arr, ari = al_ref[0], al_ref[1], al_ref[2], al_ref[3]

    def step(j, carry):
        hfr, hfi, hrr, hri = carry
        cf = jnp.where(j < ncc, ncl + j, j - ncc)
        cr = nc - 1 - j
        h_s[0, cf] = hfr
        h_s[1, cf] = hfi
        h_s[2, cr] = hrr
        h_s[3, cr] = hri
        return (afr * hfr - afi * hfi + s_s[0, cf], afr * hfi + afi * hfr + s_s[1, cf],
                arr * hrr - ari * hri + s_s[2, cr], arr * hri + ari * hrr + s_s[3, cr])

    zero = jnp.zeros((O, P), F32)
    lax.fori_loop(0, nc, step, (zero, zero, zero, zero))

    for g in range(O):
        y = y_s[g]
        for k in range(4):
            y = y + _dot(h_s[k, :, g, :].astype(BF16), cc_ref[g, k])
        y_s[g] = y

    for tb in range(nblk):
        yc = jnp.concatenate([y_s[g, :, tb * 128:(tb + 1) * 128] for g in range(O)], axis=1)
        hi = yc.astype(BF16)
        lo = (yc - hi.astype(F32)).astype(BF16)
        for jp in range(4):
            z = _dot(hi, unpack_ref[jp]) + _dot(lo, unpack_ref[jp])
            for e in range(2):
                t = 8 * tb + 2 * jp + e
                zt = z[:, e * 128:(e + 1) * 128]
                ol_ref[pl.ds(t, ncl, stride=L), :] = zt[:ncl]
                oc_ref[pl.ds(t, ncc, stride=L), :] = zt[ncl:]


def s5_scan(pscan, tables, n_batch, t_lat, t_ctx):
    mt, bs, cc, al = tables
    pack, unpack = _s5_select_tables()
    L, P, O = S5_CHUNK, S5_STATE, S5_OCT
    W = L * S5_CH
    ncl, ncc = t_lat // L, t_ctx // L
    nc = ncl + ncc
    ctx0 = n_batch * t_lat // t_ctx
    once = pl.Buffered(1)
    return pl.pallas_call(
        functools.partial(_s5_kernel, ncl=ncl, ncc=ncc),
        grid=(S5_GROUPS // O, n_batch),
        in_specs=[
            pl.BlockSpec((t_lat, O * S5_CH), lambda o, b: (b, o)),
            pl.BlockSpec((t_ctx, O * S5_CH), lambda o, b: (ctx0 + b, o)),
            pl.BlockSpec(pack.shape, lambda o, b: (0, 0, 0), pipeline_mode=once),
            pl.BlockSpec(unpack.shape, lambda o, b: (0, 0, 0), pipeline_mode=once),
            pl.BlockSpec((O, W, W), lambda o, b: (o, 0, 0), pipeline_mode=once),
            pl.BlockSpec((O, W, 4 * P), lambda o, b: (o, 0, 0), pipeline_mode=once),
            pl.BlockSpec((O, 4, P, W), lambda o, b: (o, 0, 0, 0), pipeline_mode=once),
            pl.BlockSpec((4, O, P), lambda o, b: (0, o, 0)),
        ],
        out_specs=[
            pl.BlockSpec((t_lat, O * S5_CH), lambda o, b: (b, o)),
            pl.BlockSpec((t_ctx, O * S5_CH), lambda o, b: (b, o)),
        ],
        out_shape=[
            jax.ShapeDtypeStruct((n_batch * t_lat, GROUP_WIDTH), F32),
            jax.ShapeDtypeStruct((n_batch * t_ctx, GROUP_WIDTH), F32),
        ],
        scratch_shapes=[
            pltpu.VMEM((O, nc, W), F32),
            pltpu.VMEM((O, nc, W), F32),
            pltpu.VMEM((4, nc, O, P), F32),
            pltpu.VMEM((4, nc, O, P), F32),
        ],
        compiler_params=_cparams(("arbitrary", "arbitrary")),
        name="s5_scan",
    )(pscan, pscan, pack, unpack, mt, bs, cc, al)


def _lru_kernel(xp_ref, xc_ref, xn_ref, cw_ref, cb_ref, wa_ref, ba_ref, wx_ref, bx_ref, lam_ref,
                o_ref, xs_s, a_s, b_s, h_s, *, reverse, ntl, ntc):
    tt = SEQ_TILE
    j = pl.program_id(1)
    is_ctx = j < ntc
    if reverse:
        k = jnp.where(is_ctx, ntc - 1 - j, ntl - 1 - (j - ntc))
    else:
        k = jnp.where(is_ctx, j, j - ntc)
    seg = jnp.where(is_ctx, ntc, ntl)
    has_prev = (k > 0).astype(F32)
    has_next = (k < seg - 1).astype(F32)

    @pl.when(j == 0)
    def _():
        h_s[...] = jnp.zeros_like(h_s)

    xs_s[0:8, :] = xp_ref[tt - 8:tt, :] * has_prev
    xs_s[8:8 + tt, :] = xc_ref[...]
    xs_s[8 + tt:16 + tt, :] = xn_ref[0:8, :] * has_next
    xc = cb_ref[...] + jnp.zeros((tt, GROUP_WIDTH), F32)
    for tap in range(LRU_CONV):
        xc = xc + cw_ref[tap:tap + 1, :] * xs_s[pl.ds(6 + tap, tt), :]
    xb = xc.astype(BF16)
    r = jax.nn.sigmoid(_dot(xb, wa_ref[...]) + ba_ref[...])
    i = jax.nn.sigmoid(_dot(xb, wx_ref[...]) + bx_ref[...])
    z = -lam_ref[...]
    softplus = jnp.maximum(z, 0.0) + jnp.log1p(jnp.exp(-jnp.abs(z)))
    log_a = -LRU_C * r * softplus
    a_s[...] = jnp.exp(log_a)
    th = jnp.tanh(log_a)
    b_s[...] = jnp.sqrt(-2.0 * th / (1.0 - th)) * (i * xc)

    rows = lax.broadcasted_iota(jnp.int32, (8, GROUP_WIDTH), 0)
    nblk = tt // 8

    def block(q, h):
        blk = (nblk - 1 - q) if reverse else q
        off = pl.multiple_of(blk * 8, 8)
        a = a_s[pl.ds(off, 8), :]
        b = b_s[pl.ds(off, 8), :]
        for sh in (1, 2, 4):
            if reverse:
                keep = rows < 8 - sh
                a_sh = jnp.where(keep, pltpu.roll(a, 8 - sh, 0), 1.0)
                b_sh = jnp.where(keep, pltpu.roll(b, 8 - sh, 0), 0.0)
            else:
                keep = rows >= sh
                a_sh = jnp.where(keep, pltpu.roll(a, sh, 0), 1.0)
                b_sh = jnp.where(keep, pltpu.roll(b, sh, 0), 0.0)
            b = a * b_sh + b
            a = a * a_sh
        hb = a * h + b
        o_ref[pl.ds(off, 8), :] = hb
        last = hb[0:1, :] if reverse else hb[7:8, :]
        return jnp.broadcast_to(last, (8, GROUP_WIDTH))

    h_s[...] = lax.fori_loop(0, nblk, block, h_s[...])


def lru_scan(pscan, conv_w, conv_b, w_a, b_a, w_x, b_x, lam, reverse, n_batch, t_lat, t_ctx):
    tt = SEQ_TILE
    gw = GROUP_WIDTH
    ntl, ntc = t_lat // tt, t_ctx // tt
    n = pscan.shape[0]

    def tile(b, j, d):
        is_ctx = j < ntc
        if reverse:
            k = jnp.where(is_ctx, ntc - 1 - j, ntl - 1 - (j - ntc))
        else:
            k = jnp.where(is_ctx, j, j - ntc)
        seg = jnp.where(is_ctx, ntc, ntl)
        k = jnp.clip(k + d, 0, seg - 1)
        return jnp.where(is_ctx, n_batch * ntl + b * ntc + k, b * ntl + k)

    vec = pl.BlockSpec((1, gw), lambda b, j: (0, 0))
    mat = pl.BlockSpec((gw, gw), lambda b, j: (0, 0))
    return pl.pallas_call(
        functools.partial(_lru_kernel, reverse=reverse, ntl=ntl, ntc=ntc),
        grid=(n_batch, ntl + ntc),
        in_specs=[
            pl.BlockSpec((tt, gw), lambda b, j: (tile(b, j, -1), 1)),
            pl.BlockSpec((tt, gw), lambda b, j: (tile(b, j, 0), 1)),
            pl.BlockSpec((tt, gw), lambda b, j: (tile(b, j, 1), 1)),
            pl.BlockSpec((LRU_CONV, gw), lambda b, j: (0, 0)),
            vec, mat, vec, mat, vec, vec,
        ],
        out_specs=pl.BlockSpec((tt, gw), lambda b, j: (tile(b, j, 0), 0)),
        out_shape=jax.ShapeDtypeStruct((n, gw), F32),
        scratch_shapes=[
            pltpu.VMEM((tt + 16, gw), F32),
            pltpu.VMEM((tt, gw), F32),
            pltpu.VMEM((tt, gw), F32),
            pltpu.VMEM((8, gw), F32),
        ],
        compiler_params=_cparams(("parallel", "arbitrary")),
        name="lru_rev" if reverse else "lru_fwd",
    )(pscan, pscan, pscan, conv_w, conv_b.reshape(1, gw), w_a, b_a.reshape(1, gw), w_x, b_x.reshape(1, gw),
      lam.reshape(1, gw))


def _block_diag(w):
    n, d, e = w.shape
    eye = jnp.eye(n, dtype=w.dtype)
    return (eye[:, None, :, None] * w[:, :, None, :]).reshape(n * d, n * e)


def _natten_bias(rpb):
    W, KR, KC, QR, NB = GRID_W, NA_KR, NA_KC, NA_QROWS, NA_BAND
    col = np.arange(W)
    cs = np.clip(col - KC // 2, 0, W - KC)
    inwin = (col[None, :] >= cs[:, None]) & (col[None, :] < cs[:, None] + KC)
    coff = np.clip(col[None, :] - col[:, None] + (KC - 1), 0, 2 * KC - 2)
    a = np.arange(QR)[:, None]
    i = np.arange(NB)[None, :]
    first = (i - a, (i < KR) & (a >= 0))
    mid = (i - a - KR // 2, (i >= a) & (i < a + KR))
    last = (i - a - NB + QR, (i >= NB - KR) & (a >= 0))
    roh = np.zeros((3, QR, NB, 2 * KR - 1), np.float32)
    mask = np.zeros((3, 1, QR, W, NB, W), np.float32)
    for v, (delta, valid) in enumerate((first, mid, last)):
        roff = np.clip(delta + (KR - 1), 0, 2 * KR - 2)
        roh[v, a, i, roff] = 1.0
        ok = valid[:, None, :, None] & inwin[None, :, None, :]
        mask[v, 0] = np.where(ok, 0.0, NEG_BIG)
    coh = (coff[:, :, None] == np.arange(2 * KC - 1)).astype(np.float32)
    b = jnp.einsum('hrc,vair,qkc->vhaqik', rpb.astype(F32), roh, coh, precision=lax.Precision.HIGHEST)
    return (b * LOG2E + mask).reshape(3, rpb.shape[0], QR * W, NB * W)


def _natten_kernel(q_ref, k_ref, v_ref, kc_ref, vc_ref, bias_ref, o_ref, *, rows):
    W, KR, hd = GRID_W, NA_KR, HEAD_DIM
    blk = pl.program_id(1)
    nblk = rows // NA_QROWS
    bs = jnp.clip(blk * NA_QROWS - KR // 2, 0, rows - NA_BAND)
    var = jnp.where(blk == 0, 0, jnp.where(blk == nblk - 1, 2, 1))
    start = pl.multiple_of(bs * W, W)
    q = q_ref[...]
    kw = k_ref[pl.ds(start, NA_BAND * W), :]
    vw = v_ref[pl.ds(start, NA_BAND * W), :]
    kc = kc_ref[...]
    vc = vc_ref[...]
    scale2 = (hd ** -0.5) * LOG2E
    for h in range(NA_HEADS):
        sl = slice(h * hd, (h + 1) * hd)
        qh = q[:, sl]
        s = _dot_nt(qh, kw[:, sl]) * scale2 + bias_ref[var, h]
        sc = _dot_nt(qh, kc[:, sl]) * scale2
        m = jnp.maximum(jnp.max(s, axis=-1, keepdims=True), jnp.max(sc, axis=-1, keepdims=True))
        p = jnp.exp2(s - m)
        pc = jnp.exp2(sc - m)
        l = jnp.sum(p, axis=-1, keepdims=True) + jnp.sum(pc, axis=-1, keepdims=True)
        o = _dot(p.astype(BF16), vw[:, sl]) + _dot(pc.astype(BF16), vc[:, sl])
        o_ref[:, sl] = o / l


def natten_latent(pattn, bias, n_batch, t_lat, t_ctx):
    gw = GROUP_WIDTH
    rows = t_lat // GRID_W
    assert NA_QROWS == NA_KR // 2 and NA_BAND == NA_KR + NA_QROWS and rows % NA_QROWS == 0 and rows >= NA_BAND
    nblk = rows // NA_QROWS
    tq = NA_QROWS * GRID_W
    ctx0 = n_batch * t_lat // t_ctx
    return pl.pallas_call(
        functools.partial(_natten_kernel, rows=rows),
        grid=(n_batch, nblk),
        in_specs=[
            pl.BlockSpec((tq, gw), lambda b, r: (b * nblk + r, 2)),
            pl.BlockSpec((t_lat, gw), lambda b, r: (b, 3)),
            pl.BlockSpec((t_lat, gw), lambda b, r: (b, 4)),
            pl.BlockSpec((t_ctx, gw), lambda b, r: (ctx0 + b, 3)),
            pl.BlockSpec((t_ctx, gw), lambda b, r: (ctx0 + b, 4)),
            pl.BlockSpec(bias.shape, lambda b, r: (0, 0, 0, 0)),
        ],
        out_specs=pl.BlockSpec((tq, gw), lambda b, r: (b * nblk + r, 0)),
        out_shape=jax.ShapeDtypeStruct((n_batch * t_lat, gw), F32),
        compiler_params=_cparams(("parallel", "arbitrary")),
        name="natten_latent",
    )(pattn, pattn, pattn, pattn, pattn, bias)


def _ctx_attn_kernel(q_ref, k_ref, v_ref, o_ref):
    hd = HEAD_DIM
    q, k, v = q_ref[...], k_ref[...], v_ref[...]
    for h in range(NA_HEADS):
        sl = slice(h * hd, (h + 1) * hd)
        s = _dot_nt(q[:, sl], k[:, sl]) * (hd ** -0.5)
        m = jnp.max(s, axis=-1, keepdims=True)
        p = jnp.exp(s - m)
        l = jnp.sum(p, axis=-1, keepdims=True)
        o_ref[:, sl] = _dot(p.astype(BF16), v[:, sl]) / l


def natten_context(pattn, n_batch, t_lat, t_ctx):
    gw = GROUP_WIDTH
    ctx0 = n_batch * t_lat // t_ctx
    return pl.pallas_call(
        _ctx_attn_kernel,
        grid=(n_batch,),
        in_specs=[pl.BlockSpec((t_ctx, gw), lambda b, c=c: (ctx0 + b, c)) for c in (2, 3, 4)],
        out_specs=pl.BlockSpec((t_ctx, gw), lambda b: (b, 0)),
        out_shape=jax.ShapeDtypeStruct((n_batch * t_ctx, gw), F32),
        compiler_params=_cparams(("parallel",)),
        name="natten_context",
    )(pattn, pattn, pattn)


def _rope_tables(t_lat, t_ctx):
    half = HEAD_DIM // 4
    freqs = ROPE_BASE ** (-jnp.arange(half, dtype=F32) / half)
    t = jnp.arange(t_lat)
    ang_r = (t // GRID_W).astype(F32)[:, None] * freqs[None, :]
    ang_c = (t % GRID_W).astype(F32)[:, None] * freqs[None, :]
    ang = jnp.concatenate([ang_r, ang_r, ang_c, ang_c], axis=1)
    cos = jnp.concatenate([jnp.cos(ang), jnp.ones((t_ctx, HEAD_DIM), F32)], axis=0)
    sin = jnp.concatenate([jnp.sin(ang), jnp.zeros((t_ctx, HEAD_DIM), F32)], axis=0)
    return jnp.tile(cos, (1, GQA_Q_HEADS)), jnp.tile(sin, (1, GQA_Q_HEADS))


def _gqa_prep_kernel(x_ref, cos_ref, sin_ref, gq_ref, gk_ref, gm_ref, q_ref, k_ref, v_ref):
    hd = HEAD_DIM
    nq, nk = GQA_Q_HEADS * hd, GQA_KV_HEADS * hd
    x = x_ref[...].astype(F32)
    cos, sin = cos_ref[...], sin_ref[...]
    gm = gm_ref[...]

    def norm_rope(t, gain, width):
        t2 = t * t
        hi = t2.astype(BF16)
        lo = (t2 - hi.astype(F32)).astype(BF16)
        ms = _dot(hi, gm[:width, :width]) + _dot(lo, gm[:width, :width])
        tn = t * lax.rsqrt(ms + EPS) * gain
        lane = lax.broadcasted_iota(jnp.int32, tn.shape, 1)
        first = (lane % (hd // 2)) < (hd // 4)
        rot = jnp.where(first, -pltpu.roll(tn, width - hd // 4, 1), pltpu.roll(tn, hd // 4, 1))
        return tn * cos[:, :width] + rot * sin[:, :width]

    q = norm_rope(x[:, :nq], gq_ref[...], nq) * ((hd ** -0.5) * LOG2E)
    kt = norm_rope(x[:, nq:nq + nk], gk_ref[...], nk).T
    v = x_ref[:, nq + nk:nq + 2 * nk]
    for h in range(GQA_Q_HEADS):
        q_ref[h] = q[:, h * hd:(h + 1) * hd].astype(BF16)
    for h in range(GQA_KV_HEADS):
        k_ref[h] = kt[h * hd:(h + 1) * hd, :].astype(BF16)
        v_ref[h] = v[:, h * hd:(h + 1) * hd]


def gqa_prep(pattn, gq, gk, n_batch, t_lat, t_ctx):
    tt = SEQ_TILE
    assert tt == KEY_CHUNK
    hd = HEAD_DIM
    ntl, ntc = t_lat // tt, t_ctx // tt
    s_len = t_lat + t_ctx
    cos, sin = _rope_tables(t_lat, t_ctx)
    nq = GQA_Q_HEADS * hd
    gmat = _block_diag(jnp.full((GQA_Q_HEADS, hd, hd), 1.0 / hd, F32)).astype(BF16)

    def pos(i):
        is_ctx = i >= n_batch * ntl
        c = i - n_batch * ntl
        return jnp.where(is_ctx, ntl + c % ntc, i % ntl)

    def bat(i):
        is_ctx = i >= n_batch * ntl
        return jnp.where(is_ctx, (i - n_batch * ntl) // ntc, i // ntl)

    return pl.pallas_call(
        _gqa_prep_kernel,
        grid=(n_batch * (ntl + ntc),),
        in_specs=[
            pl.BlockSpec((tt, 2 * nq), lambda i: (i, 0)),
            pl.BlockSpec((tt, nq), lambda i: (pos(i), 0)),
            pl.BlockSpec((tt, nq), lambda i: (pos(i), 0)),
            pl.BlockSpec((1, nq), lambda i: (0, 0)),
            pl.BlockSpec((1, nq // 2), lambda i: (0, 0)),
            pl.BlockSpec((nq, nq), lambda i: (0, 0)),
        ],
        out_specs=[
            pl.BlockSpec((None, GQA_Q_HEADS, tt, hd), lambda i: (bat(i), 0, pos(i), 0)),
            pl.BlockSpec((None, GQA_KV_HEADS, None, hd, tt), lambda i: (bat(i), 0, pos(i), 0, 0)),
            pl.BlockSpec((None, GQA_KV_HEADS, tt, hd), lambda i: (bat(i), 0, pos(i), 0)),
        ],
        out_shape=[
            jax.ShapeDtypeStruct((n_batch, GQA_Q_HEADS, s_len, hd), BF16),
            jax.ShapeDtypeStruct((n_batch, GQA_KV_HEADS, s_len // tt, hd, tt), BF16),
            jax.ShapeDtypeStruct((n_batch, GQA_KV_HEADS, s_len, hd), BF16),
        ],
        compiler_params=_cparams(("parallel",)),
        name="gqa_prep",
    )(pattn, cos, sin, jnp.tile(gq.astype(F32), GQA_Q_HEADS).reshape(1, nq),
      jnp.tile(gk.astype(F32), GQA_KV_HEADS).reshape(1, nq // 2), gmat)


def _gqa_kernel(q_ref, kt_ref, v_ref, o_ref, s_s, m_s, l_s, acc_s, *, unroll):
    g, tq, hd = q_ref.shape
    nch, _, ck = kt_ref.shape
    nl = ck // 128
    q = q_ref[...].reshape(g * tq, hd)
    m_s[...] = jnp.full_like(m_s, -jnp.inf)

    def scores(c, carry):
        s = _dot(q, kt_ref[c])
        s_s[c] = s
        fold = s[:, 0:128]
        for j in range(1, nl):
            fold = jnp.maximum(fold, s[:, j * 128:(j + 1) * 128])
        m_s[...] = jnp.maximum(m_s[...], fold)
        return carry

    lax.fori_loop(0, nch, scores, 0, unroll=unroll)
    m_s[...] = jnp.broadcast_to(jnp.max(m_s[...], axis=-1, keepdims=True), m_s.shape)
    l_s[...] = jnp.zeros_like(l_s)
    acc_s[...] = jnp.zeros_like(acc_s)

    def values(c, carry):
        s = s_s[c]
        m = m_s[...]
        ps = [jnp.exp2(s[:, j * 128:(j + 1) * 128] - m) for j in range(nl)]
        tot = ps[0]
        for j in range(1, nl):
            tot = tot + ps[j]
        l_s[...] += tot
        p = jnp.concatenate(ps, axis=1).astype(BF16)
        off = pl.multiple_of(c * ck, ck)
        acc_s[...] += _dot(p, v_ref[pl.ds(off, ck), :])
        return carry

    lax.fori_loop(0, nch, values, 0, unroll=unroll)
    o = acc_s[...] / jnp.sum(l_s[...], axis=-1, keepdims=True)
    o_ref[...] = jnp.concatenate([o[i * tq:(i + 1) * tq] for i in range(g)], axis=1)


def gqa_attention(q, kt, v, tq, q_blk0, nq, c_blk, nch):
    n_batch = q.shape[0]
    hd = HEAD_DIM
    g = GQA_GROUP
    ck = KEY_CHUNK
    unroll = 3 if nch % 3 == 0 else 1
    return pl.pallas_call(
        functools.partial(_gqa_kernel, unroll=unroll),
        grid=(n_batch, GQA_KV_HEADS, nq),
        in_specs=[
            pl.BlockSpec((None, g, tq, hd), lambda b, h, i: (b, h, q_blk0 + i, 0)),
            pl.BlockSpec((None, None, nch, hd, ck), lambda b, h, i: (b, h, c_blk, 0, 0)),
            pl.BlockSpec((None, None, nch * ck, hd), lambda b, h, i: (b, h, c_blk, 0)),
        ],
        out_specs=pl.BlockSpec((tq, g * hd), lambda b, h, i: (b * nq + i, h)),
        out_shape=jax.ShapeDtypeStruct((n_batch * nq * tq, GQA_Q_HEADS * hd), F32),
        scratch_shapes=[
            pltpu.VMEM((nch, g * tq, ck), F32),
            pltpu.VMEM((g * tq, 128), F32),
            pltpu.VMEM((g * tq, 128), F32),
            pltpu.VMEM((g * tq, hd), F32),
        ],
        compiler_params=_cparams(("parallel", "parallel", "arbitrary")),
        name="gqa_attention",
    )(q, kt, v)


def kernel(x, c, ctx, c_ctx, w_ada, b_ada, g_ffn1, w_ffn1_in, w_ffn1_out, g_mix, w_in, w_out, s5_lambda_re, s5_lambda_im, s5_log_step, s5_b_re, s5_b_im, s5_c_re, s5_c_im, s5_d, s5_w_glu, na_rpb, gqa_q_norm, gqa_k_norm, lru_conv_w, lru_conv_b, lru_w_a, lru_b_a, lru_w_x, lru_b_x, lru_lambda, g_ffn2, w_ffn2_in, w_ffn2_out, g_final):
    n_batch, t_lat, d = x.shape
    t_ctx = ctx.shape[1]
    depth = w_ada.shape[0]
    assert d == D_MODEL and t_lat % TOKEN_TILE == 0 and (n_batch * t_ctx) % TOKEN_TILE == 0
    assert t_lat % t_ctx == 0 and t_ctx % SEQ_TILE == 0 and t_lat % FLASH_TQ == 0
    n_lat = n_batch * t_lat
    n_all = n_lat + n_batch * t_ctx
    tiles_per_type = t_lat // TOKEN_TILE
    assert (n_batch * t_ctx) // TOKEN_TILE <= tiles_per_type

    c8 = jnp.zeros((8, D_MODEL), F32).at[:n_batch].set(c.astype(F32)).at[n_batch].set(c_ctx.astype(F32))
    mods_all = ada_mods(c8, w_ada, b_ada).reshape(depth, 8, N_MOD, D_MODEL)

    h = jnp.concatenate([x.reshape(n_lat, D_MODEL), ctx.reshape(n_batch * t_ctx, D_MODEL)], axis=0)
    s_len = t_lat + t_ctx

    for l in range(depth):
        need_ctx = l < depth - 1
        mods = mods_all[l]
        w1i, w1o = w_ffn1_in[l].astype(BF16), w_ffn1_out[l].astype(BF16)
        w2i, w2o = w_ffn2_in[l].astype(BF16), w_ffn2_out[l].astype(BF16)
        wl = w_in[l]
        w_in_perm = jnp.concatenate([wl[:, 0:256], wl[:, 1536:2048], wl[:, 1024:1536], wl[:, 256:1024]],
                                    axis=1).astype(BF16)

        h = ffn_half(h, mods, g_ffn1[l], w1i, w1o, 0, n_all, tiles_per_type)
        pscan, pattn = in_projection(h, mods, g_mix[l], w_in_perm, tiles_per_type)

        tables = _s5_tables(s5_lambda_re[l], s5_lambda_im[l], s5_log_step[l], s5_b_re[l], s5_b_im[l],
                            s5_c_re[l], s5_c_im[l])
        ys5, ys5_ctx = s5_scan(pscan, tables, n_batch, t_lat, t_ctx)
        if need_ctx:
            ys5 = jnp.concatenate([ys5, ys5_ctx], axis=0)

        lru_args = lambda dr: (lru_conv_w[l], lru_conv_b[l], _block_diag(lru_w_a[l, dr]).astype(BF16), lru_b_a[l, dr],
                               _block_diag(lru_w_x[l, dr]).astype(BF16), lru_b_x[l, dr], lru_lambda[l, dr])
        hf = lru_scan(pscan, *lru_args(0), False, n_batch, t_lat, t_ctx)
        hr = lru_scan(pscan, *lru_args(1), True, n_batch, t_lat, t_ctx)

        yb = natten_latent(pattn, _natten_bias(na_rpb[l]), n_batch, t_lat, t_ctx)

        qh, kth, vh = gqa_prep(pattn, gqa_q_norm[l], gqa_k_norm[l], n_batch, t_lat, t_ctx)
        yc = gqa_attention(qh, kth, vh, FLASH_TQ, 0, t_lat // FLASH_TQ, 0, s_len // KEY_CHUNK)

        if need_ctx:
            yb = jnp.concatenate([yb, natten_context(pattn, n_batch, t_lat, t_ctx)], axis=0)
            yc_ctx = gqa_attention(qh, kth, vh, t_ctx, t_lat // t_ctx, 1, t_lat // t_ctx, t_ctx // KEY_CHUNK)
            yc = jnp.concatenate([yc, yc_ctx], axis=0)

        n_rows = n_all if need_ctx else n_lat
        h = out_projection(h, mods, ys5, pscan, hf, hr, yb, yc, s5_d[l], s5_w_glu[l].astype(BF16),
                           w_out[l].astype(BF16), n_rows, tiles_per_type)
        h = ffn_half(h, mods, g_ffn2[l], w2i, w2o, 6, n_rows, tiles_per_type,
                     g_final=None if need_ctx else g_final)
    return h.reshape(n_batch, t_lat, D_MODEL)
```

```python
import functools
import math

import numpy as np
import jax
import jax.numpy as jnp
from jax import lax
from jax.experimental import pallas as pl
from jax.experimental.pallas import tpu as pltpu

F32 = jnp.float32
BF16 = jnp.bfloat16

D_MODEL = 1024
GRID_W = 64
HEAD_DIM = 64
GROUP_WIDTH = D_MODEL // 4
S5_CH = 16
S5_GROUPS = GROUP_WIDTH // S5_CH
S5_STATE = 64
NA_HEADS = GROUP_WIDTH // HEAD_DIM
NA_KR = 8
NA_KC = 16
GQA_Q_HEADS = GROUP_WIDTH // HEAD_DIM
GQA_KV_HEADS = GQA_Q_HEADS // 2
GQA_GROUP = GQA_Q_HEADS // GQA_KV_HEADS
LRU_BLOCKS = GROUP_WIDTH // HEAD_DIM
LRU_CONV = 4
LRU_C = 8.0
D_FF = ((8 * D_MODEL // 3 + 127) // 128) * 128
FFN_RES = 0.5
ROPE_BASE = 10000.0
EPS = 1e-6
N_MOD = 9

TOKEN_TILE = 512
SEQ_TILE = 256
LRU_SEG_PITCH = SEQ_TILE // 8 + 4
S5_CHUNK = 32
S5_OCT = 8
MXU_DEPTH = 256
FF_CHUNKS = ((0, 6 * MXU_DEPTH), (6 * MXU_DEPTH, D_FF))
FLASH_TQ = 512
KEY_CHUNK = 256
NA_QROWS = NA_KR // 2
NA_BAND = NA_KR + NA_QROWS
LOG2E = math.log2(math.e)
NEG_BIG = -1e30
VMEM_LIMIT = 56 * 1024 * 1024


def _cparams(sem):
    return pltpu.CompilerParams(dimension_semantics=sem, vmem_limit_bytes=VMEM_LIMIT)


def _dot(a, b):
    return jnp.dot(a, b, preferred_element_type=F32)


def _dot_nt(a, b):
    return lax.dot_general(a, b, (((1,), (1,)), ((), ())), preferred_element_type=F32)


def _ada_kernel(c_ref, w_ref, b_ref, o_ref):
    c = c_ref[...]
    s = c * jax.nn.sigmoid(c)
    o_ref[...] = _dot(s.astype(BF16), w_ref[...].astype(BF16)) + b_ref[...]


def ada_mods(c8, w_ada, b_ada):
    L = w_ada.shape[0]
    tn = D_MODEL
    return pl.pallas_call(
        _ada_kernel,
        grid=(L, N_MOD * D_MODEL // tn),
        in_specs=[
            pl.BlockSpec((8, D_MODEL), lambda l, j: (0, 0)),
            pl.BlockSpec((None, D_MODEL, tn), lambda l, j: (l, 0, j)),
            pl.BlockSpec((None, 1, tn), lambda l, j: (l, 0, j)),
        ],
        out_specs=pl.BlockSpec((None, 8, tn), lambda l, j: (l, 0, j)),
        out_shape=jax.ShapeDtypeStruct((L, 8, N_MOD * D_MODEL), F32),
        compiler_params=_cparams(("parallel", "parallel")),
        name="ada_mods",
    )(c8, w_ada, b_ada.reshape(L, 1, N_MOD * D_MODEL))


def _norm_mod(x, g, shift, scale):
    ms = jnp.mean(x * x, axis=-1, keepdims=True)
    y = x * lax.rsqrt(ms + EPS) * g
    return y * (1.0 + scale) + shift


def _ffn_kernel(x_ref, mod_ref, g_ref, wi_ref, wo_ref, *rest, k0, final):
    if final:
        gf_ref, o_ref = rest
    else:
        (o_ref,) = rest
    x = x_ref[...]
    y = _norm_mod(x, g_ref[...], mod_ref[k0:k0 + 1, :], mod_ref[k0 + 1:k0 + 2, :]).astype(BF16)
    acc = jnp.zeros(x.shape, F32)
    for lo, hi in FF_CHUNKS:
        a = _dot(y, wi_ref[:, lo:hi])
        b = _dot(y, wi_ref[:, D_FF + lo:D_FF + hi])
        h = (a * jax.nn.sigmoid(a) * b).astype(BF16)
        acc = acc + _dot(h, wo_ref[lo:hi, :])
    out = x + FFN_RES * mod_ref[k0 + 2:k0 + 3, :] * acc
    if final:
        ms = jnp.mean(out * out, axis=-1, keepdims=True)
        out = out * lax.rsqrt(ms + EPS) * gf_ref[...]
    o_ref[...] = out


def ffn_half(x, mods, g, w_i, w_o, k0, n_rows, tiles_per_type, g_final=None):
    final = g_final is not None
    nt = n_rows // TOKEN_TILE
    in_specs = [
        pl.BlockSpec((TOKEN_TILE, D_MODEL), lambda i: (i, 0)),
        pl.BlockSpec((None, N_MOD, D_MODEL), lambda i: (i // tiles_per_type, 0, 0)),
        pl.BlockSpec((1, D_MODEL), lambda i: (0, 0)),
        pl.BlockSpec((D_MODEL, 2 * D_FF), lambda i: (0, 0)),
        pl.BlockSpec((D_FF, D_MODEL), lambda i: (0, 0)),
    ]
    args = [x, mods, g.reshape(1, D_MODEL), w_i, w_o]
    if final:
        in_specs.append(pl.BlockSpec((1, D_MODEL), lambda i: (0, 0)))
        args.append(g_final.reshape(1, D_MODEL))
    return pl.pallas_call(
        functools.partial(_ffn_kernel, k0=k0, final=final),
        grid=(nt,),
        in_specs=in_specs,
        out_specs=pl.BlockSpec((TOKEN_TILE, D_MODEL), lambda i: (i, 0)),
        out_shape=jax.ShapeDtypeStruct((n_rows, D_MODEL), F32),
        compiler_params=_cparams(("parallel",)),
        name="ffn_half",
    )(*args)


def _inproj_kernel(x_ref, mod_ref, g_ref, w_ref, ps_ref, pa_ref):
    y = _norm_mod(x_ref[...], g_ref[...], mod_ref[3:4, :], mod_ref[4:5, :]).astype(BF16)
    p = _dot(y, w_ref[...])
    ns = ps_ref.shape[1]
    ps_ref[...] = p[:, :ns]
    pa_ref[...] = p[:, ns:].astype(BF16)


def in_projection(x, mods, g, w_in_perm, tiles_per_type):
    n = x.shape[0]
    n_scan = 3 * GROUP_WIDTH
    n_attn = w_in_perm.shape[1] - n_scan
    return pl.pallas_call(
        _inproj_kernel,
        grid=(n // TOKEN_TILE,),
        in_specs=[
            pl.BlockSpec((TOKEN_TILE, D_MODEL), lambda i: (i, 0)),
            pl.BlockSpec((None, N_MOD, D_MODEL), lambda i: (i // tiles_per_type, 0, 0)),
            pl.BlockSpec((1, D_MODEL), lambda i: (0, 0)),
            pl.BlockSpec((D_MODEL, n_scan + n_attn), lambda i: (0, 0)),
        ],
        out_specs=[
            pl.BlockSpec((TOKEN_TILE, n_scan), lambda i: (i, 0)),
            pl.BlockSpec((TOKEN_TILE, n_attn), lambda i: (i, 0)),
        ],
        out_shape=[
            jax.ShapeDtypeStruct((n, n_scan), F32),
            jax.ShapeDtypeStruct((n, n_attn), BF16),
        ],
        compiler_params=_cparams(("parallel",)),
        name="in_projection",
    )(x, mods, g.reshape(1, D_MODEL), w_in_perm)


def _outproj_kernel(h_ref, mod_ref, ys5_ref, u_ref, hf_ref, hr_ref, gl_ref, yb_ref, yc_ref,
                    dsk_ref, wglu_ref, wo_ref, o_ref):
    gw = GROUP_WIDTH
    ya = jax.nn.gelu(ys5_ref[...] + dsk_ref[...] * u_ref[...])
    ya = ya * jax.nn.sigmoid(_dot(ya.astype(BF16), wglu_ref[...]))
    yd = (hf_ref[...] + hr_ref[...]) * jax.nn.gelu(gl_ref[...])
    acc = _dot(ya.astype(BF16), wo_ref[0:gw, :])
    acc = acc + _dot(yb_ref[...].astype(BF16), wo_ref[gw:2 * gw, :])
    acc = acc + _dot(yc_ref[...].astype(BF16), wo_ref[2 * gw:3 * gw, :])
    acc = acc + _dot(yd.astype(BF16), wo_ref[3 * gw:4 * gw, :])
    o_ref[...] = h_ref[...] + mod_ref[5:6, :] * acc


def out_projection(h, mods, ys5, pscan, hf, hr, yb, yc, d_skip, w_glu, w_out, n_rows, tiles_per_type):
    gw = GROUP_WIDTH
    row = lambda i: (i, 0)
    return pl.pallas_call(
        _outproj_kernel,
        grid=(n_rows // TOKEN_TILE,),
        in_specs=[
            pl.BlockSpec((TOKEN_TILE, D_MODEL), row),
            pl.BlockSpec((None, N_MOD, D_MODEL), lambda i: (i // tiles_per_type, 0, 0)),
            pl.BlockSpec((TOKEN_TILE, gw), row),
            pl.BlockSpec((TOKEN_TILE, gw), lambda i: (i, 0)),
            pl.BlockSpec((TOKEN_TILE, gw), row),
            pl.BlockSpec((TOKEN_TILE, gw), row),
            pl.BlockSpec((TOKEN_TILE, gw), lambda i: (i, 2)),
            pl.BlockSpec((TOKEN_TILE, gw), row),
            pl.BlockSpec((TOKEN_TILE, gw), row),
            pl.BlockSpec((1, gw), lambda i: (0, 0)),
            pl.BlockSpec((gw, gw), lambda i: (0, 0)),
            pl.BlockSpec((D_MODEL, D_MODEL), lambda i: (0, 0)),
        ],
        out_specs=pl.BlockSpec((TOKEN_TILE, D_MODEL), row),
        out_shape=jax.ShapeDtypeStruct((n_rows, D_MODEL), F32),
        compiler_params=_cparams(("parallel",)),
        name="out_projection",
    )(h, mods, ys5, pscan, hf, hr, pscan, yb, yc, d_skip.reshape(1, gw), w_glu, w_out)


def _s5_tables(lam_re, lam_im, log_step, b_re, b_im, c_re, c_im):
    L = S5_CHUNK
    G, P, H = S5_GROUPS, S5_STATE, S5_CH
    hp = lax.Precision.HIGHEST
    outs = []
    for dr in range(2):
        lr, li = lam_re[dr].astype(F32), lam_im[dr].astype(F32)
        dt = jnp.exp(log_step[dr].astype(F32))[:, None]
        mag = jnp.exp(lr * dt)
        ar, ai = mag * jnp.cos(li * dt), mag * jnp.sin(li * dt)
        den = lr * lr + li * li
        fr = ((ar - 1) * lr + ai * li) / den
        fi = (ai * lr - (ar - 1) * li) / den
        br, bi = b_re[dr].astype(F32), b_im[dr].astype(F32)
        bbr = fr[..., None] * br - fi[..., None] * bi
        bbi = fr[..., None] * bi + fi[..., None] * br
        cr, ci = c_re[dr].astype(F32), c_im[dr].astype(F32)
        pr, pi = [jnp.ones_like(ar)], [jnp.zeros_like(ar)]
        for _ in range(L):
            pr.append(pr[-1] * ar - pi[-1] * ai)
            pi.append(pr[-2] * ai + pi[-1] * ar)
        pw_r, pw_i = jnp.stack(pr), jnp.stack(pi)
        abr = pw_r[:L, :, :, None] * bbr[None] - pw_i[:L, :, :, None] * bbi[None]
        abi = pw_r[:L, :, :, None] * bbi[None] + pw_i[:L, :, :, None] * bbr[None]
        kk = (jnp.einsum('gop,lgpi->lgoi', cr, abr, precision=hp)
              - jnp.einsum('gop,lgpi->lgoi', ci, abi, precision=hp))
        tt = np.arange(L)
        lag = (tt[None, :] - tt[:, None]) if dr == 0 else (tt[:, None] - tt[None, :])
        onehot = (lag[None] == tt[:, None, None]).astype(np.float32)
        mt = jnp.einsum('lgoi,lst->gsito', kk, onehot, precision=hp).reshape(G, L * H, L * H)
        abr_s, abi_s = (abr[::-1], abi[::-1]) if dr == 0 else (abr, abi)
        bsr = jnp.transpose(abr_s, (1, 0, 3, 2)).reshape(G, L * H, P)
        bsi = jnp.transpose(abi_s, (1, 0, 3, 2)).reshape(G, L * H, P)
        qr, qi = pw_r[1:], pw_i[1:]
        if dr == 1:
            qr, qi = qr[::-1], qi[::-1]
        ccr = cr[None] * qr[:, :, None, :] - ci[None] * qi[:, :, None, :]
        cci = -cr[None] * qi[:, :, None, :] - ci[None] * qr[:, :, None, :]
        ccr = jnp.transpose(ccr, (1, 3, 0, 2)).reshape(G, P, L * H)
        cci = jnp.transpose(cci, (1, 3, 0, 2)).reshape(G, P, L * H)
        outs.append((mt, bsr, bsi, ccr, cci, pw_r[L], pw_i[L]))
    (mt0, bsr0, bsi0, ccr0, cci0, alr0, ali0), (mt1, bsr1, bsi1, ccr1, cci1, alr1, ali1) = outs
    mt = (mt0 + mt1).astype(BF16)
    bs = jnp.concatenate([bsr0, bsi0, bsr1, bsi1], axis=-1).astype(BF16)
    cc = jnp.stack([ccr0, cci0, ccr1, cci1], axis=1).astype(BF16)
    al = jnp.stack([alr0, ali0, alr1, ali1])
    return mt, bs, cc, al


def _s5_select_tables():
    H, O = S5_CH, S5_OCT
    pack = np.zeros((O // 2, O * 128, 256), np.float32)
    unpack = np.zeros((O // 2, O * 128, 256), np.float32)
    for j in range(8):
        for g in range(O):
            for h in range(H):
                pack[g // 2, j * 128 + g * H + h, (g % 2) * 128 + j * H + h] = 1.0
                unpack[j // 2, g * 128 + j * H + h, (j % 2) * 128 + g * H + h] = 1.0
    return jnp.asarray(pack, BF16), jnp.asarray(unpack, BF16)


def _s5_kernel(ul_ref, uc_ref, pack_ref, unpack_ref, mt_ref, bs_ref, cc_ref, al_ref, ol_ref, oc_ref,
               ug_s, y_s, s_s, h_s, *, ncl, ncc):
    L, P, O = S5_CHUNK, S5_STATE, S5_OCT
    nc = ncl + ncc
    nblk = L // 8

    for tb in range(nblk):
        zs = [jnp.concatenate([ul_ref[pl.ds(8 * tb + j, ncl, stride=L), :],
                               uc_ref[pl.ds(8 * tb + j, ncc, stride=L), :]], axis=0) for j in range(8)]
        zc = jnp.concatenate(zs, axis=1).astype(BF16)
        for gp in range(O // 2):
            r = _dot(zc, pack_ref[gp])
            ug_s[2 * gp, :, tb * 128:(tb + 1) * 128] = r[:, :128]
            ug_s[2 * gp + 1, :, tb * 128:(tb + 1) * 128] = r[:, 128:]

    for g in range(O):
        ug = ug_s[g].astype(BF16)
        y_s[g] = _dot(ug, mt_ref[g])
        st = _dot(ug, bs_ref[g])
        for k in range(4):
            s_s[k, :, g, :] = st[:, k * P:(k + 1) * P]

    afr, afi, arr, ari = al_ref[0], al_ref[1], al_ref[2], al_ref[3]

    def step(j, carry):
        hfr, hfi, hrr, hri = carry
        cf = jnp.where(j < ncc, ncl + j, j - ncc)
        cr = nc - 1 - j
        h_s[0, cf] = hfr
        h_s[1, cf] = hfi
        h_s[2, cr] = hrr
        h_s[3, cr] = hri
        return (afr * hfr - afi * hfi + s_s[0, cf], afr * hfi + afi * hfr + s_s[1, cf],
                arr * hrr - ari * hri + s_s[2, cr], arr * hri + ari * hrr + s_s[3, cr])

    zero = jnp.zeros((O, P), F32)
    lax.fori_loop(0, nc, step, (zero, zero, zero, zero))

    for g in range(O):
        y = y_s[g]
        for k in range(4):
            y = y + _dot(h_s[k, :, g, :].astype(BF16), cc_ref[g, k])
        y_s[g] = y

    for tb in range(nblk):
        yc = jnp.concatenate([y_s[g, :, tb * 128:(tb + 1) * 128] for g in range(O)], axis=1)
        hi = yc.astype(BF16)
        lo = (yc - hi.astype(F32)).astype(BF16)
        for jp in range(4):
            z = _dot(hi, unpack_ref[jp]) + _dot(lo, unpack_ref[jp])
            for e in range(2):
                t = 8 * tb + 2 * jp + e
                zt = z[:, e * 128:(e + 1) * 128]
                ol_ref[pl.ds(t, ncl, stride=L), :] = zt[:ncl]
                oc_ref[pl.ds(t, ncc, stride=L), :] = zt[ncl:]


def s5_scan(pscan, tables, n_batch, t_lat, t_ctx):
    mt, bs, cc, al = tables
    pack, unpack = _s5_select_tables()
    L, P, O = S5_CHUNK, S5_STATE, S5_OCT
    W = L * S5_CH
    ncl, ncc = t_lat // L, t_ctx // L
    nc = ncl + ncc
    ctx0 = n_batch * t_lat // t_ctx
    once = pl.Buffered(1)
    return pl.pallas_call(
        functools.partial(_s5_kernel, ncl=ncl, ncc=ncc),
        grid=(S5_GROUPS // O, n_batch),
        in_specs=[
            pl.BlockSpec((t_lat, O * S5_CH), lambda o, b: (b, o)),
            pl.BlockSpec((t_ctx, O * S5_CH), lambda o, b: (ctx0 + b, o)),
            pl.BlockSpec(pack.shape, lambda o, b: (0, 0, 0), pipeline_mode=once),
            pl.BlockSpec(unpack.shape, lambda o, b: (0, 0, 0), pipeline_mode=once),
            pl.BlockSpec((O, W, W), lambda o, b: (o, 0, 0), pipeline_mode=once),
            pl.BlockSpec((O, W, 4 * P), lambda o, b: (o, 0, 0), pipeline_mode=once),
            pl.BlockSpec((O, 4, P, W), lambda o, b: (o, 0, 0, 0), pipeline_mode=once),
            pl.BlockSpec((4, O, P), lambda o, b: (0, o, 0)),
        ],
        out_specs=[
            pl.BlockSpec((t_lat, O * S5_CH), lambda o, b: (b, o)),
            pl.BlockSpec((t_ctx, O * S5_CH), lambda o, b: (b, o)),
        ],
        out_shape=[
            jax.ShapeDtypeStruct((n_batch * t_lat, GROUP_WIDTH), F32),
            jax.ShapeDtypeStruct((n_batch * t_ctx, GROUP_WIDTH), F32),
        ],
        scratch_shapes=[
            pltpu.VMEM((O, nc, W), F32),
            pltpu.VMEM((O, nc, W), F32),
            pltpu.VMEM((4, nc, O, P), F32),
            pltpu.VMEM((4, nc, O, P), F32),
        ],
        compiler_params=_cparams(("arbitrary", "arbitrary")),
        name="s5_scan",
    )(pscan, pscan, pack, unpack, mt, bs, cc, al)


def _lru_kernel(xp_ref, xc_ref, xn_ref, cw_ref, cb_ref, wa_ref, ba_ref, wx_ref, bx_ref, lam_ref,
                o_ref, xs_s, a_s, b_s, h_s, *, reverse, ntl, ntc):
    tt = SEQ_TILE
    j = pl.program_id(1)
    is_ctx = j < ntc
    if reverse:
        k = jnp.where(is_ctx, ntc - 1 - j, ntl - 1 - (j - ntc))
    else:
        k = jnp.where(is_ctx, j, j - ntc)
    seg = jnp.where(is_ctx, ntc, ntl)
    has_prev = (k > 0).astype(F32)
    has_next = (k < seg - 1).astype(F32)

    @pl.when(j == 0)
    def _():
        h_s[...] = jnp.zeros_like(h_s)

    xs_s[0:8, :] = xp_ref[tt - 8:tt, :] * has_prev
    xs_s[8:8 + tt, :] = xc_ref[...]
    xs_s[8 + tt:16 + tt, :] = xn_ref[0:8, :] * has_next
    xc = cb_ref[...] + jnp.zeros((tt, GROUP_WIDTH), F32)
    for tap in range(LRU_CONV):
        xc = xc + cw_ref[tap:tap + 1, :] * xs_s[pl.ds(6 + tap, tt), :]
    xb = xc.astype(BF16)
    r = jax.nn.sigmoid(_dot(xb, wa_ref[...]) + ba_ref[...])
    i = jax.nn.sigmoid(_dot(xb, wx_ref[...]) + bx_ref[...])
    z = -lam_ref[...]
    softplus = jnp.maximum(z, 0.0) + jnp.log1p(jnp.exp(-jnp.abs(z)))
    log_a = -LRU_C * r * softplus
    a_all = jnp.exp(log_a)
    th = jnp.tanh(log_a)
    b_all = jnp.sqrt(-2.0 * th / (1.0 - th)) * (i * xc)

    nseg = 8
    slen = tt // nseg
    pitch = LRU_SEG_PITCH
    order = range(slen - 1, -1, -1) if reverse else range(slen)
    e_tile = h_s[...]
    e_next = []
    for hf in range(GROUP_WIDTH // 128):
        lanes = slice(hf * 128, (hf + 1) * 128)
        for sgm in range(nseg):
            a_s[hf, sgm * pitch:sgm * pitch + slen, :] = a_all[sgm * slen:(sgm + 1) * slen, lanes]
            b_s[hf, sgm * pitch:sgm * pitch + slen, :] = b_all[sgm * slen:(sgm + 1) * slen, lanes]
        h = jnp.zeros((nseg, 128), F32)
        p = jnp.ones((nseg, 128), F32)
        for i in order:
            pos = pl.ds(i, nseg, stride=pitch)
            a = a_s[hf, pos, :]
            h = a * h + b_s[hf, pos, :]
            p = a * p
            b_s[hf, pos, :] = h
            a_s[hf, pos, :] = p
        e = e_tile[:, lanes]
        ins = [None] * nseg
        for sgm in (range(nseg - 1, -1, -1) if reverse else range(nseg)):
            ins[sgm] = e
            e = h[sgm:sgm + 1, :] + p[sgm:sgm + 1, :] * e
        e_next.append(e)
        e_in = jnp.concatenate(ins, axis=0)
        for i in range(slen):
            pos = pl.ds(i, nseg, stride=pitch)
            b_s[hf, pos, :] = b_s[hf, pos, :] + a_s[hf, pos, :] * e_in
        for sgm in range(nseg):
            o_ref[sgm * slen:(sgm + 1) * slen, lanes] = b_s[hf, sgm * pitch:sgm * pitch + slen, :]
    h_s[...] = jnp.concatenate(e_next, axis=1)


def lru_scan(pscan, conv_w, conv_b, w_a, b_a, w_x, b_x, lam, reverse, n_batch, t_lat, t_ctx):
    tt = SEQ_TILE
    gw = GROUP_WIDTH
    ntl, ntc = t_lat // tt, t_ctx // tt
    n = pscan.shape[0]

    def tile(b, j, d):
        is_ctx = j < ntc
        if reverse:
            k = jnp.where(is_ctx, ntc - 1 - j, ntl - 1 - (j - ntc))
        else:
            k = jnp.where(is_ctx, j, j - ntc)
        seg = jnp.where(is_ctx, ntc, ntl)
        k = jnp.clip(k + d, 0, seg - 1)
        return jnp.where(is_ctx, n_batch * ntl + b * ntc + k, b * ntl + k)

    vec = pl.BlockSpec((1, gw), lambda b, j: (0, 0))
    mat = pl.BlockSpec((gw, gw), lambda b, j: (0, 0))
    return pl.pallas_call(
        functools.partial(_lru_kernel, reverse=reverse, ntl=ntl, ntc=ntc),
        grid=(n_batch, ntl + ntc),
        in_specs=[
            pl.BlockSpec((tt, gw), lambda b, j: (tile(b, j, -1), 1)),
            pl.BlockSpec((tt, gw), lambda b, j: (tile(b, j, 0), 1)),
            pl.BlockSpec((tt, gw), lambda b, j: (tile(b, j, 1), 1)),
            pl.BlockSpec((LRU_CONV, gw), lambda b, j: (0, 0)),
            vec, mat, vec, mat, vec, vec,
        ],
        out_specs=pl.BlockSpec((tt, gw), lambda b, j: (tile(b, j, 0), 0)),
        out_shape=jax.ShapeDtypeStruct((n, gw), F32),
        scratch_shapes=[
            pltpu.VMEM((tt + 16, gw), F32),
            pltpu.VMEM((gw // 128, 8 * LRU_SEG_PITCH, 128), F32),
            pltpu.VMEM((gw // 128, 8 * LRU_SEG_PITCH, 128), F32),
            pltpu.VMEM((1, gw), F32),
        ],
        compiler_params=_cparams(("parallel", "arbitrary")),
        name="lru_rev" if reverse else "lru_fwd",
    )(pscan, pscan, pscan, conv_w, conv_b.reshape(1, gw), w_a, b_a.reshape(1, gw), w_x, b_x.reshape(1, gw),
      lam.reshape(1, gw))


def _block_diag(w):
    n, d, e = w.shape
    eye = jnp.eye(n, dtype=w.dtype)
    return (eye[:, None, :, None] * w[:, :, None, :]).reshape(n * d, n * e)


def _natten_bias(rpb):
    W, KR, KC, QR, NB = GRID_W, NA_KR, NA_KC, NA_QROWS, NA_BAND
    col = np.arange(W)
    cs = np.clip(col - KC // 2, 0, W - KC)
    inwin = (col[None, :] >= cs[:, None]) & (col[None, :] < cs[:, None] + KC)
    coff = np.clip(col[None, :] - col[:, None] + (KC - 1), 0, 2 * KC - 2)
    a = np.arange(QR)[:, None]
    i = np.arange(NB)[None, :]
    first = (i - a, (i < KR) & (a >= 0))
    mid = (i - a - KR // 2, (i >= a) & (i < a + KR))
    last = (i - a - NB + QR, (i >= NB - KR) & (a >= 0))
    coh = (coff[:, :, None] == np.arange(2 * KC - 1)).astype(np.float32)
    t = jnp.einsum('hrc,qkc->hrqk', rpb.astype(F32), coh, precision=lax.Precision.HIGHEST)
    t = t * LOG2E + np.where(inwin, 0.0, NEG_BIG).astype(np.float32)
    plan = tuple(
        tuple(tuple(int(np.clip(delta[ai, ii] + KR - 1, 0, 2 * KR - 2)) if valid[ai, ii] else -1
                    for ii in range(NB)) for ai in range(QR))
        for delta, valid in (first, mid, last))

    def expand(t_ref, o_ref):
        for v in range(3):
            for ai in range(QR):
                for ii in range(NB):
                    r = plan[v][ai][ii]
                    blk = t_ref[r] if r >= 0 else jnp.full((W, W), NEG_BIG, F32)
                    o_ref[v, ai * W:(ai + 1) * W, ii * W:(ii + 1) * W] = blk

    nh = rpb.shape[0]
    return pl.pallas_call(
        expand,
        grid=(nh,),
        in_specs=[pl.BlockSpec((None, 2 * KR - 1, W, W), lambda h: (h, 0, 0, 0))],
        out_specs=pl.BlockSpec((3, None, QR * W, NB * W), lambda h: (0, h, 0, 0)),
        out_shape=jax.ShapeDtypeStruct((3, nh, QR * W, NB * W), F32),
        compiler_params=_cparams(("parallel",)),
        name="natten_bias",
    )(t)


def _natten_kernel(q_ref, k_ref, v_ref, kc_ref, vc_ref, bias_ref, o_ref, *, rows):
    W, KR, hd = GRID_W, NA_KR, HEAD_DIM
    blk = pl.program_id(1)
    nblk = rows // NA_QROWS
    bs = jnp.clip(blk * NA_QROWS - KR // 2, 0, rows - NA_BAND)
    var = jnp.where(blk == 0, 0, jnp.where(blk == nblk - 1, 2, 1))
    start = pl.multiple_of(bs * W, W)
    q = q_ref[...]
    kw = k_ref[pl.ds(start, NA_BAND * W), :]
    vw = v_ref[pl.ds(start, NA_BAND * W), :]
    kc = kc_ref[...]
    vc = vc_ref[...]
    scale2 = (hd ** -0.5) * LOG2E
    for h in range(NA_HEADS):
        sl = slice(h * hd, (h + 1) * hd)
        qh = q[:, sl]
        s = _dot_nt(qh, kw[:, sl]) * scale2 + bias_ref[var, h]
        sc = _dot_nt(qh, kc[:, sl]) * scale2
        m = jnp.maximum(jnp.max(s, axis=-1, keepdims=True), jnp.max(sc, axis=-1, keepdims=True))
        p = jnp.exp2(s - m)
        pc = jnp.exp2(sc - m)
        l = jnp.sum(p, axis=-1, keepdims=True) + jnp.sum(pc, axis=-1, keepdims=True)
        o = _dot(p.astype(BF16), vw[:, sl]) + _dot(pc.astype(BF16), vc[:, sl])
        o_ref[:, sl] = o / l


def natten_latent(pattn, bias, n_batch, t_lat, t_ctx):
    gw = GROUP_WIDTH
    rows = t_lat // GRID_W
    assert NA_QROWS == NA_KR // 2 and NA_BAND == NA_KR + NA_QROWS and rows % NA_QROWS == 0 and rows >= NA_BAND
    nblk = rows // NA_QROWS
    tq = NA_QROWS * GRID_W
    ctx0 = n_batch * t_lat // t_ctx
    return pl.pallas_call(
        functools.partial(_natten_kernel, rows=rows),
        grid=(n_batch, nblk),
        in_specs=[
            pl.BlockSpec((tq, gw), lambda b, r: (b * nblk + r, 2)),
            pl.BlockSpec((t_lat, gw), lambda b, r: (b, 3)),
            pl.BlockSpec((t_lat, gw), lambda b, r: (b, 4)),
            pl.BlockSpec((t_ctx, gw), lambda b, r: (ctx0 + b, 3)),
            pl.BlockSpec((t_ctx, gw), lambda b, r: (ctx0 + b, 4)),
            pl.BlockSpec(bias.shape, lambda b, r: (0, 0, 0, 0)),
        ],
        out_specs=pl.BlockSpec((tq, gw), lambda b, r: (b * nblk + r, 0)),
        out_shape=jax.ShapeDtypeStruct((n_batch * t_lat, gw), F32),
        compiler_params=_cparams(("parallel", "arbitrary")),
        name="natten_latent",
    )(pattn, pattn, pattn, pattn, pattn, bias)


def _ctx_attn_kernel(q_ref, k_ref, v_ref, o_ref):
    hd = HEAD_DIM
    q, k, v = q_ref[...], k_ref[...], v_ref[...]
    for h in range(NA_HEADS):
        sl = slice(h * hd, (h + 1) * hd)
        s = _dot_nt(q[:, sl], k[:, sl]) * (hd ** -0.5)
        m = jnp.max(s, axis=-1, keepdims=True)
        p = jnp.exp(s - m)
        l = jnp.sum(p, axis=-1, keepdims=True)
        o_ref[:, sl] = _dot(p.astype(BF16), v[:, sl]) / l


def natten_context(pattn, n_batch, t_lat, t_ctx):
    gw = GROUP_WIDTH
    ctx0 = n_batch * t_lat // t_ctx
    return pl.pallas_call(
        _ctx_attn_kernel,
        grid=(n_batch,),
        in_specs=[pl.BlockSpec((t_ctx, gw), lambda b, c=c: (ctx0 + b, c)) for c in (2, 3, 4)],
        out_specs=pl.BlockSpec((t_ctx, gw), lambda b: (b, 0)),
        out_shape=jax.ShapeDtypeStruct((n_batch * t_ctx, gw), F32),
        compiler_params=_cparams(("parallel",)),
        name="natten_context",
    )(pattn, pattn, pattn)


def _rope_tables(t_lat, t_ctx):
    half = HEAD_DIM // 4
    freqs = ROPE_BASE ** (-jnp.arange(half, dtype=F32) / half)
    t = jnp.arange(t_lat)
    ang_r = (t // GRID_W).astype(F32)[:, None] * freqs[None, :]
    ang_c = (t % GRID_W).astype(F32)[:, None] * freqs[None, :]
    ang = jnp.concatenate([ang_r, ang_r, ang_c, ang_c], axis=1)
    cos = jnp.concatenate([jnp.cos(ang), jnp.ones((t_ctx, HEAD_DIM), F32)], axis=0)
    sin = jnp.concatenate([jnp.sin(ang), jnp.zeros((t_ctx, HEAD_DIM), F32)], axis=0)
    return jnp.tile(cos, (1, GQA_Q_HEADS)), jnp.tile(sin, (1, GQA_Q_HEADS))


def _gqa_prep_kernel(x_ref, cos_ref, sin_ref, gq_ref, gk_ref, gm_ref, q_ref, k_ref, v_ref):
    hd = HEAD_DIM
    nq, nk = GQA_Q_HEADS * hd, GQA_KV_HEADS * hd
    x = x_ref[...].astype(F32)
    cos, sin = cos_ref[...], sin_ref[...]
    gm = gm_ref[...]

    def norm_rope(t, gain, width):
        t2 = t * t
        hi = t2.astype(BF16)
        lo = (t2 - hi.astype(F32)).astype(BF16)
        ms = _dot(hi, gm[:width, :width]) + _dot(lo, gm[:width, :width])
        tn = t * lax.rsqrt(ms + EPS) * gain
        lane = lax.broadcasted_iota(jnp.int32, tn.shape, 1)
        first = (lane % (hd // 2)) < (hd // 4)
        rot = jnp.where(first, -pltpu.roll(tn, width - hd // 4, 1), pltpu.roll(tn, hd // 4, 1))
        return tn * cos[:, :width] + rot * sin[:, :width]

    q = norm_rope(x[:, :nq], gq_ref[...], nq) * ((hd ** -0.5) * LOG2E)
    kt = norm_rope(x[:, nq:nq + nk], gk_ref[...], nk).T
    v = x_ref[:, nq + nk:nq + 2 * nk]
    for h in range(GQA_Q_HEADS):
        q_ref[h] = q[:, h * hd:(h + 1) * hd].astype(BF16)
    ones_col = (lax.broadcasted_iota(jnp.int32, (v.shape[0], hd), 1) == 0).astype(BF16)
    for h in range(GQA_KV_HEADS):
        k_ref[h] = kt[h * hd:(h + 1) * hd, :].astype(BF16)
        v_ref[h] = jnp.concatenate([v[:, h * hd:(h + 1) * hd], ones_col], axis=1)


def gqa_prep(pattn, gq, gk, n_batch, t_lat, t_ctx):
    tt = SEQ_TILE
    assert tt == KEY_CHUNK
    hd = HEAD_DIM
    ntl, ntc = t_lat // tt, t_ctx // tt
    s_len = t_lat + t_ctx
    cos, sin = _rope_tables(t_lat, t_ctx)
    nq = GQA_Q_HEADS * hd
    gmat = _block_diag(jnp.full((GQA_Q_HEADS, hd, hd), 1.0 / hd, F32)).astype(BF16)

    def pos(i):
        is_ctx = i >= n_batch * ntl
        c = i - n_batch * ntl
        return jnp.where(is_ctx, ntl + c % ntc, i % ntl)

    def bat(i):
        is_ctx = i >= n_batch * ntl
        return jnp.where(is_ctx, (i - n_batch * ntl) // ntc, i // ntl)

    return pl.pallas_call(
        _gqa_prep_kernel,
        grid=(n_batch * (ntl + ntc),),
        in_specs=[
            pl.BlockSpec((tt, 2 * nq), lambda i: (i, 0)),
            pl.BlockSpec((tt, nq), lambda i: (pos(i), 0)),
            pl.BlockSpec((tt, nq), lambda i: (pos(i), 0)),
            pl.BlockSpec((1, nq), lambda i: (0, 0)),
            pl.BlockSpec((1, nq // 2), lambda i: (0, 0)),
            pl.BlockSpec((nq, nq), lambda i: (0, 0)),
        ],
        out_specs=[
            pl.BlockSpec((None, GQA_Q_HEADS, tt, hd), lambda i: (bat(i), 0, pos(i), 0)),
            pl.BlockSpec((None, GQA_KV_HEADS, None, hd, tt), lambda i: (bat(i), 0, pos(i), 0, 0)),
            pl.BlockSpec((None, GQA_KV_HEADS, tt, 2 * hd), lambda i: (bat(i), 0, pos(i), 0)),
        ],
        out_shape=[
            jax.ShapeDtypeStruct((n_batch, GQA_Q_HEADS, s_len, hd), BF16),
            jax.ShapeDtypeStruct((n_batch, GQA_KV_HEADS, s_len // tt, hd, tt), BF16),
            jax.ShapeDtypeStruct((n_batch, GQA_KV_HEADS, s_len, 2 * hd), BF16),
        ],
        compiler_params=_cparams(("parallel",)),
        name="gqa_prep",
    )(pattn, cos, sin, jnp.tile(gq.astype(F32), GQA_Q_HEADS).reshape(1, nq),
      jnp.tile(gk.astype(F32), GQA_KV_HEADS).reshape(1, nq // 2), gmat)


def _gqa_kernel(q_ref, kt_ref, v_ref, o_ref, s_s, m_s, acc_s, *, unroll):
    g, tq, hd = q_ref.shape
    nch, _, ck = kt_ref.shape
    nl = ck // 128
    q = q_ref[...].reshape(g * tq, hd)
    m_s[...] = jnp.full_like(m_s, -jnp.inf)

    def scores(c, carry):
        s = _dot(q, kt_ref[c])
        s_s[c] = s
        fold = s[:, 0:128]
        for j in range(1, nl):
            fold = jnp.maximum(fold, s[:, j * 128:(j + 1) * 128])
        m_s[...] = jnp.maximum(m_s[...], fold)
        return carry

    lax.fori_loop(0, nch, scores, 0, unroll=unroll)
    m_s[...] = jnp.broadcast_to(jnp.max(m_s[...], axis=-1, keepdims=True), m_s.shape)
    acc_s[...] = jnp.zeros_like(acc_s)

    def values(grp, carry):
        m = m_s[...]
        ps = []
        for u in range(unroll):
            s = s_s[grp * unroll + u]
            ps += [jnp.exp2(s[:, j * 128:(j + 1) * 128] - m).astype(BF16) for j in range(nl)]
        off = pl.multiple_of(grp * (unroll * ck), unroll * ck)
        acc_s[...] += _dot(jnp.concatenate(ps, axis=1), v_ref[pl.ds(off, unroll * ck), :])
        return carry

    lax.fori_loop(0, nch // unroll, values, 0)
    acc = acc_s[...]
    o = acc[:, :hd] / acc[:, hd:hd + 1]
    o_ref[...] = jnp.concatenate([o[i * tq:(i + 1) * tq] for i in range(g)], axis=1)


def gqa_attention(q, kt, v, tq, q_blk0, nq, c_blk, nch):
    n_batch = q.shape[0]
    hd = HEAD_DIM
    g = GQA_GROUP
    ck = KEY_CHUNK
    unroll = next(u for u in (11, 3, 1) if nch % u == 0)
    return pl.pallas_call(
        functools.partial(_gqa_kernel, unroll=unroll),
        grid=(n_batch, GQA_KV_HEADS, nq),
        in_specs=[
            pl.BlockSpec((None, g, tq, hd), lambda b, h, i: (b, h, q_blk0 + i, 0)),
            pl.BlockSpec((None, None, nch, hd, ck), lambda b, h, i: (b, h, c_blk, 0, 0)),
            pl.BlockSpec((None, None, nch * ck, 2 * hd), lambda b, h, i: (b, h, c_blk, 0)),
        ],
        out_specs=pl.BlockSpec((tq, g * hd), lambda b, h, i: (b * nq + i, h)),
        out_shape=jax.ShapeDtypeStruct((n_batch * nq * tq, GQA_Q_HEADS * hd), F32),
        scratch_shapes=[
            pltpu.VMEM((nch, g * tq, ck), F32),
            pltpu.VMEM((g * tq, 128), F32),
            pltpu.VMEM((g * tq, 2 * hd), F32),
        ],
        compiler_params=_cparams(("parallel", "parallel", "arbitrary")),
        name="gqa_attention",
    )(q, kt, v)


def kernel(x, c, ctx, c_ctx, w_ada, b_ada, g_ffn1, w_ffn1_in, w_ffn1_out, g_mix, w_in, w_out, s5_lambda_re, s5_lambda_im, s5_log_step, s5_b_re, s5_b_im, s5_c_re, s5_c_im, s5_d, s5_w_glu, na_rpb, gqa_q_norm, gqa_k_norm, lru_conv_w, lru_conv_b, lru_w_a, lru_b_a, lru_w_x, lru_b_x, lru_lambda, g_ffn2, w_ffn2_in, w_ffn2_out, g_final):
    n_batch, t_lat, d = x.shape
    t_ctx = ctx.shape[1]
    depth = w_ada.shape[0]
    assert d == D_MODEL and t_lat % TOKEN_TILE == 0 and (n_batch * t_ctx) % TOKEN_TILE == 0
    assert t_lat % t_ctx == 0 and t_ctx % SEQ_TILE == 0 and t_lat % FLASH_TQ == 0
    n_lat = n_batch * t_lat
    n_all = n_lat + n_batch * t_ctx
    tiles_per_type = t_lat // TOKEN_TILE
    assert (n_batch * t_ctx) // TOKEN_TILE <= tiles_per_type

    c8 = jnp.zeros((8, D_MODEL), F32).at[:n_batch].set(c.astype(F32)).at[n_batch].set(c_ctx.astype(F32))
    mods_all = ada_mods(c8, w_ada, b_ada).reshape(depth, 8, N_MOD, D_MODEL)

    h = jnp.concatenate([x.reshape(n_lat, D_MODEL), ctx.reshape(n_batch * t_ctx, D_MODEL)], axis=0)
    s_len = t_lat + t_ctx

    for l in range(depth):
        need_ctx = l < depth - 1
        mods = mods_all[l]
        w1i, w1o = w_ffn1_in[l].astype(BF16), w_ffn1_out[l].astype(BF16)
        w2i, w2o = w_ffn2_in[l].astype(BF16), w_ffn2_out[l].astype(BF16)
        wl = w_in[l]
        w_in_perm = jnp.concatenate([wl[:, 0:256], wl[:, 1536:2048], wl[:, 1024:1536], wl[:, 256:1024]],
                                    axis=1).astype(BF16)

        h = ffn_half(h, mods, g_ffn1[l], w1i, w1o, 0, n_all, tiles_per_type)
        pscan, pattn = in_projection(h, mods, g_mix[l], w_in_perm, tiles_per_type)

        tables = _s5_tables(s5_lambda_re[l], s5_lambda_im[l], s5_log_step[l], s5_b_re[l], s5_b_im[l],
                            s5_c_re[l], s5_c_im[l])
        ys5, ys5_ctx = s5_scan(pscan, tables, n_batch, t_lat, t_ctx)
        if need_ctx:
            ys5 = jnp.concatenate([ys5, ys5_ctx], axis=0)

        lru_args = lambda dr: (lru_conv_w[l], lru_conv_b[l], _block_diag(lru_w_a[l, dr]).astype(BF16), lru_b_a[l, dr],
                               _block_diag(lru_w_x[l, dr]).astype(BF16), lru_b_x[l, dr], lru_lambda[l, dr])
        hf = lru_scan(pscan, *lru_args(0), False, n_batch, t_lat, t_ctx)
        hr = lru_scan(pscan, *lru_args(1), True, n_batch, t_lat, t_ctx)

        yb = natten_latent(pattn, _natten_bias(na_rpb[l]), n_batch, t_lat, t_ctx)

        qh, kth, vh = gqa_prep(pattn, gqa_q_norm[l], gqa_k_norm[l], n_batch, t_lat, t_ctx)
        yc = gqa_attention(qh, kth, vh, FLASH_TQ, 0, t_lat // FLASH_TQ, 0, s_len // KEY_CHUNK)

        if need_ctx:
            yb = jnp.concatenate([yb, natten_context(pattn, n_batch, t_lat, t_ctx)], axis=0)
            yc_ctx = gqa_attention(qh, kth, vh, t_ctx, t_lat // t_ctx, 1, t_lat // t_ctx, t_ctx // KEY_CHUNK)
            yc = jnp.concatenate([yc, yc_ctx], axis=0)

        n_rows = n_all if need_ctx else n_lat
        h = out_projection(h, mods, ys5, pscan, hf, hr, yb, yc, s5_d[l], s5_w_glu[l].astype(BF16),
                           w_out[l].astype(BF16), n_rows, tiles_per_type)
        h = ffn_half(h, mods, g_ffn2[l], w2i, w2o, 6, n_rows, tiles_per_type,
                     g_final=None if need_ctx else g_final)
    return h.reshape(n_batch, t_lat, D_MODEL)
```

```python
import functools
import math

import numpy as np
import jax
import jax.numpy as jnp
from jax import lax
from jax.experimental import pallas as pl
from jax.experimental.pallas import tpu as pltpu

F32 = jnp.float32
BF16 = jnp.bfloat16

D_MODEL = 1024
GRID_W = 64
HEAD_DIM = 64
GROUP_WIDTH = D_MODEL // 4
S5_CH = 16
S5_GROUPS = GROUP_WIDTH // S5_CH
S5_STATE = 64
NA_HEADS = GROUP_WIDTH // HEAD_DIM
NA_KR = 8
NA_KC = 16
GQA_Q_HEADS = GROUP_WIDTH // HEAD_DIM
GQA_KV_HEADS = GQA_Q_HEADS // 2
GQA_GROUP = GQA_Q_HEADS // GQA_KV_HEADS
LRU_BLOCKS = GROUP_WIDTH // HEAD_DIM
LRU_CONV = 4
LRU_C = 8.0
D_FF = ((8 * D_MODEL // 3 + 127) // 128) * 128
FFN_RES = 0.5
ROPE_BASE = 10000.0
EPS = 1e-6
N_MOD = 9

TOKEN_TILE = 512
SEQ_TILE = 256
LRU_SEG_PITCH = SEQ_TILE // 8 + 4
S5_CHUNK = 32
S5_OCT = 8
MXU_DEPTH = 256
FF_CHUNKS = ((0, 6 * MXU_DEPTH), (6 * MXU_DEPTH, D_FF))
FLASH_TQ = 512
KEY_CHUNK = 256
NA_QROWS = NA_KR // 2
NA_BAND = NA_KR + NA_QROWS
LOG2E = math.log2(math.e)
NEG_BIG = -1e30
VMEM_LIMIT = 56 * 1024 * 1024


def _cparams(sem):
    return pltpu.CompilerParams(dimension_semantics=sem, vmem_limit_bytes=VMEM_LIMIT)


def _dot(a, b):
    return jnp.dot(a, b, preferred_element_type=F32)


def _dot_nt(a, b):
    return lax.dot_general(a, b, (((1,), (1,)), ((), ())), preferred_element_type=F32)


def _ada_kernel(c_ref, w_ref, b_ref, o_ref):
    c = c_ref[...]
    s = c * jax.nn.sigmoid(c)
    o_ref[...] = _dot(s.astype(BF16), w_ref[...].astype(BF16)) + b_ref[...]


def ada_mods(c8, w_ada, b_ada):
    L = w_ada.shape[0]
    tn = D_MODEL
    return pl.pallas_call(
        _ada_kernel,
        grid=(L, N_MOD * D_MODEL // tn),
        in_specs=[
            pl.BlockSpec((8, D_MODEL), lambda l, j: (0, 0)),
            pl.BlockSpec((None, D_MODEL, tn), lambda l, j: (l, 0, j)),
            pl.BlockSpec((None, 1, tn), lambda l, j: (l, 0, j)),
        ],
        out_specs=pl.BlockSpec((None, 8, tn), lambda l, j: (l, 0, j)),
        out_shape=jax.ShapeDtypeStruct((L, 8, N_MOD * D_MODEL), F32),
        compiler_params=_cparams(("parallel", "parallel")),
        name="ada_mods",
    )(c8, w_ada, b_ada.reshape(L, 1, N_MOD * D_MODEL))


def _norm_mod(x, g, shift, scale):
    ms = jnp.mean(x * x, axis=-1, keepdims=True)
    y = x * lax.rsqrt(ms + EPS) * g
    return y * (1.0 + scale) + shift


def _ffn_kernel(x_ref, mod_ref, g_ref, wi_ref, wo_ref, *rest, k0, final):
    if final:
        gf_ref, o_ref = rest
    else:
        (o_ref,) = rest
    x = x_ref[...]
    y = _norm_mod(x, g_ref[...], mod_ref[k0:k0 + 1, :], mod_ref[k0 + 1:k0 + 2, :]).astype(BF16)
    acc = jnp.zeros(x.shape, F32)
    for lo, hi in FF_CHUNKS:
        a = _dot(y, wi_ref[:, lo:hi])
        b = _dot(y, wi_ref[:, D_FF + lo:D_FF + hi])
        h = (a * jax.nn.sigmoid(a) * b).astype(BF16)
        acc = acc + _dot(h, wo_ref[lo:hi, :])
    out = x + FFN_RES * mod_ref[k0 + 2:k0 + 3, :] * acc
    if final:
        ms = jnp.mean(out * out, axis=-1, keepdims=True)
        out = out * lax.rsqrt(ms + EPS) * gf_ref[...]
    o_ref[...] = out


def ffn_half(x, mods, g, w_i, w_o, k0, n_rows, tiles_per_type, g_final=None):
    final = g_final is not None
    nt = n_rows // TOKEN_TILE
    in_specs = [
        pl.BlockSpec((TOKEN_TILE, D_MODEL), lambda i: (i, 0)),
        pl.BlockSpec((None, N_MOD, D_MODEL), lambda i: (i // tiles_per_type, 0, 0)),
        pl.BlockSpec((1, D_MODEL), lambda i: (0, 0)),
        pl.BlockSpec((D_MODEL, 2 * D_FF), lambda i: (0, 0)),
        pl.BlockSpec((D_FF, D_MODEL), lambda i: (0, 0)),
    ]
    args = [x, mods, g.reshape(1, D_MODEL), w_i, w_o]
    if final:
        in_specs.append(pl.BlockSpec((1, D_MODEL), lambda i: (0, 0)))
        args.append(g_final.reshape(1, D_MODEL))
    return pl.pallas_call(
        functools.partial(_ffn_kernel, k0=k0, final=final),
        grid=(nt,),
        in_specs=in_specs,
        out_specs=pl.BlockSpec((TOKEN_TILE, D_MODEL), lambda i: (i, 0)),
        out_shape=jax.ShapeDtypeStruct((n_rows, D_MODEL), F32),
        compiler_params=_cparams(("parallel",)),
        name="ffn_half",
    )(*args)


def _inproj_kernel(x_ref, mod_ref, g_ref, w_ref, ps_ref, pa_ref):
    y = _norm_mod(x_ref[...], g_ref[...], mod_ref[3:4, :], mod_ref[4:5, :]).astype(BF16)
    p = _dot(y, w_ref[...])
    ns = ps_ref.shape[1]
    ps_ref[...] = p[:, :ns]
    pa_ref[...] = p[:, ns:].astype(BF16)


def in_projection(x, mods, g, w_in_perm, tiles_per_type):
    n = x.shape[0]
    n_scan = 3 * GROUP_WIDTH
    n_attn = w_in_perm.shape[1] - n_scan
    return pl.pallas_call(
        _inproj_kernel,
        grid=(n // TOKEN_TILE,),
        in_specs=[
            pl.BlockSpec((TOKEN_TILE, D_MODEL), lambda i: (i, 0)),
            pl.BlockSpec((None, N_MOD, D_MODEL), lambda i: (i // tiles_per_type, 0, 0)),
            pl.BlockSpec((1, D_MODEL), lambda i: (0, 0)),
            pl.BlockSpec((D_MODEL, n_scan + n_attn), lambda i: (0, 0)),
        ],
        out_specs=[
            pl.BlockSpec((TOKEN_TILE, n_scan), lambda i: (i, 0)),
            pl.BlockSpec((TOKEN_TILE, n_attn), lambda i: (i, 0)),
        ],
        out_shape=[
            jax.ShapeDtypeStruct((n, n_scan), F32),
            jax.ShapeDtypeStruct((n, n_attn), BF16),
        ],
        compiler_params=_cparams(("parallel",)),
        name="in_projection",
    )(x, mods, g.reshape(1, D_MODEL), w_in_perm)


def _outproj_kernel(h_ref, mod_ref, ys5_ref, u_ref, hf_ref, hr_ref, gl_ref, yb_ref, yc_ref,
                    dsk_ref, wglu_ref, wo_ref, o_ref):
    gw = GROUP_WIDTH
    ya = jax.nn.gelu(ys5_ref[...] + dsk_ref[...] * u_ref[...])
    ya = ya * jax.nn.sigmoid(_dot(ya.astype(BF16), wglu_ref[...]))
    yd = (hf_ref[...] + hr_ref[...]) * jax.nn.gelu(gl_ref[...])
    acc = _dot(ya.astype(BF16), wo_ref[0:gw, :])
    acc = acc + _dot(yb_ref[...].astype(BF16), wo_ref[gw:2 * gw, :])
    acc = acc + _dot(yc_ref[...].astype(BF16), wo_ref[2 * gw:3 * gw, :])
    acc = acc + _dot(yd.astype(BF16), wo_ref[3 * gw:4 * gw, :])
    o_ref[...] = h_ref[...] + mod_ref[5:6, :] * acc


def out_projection(h, mods, ys5, pscan, hf, hr, yb, yc, d_skip, w_glu, w_out, n_rows, tiles_per_type):
    gw = GROUP_WIDTH
    row = lambda i: (i, 0)
    return pl.pallas_call(
        _outproj_kernel,
        grid=(n_rows // TOKEN_TILE,),
        in_specs=[
            pl.BlockSpec((TOKEN_TILE, D_MODEL), row),
            pl.BlockSpec((None, N_MOD, D_MODEL), lambda i: (i // tiles_per_type, 0, 0)),
            pl.BlockSpec((TOKEN_TILE, gw), row),
            pl.BlockSpec((TOKEN_TILE, gw), lambda i: (i, 0)),
            pl.BlockSpec((TOKEN_TILE, gw), row),
            pl.BlockSpec((TOKEN_TILE, gw), row),
            pl.BlockSpec((TOKEN_TILE, gw), lambda i: (i, 2)),
            pl.BlockSpec((TOKEN_TILE, gw), row),
            pl.BlockSpec((TOKEN_TILE, gw), row),
            pl.BlockSpec((1, gw), lambda i: (0, 0)),
            pl.BlockSpec((gw, gw), lambda i: (0, 0)),
            pl.BlockSpec((D_MODEL, D_MODEL), lambda i: (0, 0)),
        ],
        out_specs=pl.BlockSpec((TOKEN_TILE, D_MODEL), row),
        out_shape=jax.ShapeDtypeStruct((n_rows, D_MODEL), F32),
        compiler_params=_cparams(("parallel",)),
        name="out_projection",
    )(h, mods, ys5, pscan, hf, hr, pscan, yb, yc, d_skip.reshape(1, gw), w_glu, w_out)


def _s5_tables(lam_re, lam_im, log_step, b_re, b_im, c_re, c_im):
    L = S5_CHUNK
    G, P, H = S5_GROUPS, S5_STATE, S5_CH
    hp = lax.Precision.HIGHEST
    W = L * H
    lr, li = lam_re.astype(F32), lam_im.astype(F32)
    dt = jnp.exp(log_step.astype(F32))[..., None]
    mag = jnp.exp(lr * dt)
    ar, ai = mag * jnp.cos(li * dt), mag * jnp.sin(li * dt)
    den = lr * lr + li * li
    fr = ((ar - 1) * lr + ai * li) / den
    fi = (ai * lr - (ar - 1) * li) / den
    br, bi = b_re.astype(F32), b_im.astype(F32)
    bbr = jnp.swapaxes(fr[..., None] * br - fi[..., None] * bi, 2, 3)
    bbi = jnp.swapaxes(fr[..., None] * bi + fi[..., None] * br, 2, 3)
    cr, ci = c_re.astype(F32), c_im.astype(F32)
    tau = jnp.arange(L + 1, dtype=F32)[:, None, None, None]
    pmag = jnp.exp(tau * (lr * dt)[None])
    pw_r, pw_i = pmag * jnp.cos(tau * (li * dt)[None]), pmag * jnp.sin(tau * (li * dt)[None])
    abr = pw_r[:L, :, :, None, :] * bbr[None] - pw_i[:L, :, :, None, :] * bbi[None]
    abi = pw_r[:L, :, :, None, :] * bbi[None] + pw_i[:L, :, :, None, :] * bbr[None]
    kk = (jnp.einsum('dgop,ldgip->dgilo', cr, abr, precision=hp)
          - jnp.einsum('dgop,ldgip->dgilo', ci, abi, precision=hp))
    mt = _s5_toeplitz(kk[0].reshape(G, H, W), kk[1][:, :, ::-1, :].reshape(G, H, W))

    def rows(x, flip):
        x = x[::-1] if flip else x
        return jnp.transpose(x, (1, 0, 2, 3)).reshape(G, W, P)

    bs = jnp.concatenate([rows(abr[:, 0], True), rows(abi[:, 0], True),
                          rows(abr[:, 1], False), rows(abi[:, 1], False)], axis=-1).astype(BF16)
    eo = np.tile(np.eye(H, dtype=np.float32), (1, L))
    et = np.repeat(np.eye(L, dtype=np.float32), H, axis=1)
    qr = jnp.stack([pw_r[1:, 0], pw_r[:0:-1, 1]], axis=1)
    qi = jnp.stack([pw_i[1:, 0], pw_i[:0:-1, 1]], axis=1)
    crx = jnp.einsum('dgop,ox->dgpx', cr, eo, precision=hp)
    cix = jnp.einsum('dgop,ox->dgpx', ci, eo, precision=hp)
    qrx = jnp.einsum('tdgp,tx->dgpx', qr, et, precision=hp)
    qix = jnp.einsum('tdgp,tx->dgpx', qi, et, precision=hp)
    ccr = crx * qrx - cix * qix
    cci = -crx * qix - cix * qrx
    cc = jnp.stack([ccr[0], cci[0], ccr[1], cci[1]], axis=1).astype(BF16)
    al = jnp.stack([pw_r[L, 0], pw_i[L, 0], pw_r[L, 1], pw_i[L, 1]])
    return mt, bs, cc, al


def _s5_toeplitz_kernel(pf_ref, pr_ref, o_ref):
    L, H = S5_CHUNK, S5_CH
    W = L * H
    pf, pr = pf_ref[...], pr_ref[...]
    lane = lax.broadcasted_iota(jnp.int32, (H, W), 1)
    for s in range(L):
        f = pf if s == 0 else jnp.where(lane >= s * H, pltpu.roll(pf, s * H, 1), 0.0)
        back = (L - 1 - s) * H
        r = pr if back == 0 else jnp.where(lane < (s + 1) * H, pltpu.roll(pr, W - back, 1), 0.0)
        o_ref[s * H:(s + 1) * H, :] = (f + r).astype(BF16)


def _s5_toeplitz(panel_f, panel_r):
    G, H, W = panel_f.shape
    return pl.pallas_call(
        _s5_toeplitz_kernel,
        grid=(G,),
        in_specs=[pl.BlockSpec((None, H, W), lambda g: (g, 0, 0))] * 2,
        out_specs=pl.BlockSpec((None, W, W), lambda g: (g, 0, 0)),
        out_shape=jax.ShapeDtypeStruct((G, W, W), BF16),
        compiler_params=_cparams(("parallel",)),
        name="s5_toeplitz",
    )(panel_f, panel_r)


def _s5_select_tables():
    H, O = S5_CH, S5_OCT
    pack = np.zeros((O // 2, O * 128, 256), np.float32)
    unpack = np.zeros((O // 2, O * 128, 256), np.float32)
    for j in range(8):
        for g in range(O):
            for h in range(H):
                pack[g // 2, j * 128 + g * H + h, (g % 2) * 128 + j * H + h] = 1.0
                unpack[j // 2, g * 128 + j * H + h, (j % 2) * 128 + g * H + h] = 1.0
    return jnp.asarray(pack, BF16), jnp.asarray(unpack, BF16)


def _s5_kernel(ul_ref, uc_ref, pack_ref, unpack_ref, mt_ref, bs_ref, cc_ref, al_ref, ol_ref, oc_ref,
               ug_s, y_s, s_s, h_s, *, ncl, ncc):
    L, P, O = S5_CHUNK, S5_STATE, S5_OCT
    nc = ncl + ncc
    nblk = L // 8

    for tb in range(nblk):
        zs = [jnp.concatenate([ul_ref[pl.ds(8 * tb + j, ncl, stride=L), :],
                               uc_ref[pl.ds(8 * tb + j, ncc, stride=L), :]], axis=0) for j in range(8)]
        zc = jnp.concatenate(zs, axis=1).astype(BF16)
        for gp in range(O // 2):
            r = _dot(zc, pack_ref[gp])
            ug_s[2 * gp, :, tb * 128:(tb + 1) * 128] = r[:, :128]
            ug_s[2 * gp + 1, :, tb * 128:(tb + 1) * 128] = r[:, 128:]

    for g in range(O):
        ug = ug_s[g].astype(BF16)
        y_s[g] = _dot(ug, mt_ref[g])
        st = _dot(ug, bs_ref[g])
        for k in range(4):
            s_s[k, :, g, :] = st[:, k * P:(k + 1) * P]

    afr, afi, arr, ari = al_ref[0], al_ref[1], al_ref[2], al_ref[3]

    def step(j, carry):
        hfr, hfi, hrr, hri = carry
        cf = jnp.where(j < ncc, ncl + j, j - ncc)
        cr = nc - 1 - j
        h_s[0, cf] = hfr
        h_s[1, cf] = hfi
        h_s[2, cr] = hrr
        h_s[3, cr] = hri
        return (afr * hfr - afi * hfi + s_s[0, cf], afr * hfi + afi * hfr + s_s[1, cf],
                arr * hrr - ari * hri + s_s[2, cr], arr * hri + ari * hrr + s_s[3, cr])

    zero = jnp.zeros((O, P), F32)
    lax.fori_loop(0, nc, step, (zero, zero, zero, zero))

    for g in range(O):
        y = y_s[g]
        for k in range(4):
            y = y + _dot(h_s[k, :, g, :].astype(BF16), cc_ref[g, k])
        y_s[g] = y

    for tb in range(nblk):
        yc = jnp.concatenate([y_s[g, :, tb * 128:(tb + 1) * 128] for g in range(O)], axis=1)
        hi = yc.astype(BF16)
        lo = (yc - hi.astype(F32)).astype(BF16)
        for jp in range(4):
            z = _dot(hi, unpack_ref[jp]) + _dot(lo, unpack_ref[jp])
            for e in range(2):
                t = 8 * tb + 2 * jp + e
                zt = z[:, e * 128:(e + 1) * 128]
                ol_ref[pl.ds(t, ncl, stride=L), :] = zt[:ncl]
                oc_ref[pl.ds(t, ncc, stride=L), :] = zt[ncl:]


def s5_scan(pscan, tables, n_batch, t_lat, t_ctx):
    mt, bs, cc, al = tables
    pack, unpack = _s5_select_tables()
    L, P, O = S5_CHUNK, S5_STATE, S5_OCT
    W = L * S5_CH
    ncl, ncc = t_lat // L, t_ctx // L
    nc = ncl + ncc
    ctx0 = n_batch * t_lat // t_ctx
    once = pl.Buffered(1)
    return pl.pallas_call(
        functools.partial(_s5_kernel, ncl=ncl, ncc=ncc),
        grid=(S5_GROUPS // O, n_batch),
        in_specs=[
            pl.BlockSpec((t_lat, O * S5_CH), lambda o, b: (b, o)),
            pl.BlockSpec((t_ctx, O * S5_CH), lambda o, b: (ctx0 + b, o)),
            pl.BlockSpec(pack.shape, lambda o, b: (0, 0, 0), pipeline_mode=once),
            pl.BlockSpec(unpack.shape, lambda o, b: (0, 0, 0), pipeline_mode=once),
            pl.BlockSpec((O, W, W), lambda o, b: (o, 0, 0), pipeline_mode=once),
            pl.BlockSpec((O, W, 4 * P), lambda o, b: (o, 0, 0), pipeline_mode=once),
            pl.BlockSpec((O, 4, P, W), lambda o, b: (o, 0, 0, 0), pipeline_mode=once),
            pl.BlockSpec((4, O, P), lambda o, b: (0, o, 0)),
        ],
        out_specs=[
            pl.BlockSpec((t_lat, O * S5_CH), lambda o, b: (b, o)),
            pl.BlockSpec((t_ctx, O * S5_CH), lambda o, b: (b, o)),
        ],
        out_shape=[
            jax.ShapeDtypeStruct((n_batch * t_lat, GROUP_WIDTH), F32),
            jax.ShapeDtypeStruct((n_batch * t_ctx, GROUP_WIDTH), F32),
        ],
        scratch_shapes=[
            pltpu.VMEM((O, nc, W), F32),
            pltpu.VMEM((O, nc, W), F32),
            pltpu.VMEM((4, nc, O, P), F32),
            pltpu.VMEM((4, nc, O, P), F32),
        ],
        compiler_params=_cparams(("arbitrary", "arbitrary")),
        name="s5_scan",
    )(pscan, pscan, pack, unpack, mt, bs, cc, al)


def _lru_kernel(xp_ref, xc_ref, xn_ref, cw_ref, cb_ref, wa_ref, ba_ref, wx_ref, bx_ref, lam_ref,
                o_ref, xs_s, a_s, b_s, h_s, *, reverse, ntl, ntc):
    tt = SEQ_TILE
    j = pl.program_id(1)
    is_ctx = j < ntc
    if reverse:
        k = jnp.where(is_ctx, ntc - 1 - j, ntl - 1 - (j - ntc))
    else:
        k = jnp.where(is_ctx, j, j - ntc)
    seg = jnp.where(is_ctx, ntc, ntl)
    has_prev = (k > 0).astype(F32)
    has_next = (k < seg - 1).astype(F32)

    @pl.when(j == 0)
    def _():
        h_s[...] = jnp.zeros_like(h_s)

    xs_s[0:8, :] = xp_ref[tt - 8:tt, :] * has_prev
    xs_s[8:8 + tt, :] = xc_ref[...]
    xs_s[8 + tt:16 + tt, :] = xn_ref[0:8, :] * has_next
    xc = cb_ref[...] + jnp.zeros((tt, GROUP_WIDTH), F32)
    for tap in range(LRU_CONV):
        xc = xc + cw_ref[tap:tap + 1, :] * xs_s[pl.ds(6 + tap, tt), :]
    xb = xc.astype(BF16)
    r = jax.nn.sigmoid(_dot(xb, wa_ref[...]) + ba_ref[...])
    i = jax.nn.sigmoid(_dot(xb, wx_ref[...]) + bx_ref[...])
    z = -lam_ref[...]
    softplus = jnp.maximum(z, 0.0) + jnp.log1p(jnp.exp(-jnp.abs(z)))
    log_a = -LRU_C * r * softplus
    a_all = jnp.exp(log_a)
    th = jnp.tanh(log_a)
    b_all = jnp.sqrt(-2.0 * th / (1.0 - th)) * (i * xc)

    nseg = 8
    slen = tt // nseg
    pitch = LRU_SEG_PITCH
    order = range(slen - 1, -1, -1) if reverse else range(slen)
    e_tile = h_s[...]
    e_next = []
    for hf in range(GROUP_WIDTH // 128):
        lanes = slice(hf * 128, (hf + 1) * 128)
        for sgm in range(nseg):
            a_s[hf, sgm * pitch:sgm * pitch + slen, :] = a_all[sgm * slen:(sgm + 1) * slen, lanes]
            b_s[hf, sgm * pitch:sgm * pitch + slen, :] = b_all[sgm * slen:(sgm + 1) * slen, lanes]
        h = jnp.zeros((nseg, 128), F32)
        p = jnp.ones((nseg, 128), F32)
        for i in order:
            pos = pl.ds(i, nseg, stride=pitch)
            a = a_s[hf, pos, :]
            h = a * h + b_s[hf, pos, :]
            p = a * p
            b_s[hf, pos, :] = h
            a_s[hf, pos, :] = p
        e = e_tile[:, lanes]
        ins = [None] * nseg
        for sgm in (range(nseg - 1, -1, -1) if reverse else range(nseg)):
            ins[sgm] = e
            e = h[sgm:sgm + 1, :] + p[sgm:sgm + 1, :] * e
        e_next.append(e)
        e_in = jnp.concatenate(ins, axis=0)
        for i in range(slen):
            pos = pl.ds(i, nseg, stride=pitch)
            b_s[hf, pos, :] = b_s[hf, pos, :] + a_s[hf, pos, :] * e_in
        for sgm in range(nseg):
            o_ref[sgm * slen:(sgm + 1) * slen, lanes] = b_s[hf, sgm * pitch:sgm * pitch + slen, :]
    h_s[...] = jnp.concatenate(e_next, axis=1)


def lru_scan(pscan, conv_w, conv_b, w_a, b_a, w_x, b_x, lam, reverse, n_batch, t_lat, t_ctx):
    tt = SEQ_TILE
    gw = GROUP_WIDTH
    ntl, ntc = t_lat // tt, t_ctx // tt
    n = pscan.shape[0]

    def tile(b, j, d):
        is_ctx = j < ntc
        if reverse:
            k = jnp.where(is_ctx, ntc - 1 - j, ntl - 1 - (j - ntc))
        else:
            k = jnp.where(is_ctx, j, j - ntc)
        seg = jnp.where(is_ctx, ntc, ntl)
        k = jnp.clip(k + d, 0, seg - 1)
        return jnp.where(is_ctx, n_batch * ntl + b * ntc + k, b * ntl + k)

    vec = pl.BlockSpec((1, gw), lambda b, j: (0, 0))
    mat = pl.BlockSpec((gw, gw), lambda b, j: (0, 0))
    return pl.pallas_call(
        functools.partial(_lru_kernel, reverse=reverse, ntl=ntl, ntc=ntc),
        grid=(n_batch, ntl + ntc),
        in_specs=[
            pl.BlockSpec((tt, gw), lambda b, j: (tile(b, j, -1), 1)),
            pl.BlockSpec((tt, gw), lambda b, j: (tile(b, j, 0), 1)),
            pl.BlockSpec((tt, gw), lambda b, j: (tile(b, j, 1), 1)),
            pl.BlockSpec((LRU_CONV, gw), lambda b, j: (0, 0)),
            vec, mat, vec, mat, vec, vec,
        ],
        out_specs=pl.BlockSpec((tt, gw), lambda b, j: (tile(b, j, 0), 0)),
        out_shape=jax.ShapeDtypeStruct((n, gw), F32),
        scratch_shapes=[
            pltpu.VMEM((tt + 16, gw), F32),
            pltpu.VMEM((gw // 128, 8 * LRU_SEG_PITCH, 128), F32),
            pltpu.VMEM((gw // 128, 8 * LRU_SEG_PITCH, 128), F32),
            pltpu.VMEM((1, gw), F32),
        ],
        compiler_params=_cparams(("parallel", "arbitrary")),
        name="lru_rev" if reverse else "lru_fwd",
    )(pscan, pscan, pscan, conv_w, conv_b.reshape(1, gw), w_a, b_a.reshape(1, gw), w_x, b_x.reshape(1, gw),
      lam.reshape(1, gw))


def _block_diag(w):
    n, d, e = w.shape
    eye = jnp.eye(n, dtype=w.dtype)
    return (eye[:, None, :, None] * w[:, :, None, :]).reshape(n * d, n * e)


def _natten_bias(rpb):
    W, KR, KC, QR, NB = GRID_W, NA_KR, NA_KC, NA_QROWS, NA_BAND
    col = np.arange(W)
    cs = np.clip(col - KC // 2, 0, W - KC)
    inwin = (col[None, :] >= cs[:, None]) & (col[None, :] < cs[:, None] + KC)
    coff = np.clip(col[None, :] - col[:, None] + (KC - 1), 0, 2 * KC - 2)
    a = np.arange(QR)[:, None]
    i = np.arange(NB)[None, :]
    first = (i - a, (i < KR) & (a >= 0))
    mid = (i - a - KR // 2, (i >= a) & (i < a + KR))
    last = (i - a - NB + QR, (i >= NB - KR) & (a >= 0))
    coh = (coff[:, :, None] == np.arange(2 * KC - 1)).astype(np.float32)
    t = jnp.einsum('hrc,qkc->hrqk', rpb.astype(F32), coh, precision=lax.Precision.HIGHEST)
    t = t * LOG2E + np.where(inwin, 0.0, NEG_BIG).astype(np.float32)
    plan = tuple(
        tuple(tuple(int(np.clip(delta[ai, ii] + KR - 1, 0, 2 * KR - 2)) if valid[ai, ii] else -1
                    for ii in range(NB)) for ai in range(QR))
        for delta, valid in (first, mid, last))

    def expand(t_ref, o_ref):
        for v in range(3):
            for ai in range(QR):
                for ii in range(NB):
                    r = plan[v][ai][ii]
                    blk = t_ref[r] if r >= 0 else jnp.full((W, W), NEG_BIG, F32)
                    o_ref[v, ai * W:(ai + 1) * W, ii * W:(ii + 1) * W] = blk

    nh = rpb.shape[0]
    return pl.pallas_call(
        expand,
        grid=(nh,),
        in_specs=[pl.BlockSpec((None, 2 * KR - 1, W, W), lambda h: (h, 0, 0, 0))],
        out_specs=pl.BlockSpec((3, None, QR * W, NB * W), lambda h: (0, h, 0, 0)),
        out_shape=jax.ShapeDtypeStruct((3, nh, QR * W, NB * W), F32),
        compiler_params=_cparams(("parallel",)),
        name="natten_bias",
    )(t)


def _natten_kernel(q_ref, k_ref, v_ref, kc_ref, vc_ref, bias_ref, o_ref, *, rows):
    W, KR, hd = GRID_W, NA_KR, HEAD_DIM
    blk = pl.program_id(1)
    nblk = rows // NA_QROWS
    bs = jnp.clip(blk * NA_QROWS - KR // 2, 0, rows - NA_BAND)
    var = jnp.where(blk == 0, 0, jnp.where(blk == nblk - 1, 2, 1))
    start = pl.multiple_of(bs * W, W)
    q = q_ref[...]
    kw = k_ref[pl.ds(start, NA_BAND * W), :]
    vw = v_ref[pl.ds(start, NA_BAND * W), :]
    kc = kc_ref[...]
    vc = vc_ref[...]
    scale2 = (hd ** -0.5) * LOG2E
    for h in range(NA_HEADS):
        sl = slice(h * hd, (h + 1) * hd)
        qh = q[:, sl]
        s = _dot_nt(qh, kw[:, sl]) * scale2 + bias_ref[var, h]
        sc = _dot_nt(qh, kc[:, sl]) * scale2
        m = jnp.maximum(jnp.max(s, axis=-1, keepdims=True), jnp.max(sc, axis=-1, keepdims=True))
        p = jnp.exp2(s - m)
        pc = jnp.exp2(sc - m)
        l = jnp.sum(p, axis=-1, keepdims=True) + jnp.sum(pc, axis=-1, keepdims=True)
        o = _dot(p.astype(BF16), vw[:, sl]) + _dot(pc.astype(BF16), vc[:, sl])
        o_ref[:, sl] = o / l


def natten_latent(pattn, bias, n_batch, t_lat, t_ctx):
    gw = GROUP_WIDTH
    rows = t_lat // GRID_W
    assert NA_QROWS == NA_KR // 2 and NA_BAND == NA_KR + NA_QROWS and rows % NA_QROWS == 0 and rows >= NA_BAND
    nblk = rows // NA_QROWS
    tq = NA_QROWS * GRID_W
    ctx0 = n_batch * t_lat // t_ctx
    return pl.pallas_call(
        functools.partial(_natten_kernel, rows=rows),
        grid=(n_batch, nblk),
        in_specs=[
            pl.BlockSpec((tq, gw), lambda b, r: (b * nblk + r, 2)),
            pl.BlockSpec((t_lat, gw), lambda b, r: (b, 3)),
            pl.BlockSpec((t_lat, gw), lambda b, r: (b, 4)),
            pl.BlockSpec((t_ctx, gw), lambda b, r: (ctx0 + b, 3)),
            pl.BlockSpec((t_ctx, gw), lambda b, r: (ctx0 + b, 4)),
            pl.BlockSpec(bias.shape, lambda b, r: (0, 0, 0, 0)),
        ],
        out_specs=pl.BlockSpec((tq, gw), lambda b, r: (b * nblk + r, 0)),
        out_shape=jax.ShapeDtypeStruct((n_batch * t_lat, gw), F32),
        compiler_params=_cparams(("parallel", "arbitrary")),
        name="natten_latent",
    )(pattn, pattn, pattn, pattn, pattn, bias)


def _ctx_attn_kernel(q_ref, k_ref, v_ref, o_ref):
    hd = HEAD_DIM
    q, k, v = q_ref[...], k_ref[...], v_ref[...]
    for h in range(NA_HEADS):
        sl = slice(h * hd, (h + 1) * hd)
        s = _dot_nt(q[:, sl], k[:, sl]) * (hd ** -0.5)
        m = jnp.max(s, axis=-1, keepdims=True)
        p = jnp.exp(s - m)
        l = jnp.sum(p, axis=-1, keepdims=True)
        o_ref[:, sl] = _dot(p.astype(BF16), v[:, sl]) / l


def natten_context(pattn, n_batch, t_lat, t_ctx):
    gw = GROUP_WIDTH
    ctx0 = n_batch * t_lat // t_ctx
    return pl.pallas_call(
        _ctx_attn_kernel,
        grid=(n_batch,),
        in_specs=[pl.BlockSpec((t_ctx, gw), lambda b, c=c: (ctx0 + b, c)) for c in (2, 3, 4)],
        out_specs=pl.BlockSpec((t_ctx, gw), lambda b: (b, 0)),
        out_shape=jax.ShapeDtypeStruct((n_batch * t_ctx, gw), F32),
        compiler_params=_cparams(("parallel",)),
        name="natten_context",
    )(pattn, pattn, pattn)


def _rope_tables(t_lat, t_ctx):
    half = HEAD_DIM // 4
    freqs = ROPE_BASE ** (-jnp.arange(half, dtype=F32) / half)
    t = jnp.arange(t_lat)
    ang_r = (t // GRID_W).astype(F32)[:, None] * freqs[None, :]
    ang_c = (t % GRID_W).astype(F32)[:, None] * freqs[None, :]
    ang = jnp.concatenate([ang_r, ang_r, ang_c, ang_c], axis=1)
    cos = jnp.concatenate([jnp.cos(ang), jnp.ones((t_ctx, HEAD_DIM), F32)], axis=0)
    sin = jnp.concatenate([jnp.sin(ang), jnp.zeros((t_ctx, HEAD_DIM), F32)], axis=0)
    return jnp.tile(cos, (1, GQA_Q_HEADS)), jnp.tile(sin, (1, GQA_Q_HEADS))


def _gqa_prep_kernel(x_ref, cos_ref, sin_ref, gq_ref, gk_ref, gm_ref, q_ref, k_ref, v_ref):
    hd = HEAD_DIM
    nq, nk = GQA_Q_HEADS * hd, GQA_KV_HEADS * hd
    x = x_ref[...].astype(F32)
    cos, sin = cos_ref[...], sin_ref[...]
    gm = gm_ref[...]

    def norm_rope(t, gain, width):
        t2 = t * t
        hi = t2.astype(BF16)
        lo = (t2 - hi.astype(F32)).astype(BF16)
        ms = _dot(hi, gm[:width, :width]) + _dot(lo, gm[:width, :width])
        tn = t * lax.rsqrt(ms + EPS) * gain
        lane = lax.broadcasted_iota(jnp.int32, tn.shape, 1)
        first = (lane % (hd // 2)) < (hd // 4)
        rot = jnp.where(first, -pltpu.roll(tn, width - hd // 4, 1), pltpu.roll(tn, hd // 4, 1))
        return tn * cos[:, :width] + rot * sin[:, :width]

    q = norm_rope(x[:, :nq], gq_ref[...], nq) * ((hd ** -0.5) * LOG2E)
    kt = norm_rope(x[:, nq:nq + nk], gk_ref[...], nk).T
    v = x_ref[:, nq + nk:nq + 2 * nk]
    for h in range(GQA_Q_HEADS):
        q_ref[h] = q[:, h * hd:(h + 1) * hd].astype(BF16)
    ones_col = (lax.broadcasted_iota(jnp.int32, (v.shape[0], hd), 1) == 0).astype(BF16)
    for h in range(GQA_KV_HEADS):
        k_ref[h] = kt[h * hd:(h + 1) * hd, :].astype(BF16)
        v_ref[h] = jnp.concatenate([v[:, h * hd:(h + 1) * hd], ones_col], axis=1)


def gqa_prep(pattn, gq, gk, n_batch, t_lat, t_ctx):
    tt = SEQ_TILE
    assert tt == KEY_CHUNK
    hd = HEAD_DIM
    ntl, ntc = t_lat // tt, t_ctx // tt
    s_len = t_lat + t_ctx
    cos, sin = _rope_tables(t_lat, t_ctx)
    nq = GQA_Q_HEADS * hd
    gmat = _block_diag(jnp.full((GQA_Q_HEADS, hd, hd), 1.0 / hd, F32)).astype(BF16)

    def pos(i):
        is_ctx = i >= n_batch * ntl
        c = i - n_batch * ntl
        return jnp.where(is_ctx, ntl + c % ntc, i % ntl)

    def bat(i):
        is_ctx = i >= n_batch * ntl
        return jnp.where(is_ctx, (i - n_batch * ntl) // ntc, i // ntl)

    return pl.pallas_call(
        _gqa_prep_kernel,
        grid=(n_batch * (ntl + ntc),),
        in_specs=[
            pl.BlockSpec((tt, 2 * nq), lambda i: (i, 0)),
            pl.BlockSpec((tt, nq), lambda i: (pos(i), 0)),
            pl.BlockSpec((tt, nq), lambda i: (pos(i), 0)),
            pl.BlockSpec((1, nq), lambda i: (0, 0)),
            pl.BlockSpec((1, nq // 2), lambda i: (0, 0)),
            pl.BlockSpec((nq, nq), lambda i: (0, 0)),
        ],
        out_specs=[
            pl.BlockSpec((None, GQA_Q_HEADS, tt, hd), lambda i: (bat(i), 0, pos(i), 0)),
            pl.BlockSpec((None, GQA_KV_HEADS, None, hd, tt), lambda i: (bat(i), 0, pos(i), 0, 0)),
            pl.BlockSpec((None, GQA_KV_HEADS, tt, 2 * hd), lambda i: (bat(i), 0, pos(i), 0)),
        ],
        out_shape=[
            jax.ShapeDtypeStruct((n_batch, GQA_Q_HEADS, s_len, hd), BF16),
            jax.ShapeDtypeStruct((n_batch, GQA_KV_HEADS, s_len // tt, hd, tt), BF16),
            jax.ShapeDtypeStruct((n_batch, GQA_KV_HEADS, s_len, 2 * hd), BF16),
        ],
        compiler_params=_cparams(("parallel",)),
        name="gqa_prep",
    )(pattn, cos, sin, jnp.tile(gq.astype(F32), GQA_Q_HEADS).reshape(1, nq),
      jnp.tile(gk.astype(F32), GQA_KV_HEADS).reshape(1, nq // 2), gmat)


def _gqa_kernel(q_ref, kt_ref, v_ref, o_ref, s_s, m_s, acc_s, *, unroll):
    g, tq, hd = q_ref.shape
    nch, _, ck = kt_ref.shape
    nl = ck // 128
    q = q_ref[...].reshape(g * tq, hd)
    m_s[...] = jnp.full_like(m_s, -jnp.inf)

    def scores(c, carry):
        s = _dot(q, kt_ref[c])
        s_s[c] = s
        fold = s[:, 0:128]
        for j in range(1, nl):
            fold = jnp.maximum(fold, s[:, j * 128:(j + 1) * 128])
        m_s[...] = jnp.maximum(m_s[...], fold)
        return carry

    lax.fori_loop(0, nch, scores, 0, unroll=unroll)
    m_s[...] = jnp.broadcast_to(jnp.max(m_s[...], axis=-1, keepdims=True), m_s.shape)
    acc_s[...] = jnp.zeros_like(acc_s)

    def values(grp, carry):
        m = m_s[...]
        ps = []
        for u in range(unroll):
            s = s_s[grp * unroll + u]
            ps += [jnp.exp2(s[:, j * 128:(j + 1) * 128] - m).astype(BF16) for j in range(nl)]
        off = pl.multiple_of(grp * (unroll * ck), unroll * ck)
        acc_s[...] += _dot(jnp.concatenate(ps, axis=1), v_ref[pl.ds(off, unroll * ck), :])
        return carry

    lax.fori_loop(0, nch // unroll, values, 0)
    acc = acc_s[...]
    o = acc[:, :hd] / acc[:, hd:hd + 1]
    o_ref[...] = jnp.concatenate([o[i * tq:(i + 1) * tq] for i in range(g)], axis=1)


def gqa_attention(q, kt, v, tq, q_blk0, nq, c_blk, nch):
    n_batch = q.shape[0]
    hd = HEAD_DIM
    g = GQA_GROUP
    ck = KEY_CHUNK
    unroll = next(u for u in (11, 3, 1) if nch % u == 0)
    return pl.pallas_call(
        functools.partial(_gqa_kernel, unroll=unroll),
        grid=(n_batch, GQA_KV_HEADS, nq),
        in_specs=[
            pl.BlockSpec((None, g, tq, hd), lambda b, h, i: (b, h, q_blk0 + i, 0)),
            pl.BlockSpec((None, None, nch, hd, ck), lambda b, h, i: (b, h, c_blk, 0, 0)),
            pl.BlockSpec((None, None, nch * ck, 2 * hd), lambda b, h, i: (b, h, c_blk, 0)),
        ],
        out_specs=pl.BlockSpec((tq, g * hd), lambda b, h, i: (b * nq + i, h)),
        out_shape=jax.ShapeDtypeStruct((n_batch * nq * tq, GQA_Q_HEADS * hd), F32),
        scratch_shapes=[
            pltpu.VMEM((nch, g * tq, ck), F32),
            pltpu.VMEM((g * tq, 128), F32),
            pltpu.VMEM((g * tq, 2 * hd), F32),
        ],
        compiler_params=_cparams(("parallel", "parallel", "arbitrary")),
        name="gqa_attention",
    )(q, kt, v)


def kernel(x, c, ctx, c_ctx, w_ada, b_ada, g_ffn1, w_ffn1_in, w_ffn1_out, g_mix, w_in, w_out, s5_lambda_re, s5_lambda_im, s5_log_step, s5_b_re, s5_b_im, s5_c_re, s5_c_im, s5_d, s5_w_glu, na_rpb, gqa_q_norm, gqa_k_norm, lru_conv_w, lru_conv_b, lru_w_a, lru_b_a, lru_w_x, lru_b_x, lru_lambda, g_ffn2, w_ffn2_in, w_ffn2_out, g_final):
    n_batch, t_lat, d = x.shape
    t_ctx = ctx.shape[1]
    depth = w_ada.shape[0]
    assert d == D_MODEL and t_lat % TOKEN_TILE == 0 and (n_batch * t_ctx) % TOKEN_TILE == 0
    assert t_lat % t_ctx == 0 and t_ctx % SEQ_TILE == 0 and t_lat % FLASH_TQ == 0
    n_lat = n_batch * t_lat
    n_all = n_lat + n_batch * t_ctx
    tiles_per_type = t_lat // TOKEN_TILE
    assert (n_batch * t_ctx) // TOKEN_TILE <= tiles_per_type

    c8 = jnp.zeros((8, D_MODEL), F32).at[:n_batch].set(c.astype(F32)).at[n_batch].set(c_ctx.astype(F32))
    mods_all = ada_mods(c8, w_ada, b_ada).reshape(depth, 8, N_MOD, D_MODEL)

    h = jnp.concatenate([x.reshape(n_lat, D_MODEL), ctx.reshape(n_batch * t_ctx, D_MODEL)], axis=0)
    s_len = t_lat + t_ctx

    for l in range(depth):
        need_ctx = l < depth - 1
        mods = mods_all[l]
        w1i, w1o = w_ffn1_in[l].astype(BF16), w_ffn1_out[l].astype(BF16)
        w2i, w2o = w_ffn2_in[l].astype(BF16), w_ffn2_out[l].astype(BF16)
        wl = w_in[l]
        w_in_perm = jnp.concatenate([wl[:, 0:256], wl[:, 1536:2048], wl[:, 1024:1536], wl[:, 256:1024]],
                                    axis=1).astype(BF16)

        h = ffn_half(h, mods, g_ffn1[l], w1i, w1o, 0, n_all, tiles_per_type)
        pscan, pattn = in_projection(h, mods, g_mix[l], w_in_perm, tiles_per_type)

        tables = _s5_tables(s5_lambda_re[l], s5_lambda_im[l], s5_log_step[l], s5_b_re[l], s5_b_im[l],
                            s5_c_re[l], s5_c_im[l])
        ys5, ys5_ctx = s5_scan(pscan, tables, n_batch, t_lat, t_ctx)
        if need_ctx:
            ys5 = jnp.concatenate([ys5, ys5_ctx], axis=0)

        lru_args = lambda dr: (lru_conv_w[l], lru_conv_b[l], _block_diag(lru_w_a[l, dr]).astype(BF16), lru_b_a[l, dr],
                               _block_diag(lru_w_x[l, dr]).astype(BF16), lru_b_x[l, dr], lru_lambda[l, dr])
        hf = lru_scan(pscan, *lru_args(0), False, n_batch, t_lat, t_ctx)
        hr = lru_scan(pscan, *lru_args(1), True, n_batch, t_lat, t_ctx)

        yb = natten_latent(pattn, _natten_bias(na_rpb[l]), n_batch, t_lat, t_ctx)

        qh, kth, vh = gqa_prep(pattn, gqa_q_norm[l], gqa_k_norm[l], n_batch, t_lat, t_ctx)
        yc = gqa_attention(qh, kth, vh, FLASH_TQ, 0, t_lat // FLASH_TQ, 0, s_len // KEY_CHUNK)

        if need_ctx:
            yb = jnp.concatenate([yb, natten_context(pattn, n_batch, t_lat, t_ctx)], axis=0)
            yc_ctx = gqa_attention(qh, kth, vh, t_ctx, t_lat // t_ctx, 1, t_lat // t_ctx, t_ctx // KEY_CHUNK)
            yc = jnp.concatenate([yc, yc_ctx], axis=0)

        n_rows = n_all if need_ctx else n_lat
        h = out_projection(h, mods, ys5, pscan, hf, hr, yb, yc, s5_d[l], s5_w_glu[l].astype(BF16),
                           w_out[l].astype(BF16), n_rows, tiles_per_type)
        h = ffn_half(h, mods, g_ffn2[l], w2i, w2o, 6, n_rows, tiles_per_type,
                     g_final=None if need_ctx else g_final)
    return h.reshape(n_batch, t_lat, D_MODEL)
```

```python
import functools
import math

import numpy as np
import jax
import jax.numpy as jnp
from jax import lax
from jax.experimental import pallas as pl
from jax.experimental.pallas import tpu as pltpu

F32 = jnp.float32
BF16 = jnp.bfloat16

D_MODEL = 1024
GRID_W = 64
HEAD_DIM = 64
GROUP_WIDTH = D_MODEL // 4
S5_CH = 16
S5_GROUPS = GROUP_WIDTH // S5_CH
S5_STATE = 64
NA_HEADS = GROUP_WIDTH // HEAD_DIM
NA_KR = 8
NA_KC = 16
GQA_Q_HEADS = GROUP_WIDTH // HEAD_DIM
GQA_KV_HEADS = GQA_Q_HEADS // 2
GQA_GROUP = GQA_Q_HEADS // GQA_KV_HEADS
LRU_BLOCKS = GROUP_WIDTH // HEAD_DIM
LRU_CONV = 4
LRU_C = 8.0
D_FF = ((8 * D_MODEL // 3 + 127) // 128) * 128
FFN_RES = 0.5
ROPE_BASE = 10000.0
EPS = 1e-6
N_MOD = 9

TOKEN_TILE = 512
SEQ_TILE = 256
LRU_SEG_PITCH = SEQ_TILE // 8 + 4
S5_CHUNK = 32
S5_OCT = 8
MXU_DEPTH = 256
FF_CHUNKS = ((0, 6 * MXU_DEPTH), (6 * MXU_DEPTH, D_FF))
FLASH_TQ = 512
KEY_CHUNK = 256
NA_QROWS = NA_KR // 2
NA_BAND = NA_KR + NA_QROWS
LOG2E = math.log2(math.e)
NEG_BIG = -1e30
VMEM_LIMIT = 56 * 1024 * 1024


def _cparams(sem):
    return pltpu.CompilerParams(dimension_semantics=sem, vmem_limit_bytes=VMEM_LIMIT)


def _dot(a, b):
    return jnp.dot(a, b, preferred_element_type=F32)


def _dot_nt(a, b):
    return lax.dot_general(a, b, (((1,), (1,)), ((), ())), preferred_element_type=F32)


def _ada_kernel(c_ref, w_ref, b_ref, o_ref):
    c = c_ref[...]
    s = c * jax.nn.sigmoid(c)
    o_ref[...] = _dot(s.astype(BF16), w_ref[...].astype(BF16)) + b_ref[...]


def ada_mods(c8, w_ada, b_ada):
    L = w_ada.shape[0]
    tn = D_MODEL
    return pl.pallas_call(
        _ada_kernel,
        grid=(L, N_MOD * D_MODEL // tn),
        in_specs=[
            pl.BlockSpec((8, D_MODEL), lambda l, j: (0, 0)),
            pl.BlockSpec((None, D_MODEL, tn), lambda l, j: (l, 0, j)),
            pl.BlockSpec((None, 1, tn), lambda l, j: (l, 0, j)),
        ],
        out_specs=pl.BlockSpec((None, 8, tn), lambda l, j: (l, 0, j)),
        out_shape=jax.ShapeDtypeStruct((L, 8, N_MOD * D_MODEL), F32),
        compiler_params=_cparams(("parallel", "parallel")),
        name="ada_mods",
    )(c8, w_ada, b_ada.reshape(L, 1, N_MOD * D_MODEL))


def _norm_mod(x, g, shift, scale):
    ms = jnp.mean(x * x, axis=-1, keepdims=True)
    y = x * lax.rsqrt(ms + EPS) * g
    return y * (1.0 + scale) + shift


def _ffn_core(x, mod_ref, g, wi_ref, wo_ref, k0):
    y = _norm_mod(x, g, mod_ref[k0:k0 + 1, :], mod_ref[k0 + 1:k0 + 2, :]).astype(BF16)
    acc = jnp.zeros(x.shape, F32)
    for lo, hi in FF_CHUNKS:
        a = _dot(y, wi_ref[:, lo:hi])
        b = _dot(y, wi_ref[:, D_FF + lo:D_FF + hi])
        h = (a * jax.nn.sigmoid(a) * b).astype(BF16)
        acc = acc + _dot(h, wo_ref[lo:hi, :])
    return x + FFN_RES * mod_ref[k0 + 2:k0 + 3, :] * acc


def _resident(shape):
    return pl.BlockSpec(shape, lambda i: (0,) * len(shape), pipeline_mode=pl.Buffered(1))


def _row_specs(n_lat_tiles, n_ctx_tiles, width, col=0):
    lat = pl.BlockSpec((TOKEN_TILE, width), lambda i: (jnp.minimum(i, n_lat_tiles - 1), col))
    if n_ctx_tiles == 0:
        return [lat]
    ctx = pl.BlockSpec((TOKEN_TILE, width), lambda i: (jnp.clip(i - n_lat_tiles, 0, n_ctx_tiles - 1), col))
    return [lat, ctx]


def _pick(refs, n_lat_tiles):
    if len(refs) == 1:
        return refs[0][...]
    return jnp.where(pl.program_id(0) >= n_lat_tiles, refs[1][...], refs[0][...])


def _ffn_in_kernel(*refs, n_src, n_lat_tiles):
    x_refs = refs[:n_src]
    mod_ref, g_ref, wi_ref, wo_ref, gm_ref, win_ref, h_ref, ps_ref, pa_ref = refs[n_src:]
    h = _ffn_core(_pick(x_refs, n_lat_tiles), mod_ref, g_ref[...], wi_ref, wo_ref, 0)
    h_ref[...] = h
    y = _norm_mod(h, gm_ref[...], mod_ref[3:4, :], mod_ref[4:5, :]).astype(BF16)
    p = _dot(y, win_ref[...])
    ns = ps_ref.shape[1]
    ps_ref[...] = p[:, :ns]
    pa_ref[...] = p[:, ns:].astype(BF16)


def ffn_in_projection(xs, mods, g, w_i, w_o, g_mix, w_in_perm, n_lat_tiles, n_ctx_tiles, tiles_per_type):
    nt = n_lat_tiles + n_ctx_tiles
    n = nt * TOKEN_TILE
    n_scan = 3 * GROUP_WIDTH
    n_attn = w_in_perm.shape[1] - n_scan
    x_specs = (_row_specs(n_lat_tiles, n_ctx_tiles, D_MODEL) if len(xs) == 2
               else [pl.BlockSpec((TOKEN_TILE, D_MODEL), lambda i: (i, 0))])
    vec = pl.BlockSpec((1, D_MODEL), lambda i: (0, 0))
    return pl.pallas_call(
        functools.partial(_ffn_in_kernel, n_src=len(xs), n_lat_tiles=n_lat_tiles),
        grid=(nt,),
        in_specs=x_specs + [
            pl.BlockSpec((None, N_MOD, D_MODEL), lambda i: (i // tiles_per_type, 0, 0)),
            vec, _resident((D_MODEL, 2 * D_FF)), _resident((D_FF, D_MODEL)),
            vec, _resident((D_MODEL, n_scan + n_attn)),
        ],
        out_specs=[
            pl.BlockSpec((TOKEN_TILE, D_MODEL), lambda i: (i, 0)),
            pl.BlockSpec((TOKEN_TILE, n_scan), lambda i: (i, 0)),
            pl.BlockSpec((TOKEN_TILE, n_attn), lambda i: (i, 0)),
        ],
        out_shape=[
            jax.ShapeDtypeStruct((n, D_MODEL), F32),
            jax.ShapeDtypeStruct((n, n_scan), F32),
            jax.ShapeDtypeStruct((n, n_attn), BF16),
        ],
        compiler_params=_cparams(("parallel",)),
        name="ffn_in_projection",
    )(*xs, mods, g.reshape(1, D_MODEL), w_i, w_o, g_mix.reshape(1, D_MODEL), w_in_perm)


def _mix_ffn_kernel(*refs, n_src, n_lat_tiles, final):
    gw = GROUP_WIDTH
    h_ref, mod_ref = refs[:2]
    ys5_refs, yb_refs, yc_refs = (refs[2 + k * n_src:2 + (k + 1) * n_src] for k in range(3))
    rest = refs[2 + 3 * n_src:]
    u_ref, hf_ref, hr_ref, gl_ref, dsk_ref, wglu_ref, wout_ref, g_ref, wi_ref, wo_ref = rest[:10]
    o_ref = rest[-1]
    ya = jax.nn.gelu(_pick(ys5_refs, n_lat_tiles) + dsk_ref[...] * u_ref[...])
    ya = ya * jax.nn.sigmoid(_dot(ya.astype(BF16), wglu_ref[...]))
    yd = (hf_ref[...] + hr_ref[...]) * jax.nn.gelu(gl_ref[...])
    acc = _dot(ya.astype(BF16), wout_ref[0:gw, :])
    acc = acc + _dot(_pick(yb_refs, n_lat_tiles).astype(BF16), wout_ref[gw:2 * gw, :])
    acc = acc + _dot(_pick(yc_refs, n_lat_tiles).astype(BF16), wout_ref[2 * gw:3 * gw, :])
    acc = acc + _dot(yd.astype(BF16), wout_ref[3 * gw:4 * gw, :])
    h = h_ref[...] + mod_ref[5:6, :] * acc
    out = _ffn_core(h, mod_ref, g_ref[...], wi_ref, wo_ref, 6)
    if final:
        ms = jnp.mean(out * out, axis=-1, keepdims=True)
        out = out * lax.rsqrt(ms + EPS) * rest[10][...]
    o_ref[...] = out


def mix_out_ffn(h, mods, ys5, yb, yc, pscan, hf, hr, d_skip, w_glu, w_out, g, w_i, w_o,
                n_lat_tiles, n_ctx_tiles, tiles_per_type, g_final=None):
    gw = GROUP_WIDTH
    final = g_final is not None
    nt = n_lat_tiles + n_ctx_tiles
    row = lambda c: pl.BlockSpec((TOKEN_TILE, gw), lambda i: (i, c))
    vec = pl.BlockSpec((1, D_MODEL), lambda i: (0, 0))
    mixer_specs = _row_specs(n_lat_tiles, n_ctx_tiles, gw)
    in_specs = ([pl.BlockSpec((TOKEN_TILE, D_MODEL), lambda i: (i, 0)),
                 pl.BlockSpec((None, N_MOD, D_MODEL), lambda i: (i // tiles_per_type, 0, 0))]
                + mixer_specs * 3
                + [row(0), row(0), row(0), row(2),
                   pl.BlockSpec((1, gw), lambda i: (0, 0)), _resident((gw, gw)), _resident((D_MODEL, D_MODEL)),
                   vec, _resident((D_MODEL, 2 * D_FF)), _resident((D_FF, D_MODEL))])
    args = [h, mods, *ys5, *yb, *yc, pscan, hf, hr, pscan, d_skip.reshape(1, gw), w_glu, w_out,
            g.reshape(1, D_MODEL), w_i, w_o]
    if final:
        in_specs.append(vec)
        args.append(g_final.reshape(1, D_MODEL))
    return pl.pallas_call(
        functools.partial(_mix_ffn_kernel, n_src=len(ys5), n_lat_tiles=n_lat_tiles, final=final),
        grid=(nt,),
        in_specs=in_specs,
        out_specs=pl.BlockSpec((TOKEN_TILE, D_MODEL), lambda i: (i, 0)),
        out_shape=jax.ShapeDtypeStruct((nt * TOKEN_TILE, D_MODEL), F32),
        compiler_params=_cparams(("parallel",)),
        name="mix_out_ffn",
    )(*args)


def _s5_tables(lam_re, lam_im, log_step, b_re, b_im, c_re, c_im):
    L = S5_CHUNK
    G, P, H = S5_GROUPS, S5_STATE, S5_CH
    hp = lax.Precision.HIGHEST
    W = L * H
    lr, li = lam_re.astype(F32), lam_im.astype(F32)
    dt = jnp.exp(log_step.astype(F32))[..., None]
    mag = jnp.exp(lr * dt)
    ar, ai = mag * jnp.cos(li * dt), mag * jnp.sin(li * dt)
    den = lr * lr + li * li
    fr = ((ar - 1) * lr + ai * li) / den
    fi = (ai * lr - (ar - 1) * li) / den
    br, bi = b_re.astype(F32), b_im.astype(F32)
    bbr = jnp.swapaxes(fr[..., None] * br - fi[..., None] * bi, 2, 3)
    bbi = jnp.swapaxes(fr[..., None] * bi + fi[..., None] * br, 2, 3)
    cr, ci = c_re.astype(F32), c_im.astype(F32)
    tau = jnp.arange(L + 1, dtype=F32)[:, None, None, None]
    pmag = jnp.exp(tau * (lr * dt)[None])
    pw_r, pw_i = pmag * jnp.cos(tau * (li * dt)[None]), pmag * jnp.sin(tau * (li * dt)[None])
    abr = pw_r[:L, :, :, None, :] * bbr[None] - pw_i[:L, :, :, None, :] * bbi[None]
    abi = pw_r[:L, :, :, None, :] * bbi[None] + pw_i[:L, :, :, None, :] * bbr[None]
    kk = (jnp.einsum('dgop,ldgip->dgilo', cr, abr, precision=hp)
          - jnp.einsum('dgop,ldgip->dgilo', ci, abi, precision=hp))
    mt = _s5_toeplitz(kk[0].reshape(G, H, W), kk[1][:, :, ::-1, :].reshape(G, H, W))

    def rows(x, flip):
        x = x[::-1] if flip else x
        return jnp.transpose(x, (1, 0, 2, 3)).reshape(G, W, P)

    bs = jnp.concatenate([rows(abr[:, 0], True), rows(abi[:, 0], True),
                          rows(abr[:, 1], False), rows(abi[:, 1], False)], axis=-1).astype(BF16)
    eo = np.tile(np.eye(H, dtype=np.float32), (1, L))
    et = np.repeat(np.eye(L, dtype=np.float32), H, axis=1)
    qr = jnp.stack([pw_r[1:, 0], pw_r[:0:-1, 1]], axis=1)
    qi = jnp.stack([pw_i[1:, 0], pw_i[:0:-1, 1]], axis=1)
    crx = jnp.einsum('dgop,ox->dgpx', cr, eo, precision=hp)
    cix = jnp.einsum('dgop,ox->dgpx', ci, eo, precision=hp)
    qrx = jnp.einsum('tdgp,tx->dgpx', qr, et, precision=hp)
    qix = jnp.einsum('tdgp,tx->dgpx', qi, et, precision=hp)
    ccr = crx * qrx - cix * qix
    cci = -crx * qix - cix * qrx
    cc = jnp.stack([ccr[0], cci[0], ccr[1], cci[1]], axis=1).astype(BF16)
    al = jnp.stack([pw_r[L, 0], pw_i[L, 0], pw_r[L, 1], pw_i[L, 1]])
    return mt, bs, cc, al


def _s5_toeplitz_kernel(pf_ref, pr_ref, o_ref):
    L, H = S5_CHUNK, S5_CH
    W = L * H
    pf, pr = pf_ref[...], pr_ref[...]
    lane = lax.broadcasted_iota(jnp.int32, (H, W), 1)
    for s in range(L):
        f = pf if s == 0 else jnp.where(lane >= s * H, pltpu.roll(pf, s * H, 1), 0.0)
        back = (L - 1 - s) * H
        r = pr if back == 0 else jnp.where(lane < (s + 1) * H, pltpu.roll(pr, W - back, 1), 0.0)
        o_ref[s * H:(s + 1) * H, :] = (f + r).astype(BF16)


def _s5_toeplitz(panel_f, panel_r):
    G, H, W = panel_f.shape
    return pl.pallas_call(
        _s5_toeplitz_kernel,
        grid=(G,),
        in_specs=[pl.BlockSpec((None, H, W), lambda g: (g, 0, 0))] * 2,
        out_specs=pl.BlockSpec((None, W, W), lambda g: (g, 0, 0)),
        out_shape=jax.ShapeDtypeStruct((G, W, W), BF16),
        compiler_params=_cparams(("parallel",)),
        name="s5_toeplitz",
    )(panel_f, panel_r)


def _s5_select_tables():
    H, O = S5_CH, S5_OCT
    pack = np.zeros((O // 2, O * 128, 256), np.float32)
    unpack = np.zeros((O // 2, O * 128, 256), np.float32)
    for j in range(8):
        for g in range(O):
            for h in range(H):
                pack[g // 2, j * 128 + g * H + h, (g % 2) * 128 + j * H + h] = 1.0
                unpack[j // 2, g * 128 + j * H + h, (j % 2) * 128 + g * H + h] = 1.0
    return jnp.asarray(pack, BF16), jnp.asarray(unpack, BF16)


def _s5_kernel(ul_ref, uc_ref, pack_ref, unpack_ref, mt_ref, bs_ref, cc_ref, al_ref, ol_ref, oc_ref,
               ug_s, y_s, s_s, h_s, *, ncl, ncc):
    L, P, O = S5_CHUNK, S5_STATE, S5_OCT
    nc = ncl + ncc
    nblk = L // 8

    for tb in range(nblk):
        zs = [jnp.concatenate([ul_ref[pl.ds(8 * tb + j, ncl, stride=L), :],
                               uc_ref[pl.ds(8 * tb + j, ncc, stride=L), :]], axis=0) for j in range(8)]
        zc = jnp.concatenate(zs, axis=1).astype(BF16)
        for gp in range(O // 2):
            r = _dot(zc, pack_ref[gp])
            ug_s[2 * gp, :, tb * 128:(tb + 1) * 128] = r[:, :128]
            ug_s[2 * gp + 1, :, tb * 128:(tb + 1) * 128] = r[:, 128:]

    for g in range(O):
        ug = ug_s[g].astype(BF16)
        y_s[g] = _dot(ug, mt_ref[g])
        st = _dot(ug, bs_ref[g])
        for k in range(4):
            s_s[k, :, g, :] = st[:, k * P:(k + 1) * P]

    afr, afi, arr, ari = al_ref[0], al_ref[1], al_ref[2], al_ref[3]

    def step(j, carry):
        hfr, hfi, hrr, hri = carry
        cf = jnp.where(j < ncc, ncl + j, j - ncc)
        cr = nc - 1 - j
        h_s[0, cf] = hfr
        h_s[1, cf] = hfi
        h_s[2, cr] = hrr
        h_s[3, cr] = hri
        return (afr * hfr - afi * hfi + s_s[0, cf], afr * hfi + afi * hfr + s_s[1, cf],
                arr * hrr - ari * hri + s_s[2, cr], arr * hri + ari * hrr + s_s[3, cr])

    zero = jnp.zeros((O, P), F32)
    lax.fori_loop(0, nc, step, (zero, zero, zero, zero))

    for g in range(O):
        y = y_s[g]
        for k in range(4):
            y = y + _dot(h_s[k, :, g, :].astype(BF16), cc_ref[g, k])
        y_s[g] = y

    for tb in range(nblk):
        yc = jnp.concatenate([y_s[g, :, tb * 128:(tb + 1) * 128] for g in range(O)], axis=1)
        hi = yc.astype(BF16)
        lo = (yc - hi.astype(F32)).astype(BF16)
        for jp in range(4):
            z = _dot(hi, unpack_ref[jp]) + _dot(lo, unpack_ref[jp])
            for e in range(2):
                t = 8 * tb + 2 * jp + e
                zt = z[:, e * 128:(e + 1) * 128]
                ol_ref[pl.ds(t, ncl, stride=L), :] = zt[:ncl]
                oc_ref[pl.ds(t, ncc, stride=L), :] = zt[ncl:]


def s5_scan(pscan, tables, n_batch, t_lat, t_ctx):
    mt, bs, cc, al = tables
    pack, unpack = _s5_select_tables()
    L, P, O = S5_CHUNK, S5_STATE, S5_OCT
    W = L * S5_CH
    ncl, ncc = t_lat // L, t_ctx // L
    nc = ncl + ncc
    ctx0 = n_batch * t_lat // t_ctx
    once = pl.Buffered(1)
    return pl.pallas_call(
        functools.partial(_s5_kernel, ncl=ncl, ncc=ncc),
        grid=(S5_GROUPS // O, n_batch),
        in_specs=[
            pl.BlockSpec((t_lat, O * S5_CH), lambda o, b: (b, o)),
            pl.BlockSpec((t_ctx, O * S5_CH), lambda o, b: (ctx0 + b, o)),
            pl.BlockSpec(pack.shape, lambda o, b: (0, 0, 0), pipeline_mode=once),
            pl.BlockSpec(unpack.shape, lambda o, b: (0, 0, 0), pipeline_mode=once),
            pl.BlockSpec((O, W, W), lambda o, b: (o, 0, 0), pipeline_mode=once),
            pl.BlockSpec((O, W, 4 * P), lambda o, b: (o, 0, 0), pipeline_mode=once),
            pl.BlockSpec((O, 4, P, W), lambda o, b: (o, 0, 0, 0), pipeline_mode=once),
            pl.BlockSpec((4, O, P), lambda o, b: (0, o, 0)),
        ],
        out_specs=[
            pl.BlockSpec((t_lat, O * S5_CH), lambda o, b: (b, o)),
            pl.BlockSpec((t_ctx, O * S5_CH), lambda o, b: (b, o)),
        ],
        out_shape=[
            jax.ShapeDtypeStruct((n_batch * t_lat, GROUP_WIDTH), F32),
            jax.ShapeDtypeStruct((n_batch * t_ctx, GROUP_WIDTH), F32),
        ],
        scratch_shapes=[
            pltpu.VMEM((O, nc, W), F32),
            pltpu.VMEM((O, nc, W), F32),
            pltpu.VMEM((4, nc, O, P), F32),
            pltpu.VMEM((4, nc, O, P), F32),
        ],
        compiler_params=_cparams(("arbitrary", "arbitrary")),
        name="s5_scan",
    )(pscan, pscan, pack, unpack, mt, bs, cc, al)


def _lru_kernel(xp_ref, xc_ref, xn_ref, cw_ref, cb_ref, wa_ref, ba_ref, wx_ref, bx_ref, lam_ref,
                o_ref, xs_s, a_s, b_s, h_s, *, reverse, ntl, ntc):
    tt = SEQ_TILE
    j = pl.program_id(1)
    is_ctx = j < ntc
    if reverse:
        k = jnp.where(is_ctx, ntc - 1 - j, ntl - 1 - (j - ntc))
    else:
        k = jnp.where(is_ctx, j, j - ntc)
    seg = jnp.where(is_ctx, ntc, ntl)
    has_prev = (k > 0).astype(F32)
    has_next = (k < seg - 1).astype(F32)

    @pl.when(j == 0)
    def _():
        h_s[...] = jnp.zeros_like(h_s)

    xs_s[0:8, :] = xp_ref[tt - 8:tt, :] * has_prev
    xs_s[8:8 + tt, :] = xc_ref[...]
    xs_s[8 + tt:16 + tt, :] = xn_ref[0:8, :] * has_next
    xc = cb_ref[...] + jnp.zeros((tt, GROUP_WIDTH), F32)
    for tap in range(LRU_CONV):
        xc = xc + cw_ref[tap:tap + 1, :] * xs_s[pl.ds(6 + tap, tt), :]
    xb = xc.astype(BF16)
    r = jax.nn.sigmoid(_dot(xb, wa_ref[...]) + ba_ref[...])
    i = jax.nn.sigmoid(_dot(xb, wx_ref[...]) + bx_ref[...])
    z = -lam_ref[...]
    softplus = jnp.maximum(z, 0.0) + jnp.log1p(jnp.exp(-jnp.abs(z)))
    log_a = -LRU_C * r * softplus
    a_all = jnp.exp(log_a)
    th = jnp.tanh(log_a)
    b_all = jnp.sqrt(-2.0 * th / (1.0 - th)) * (i * xc)

    nseg = 8
    slen = tt // nseg
    pitch = LRU_SEG_PITCH
    order = range(slen - 1, -1, -1) if reverse else range(slen)
    e_tile = h_s[...]
    e_next = []
    for hf in range(GROUP_WIDTH // 128):
        lanes = slice(hf * 128, (hf + 1) * 128)
        for sgm in range(nseg):
            a_s[hf, sgm * pitch:sgm * pitch + slen, :] = a_all[sgm * slen:(sgm + 1) * slen, lanes]
            b_s[hf, sgm * pitch:sgm * pitch + slen, :] = b_all[sgm * slen:(sgm + 1) * slen, lanes]
        h = jnp.zeros((nseg, 128), F32)
        p = jnp.ones((nseg, 128), F32)
        for i in order:
            pos = pl.ds(i, nseg, stride=pitch)
            a = a_s[hf, pos, :]
            h = a * h + b_s[hf, pos, :]
            p = a * p
            b_s[hf, pos, :] = h
            a_s[hf, pos, :] = p
        e = e_tile[:, lanes]
        ins = [None] * nseg
        for sgm in (range(nseg - 1, -1, -1) if reverse else range(nseg)):
            ins[sgm] = e
            e = h[sgm:sgm + 1, :] + p[sgm:sgm + 1, :] * e
        e_next.append(e)
        e_in = jnp.concatenate(ins, axis=0)
        for i in range(slen):
            pos = pl.ds(i, nseg, stride=pitch)
            b_s[hf, pos, :] = b_s[hf, pos, :] + a_s[hf, pos, :] * e_in
        for sgm in range(nseg):
            o_ref[sgm * slen:(sgm + 1) * slen, lanes] = b_s[hf, sgm * pitch:sgm * pitch + slen, :]
    h_s[...] = jnp.concatenate(e_next, axis=1)


def lru_scan(pscan, conv_w, conv_b, w_a, b_a, w_x, b_x, lam, reverse, n_batch, t_lat, t_ctx):
    tt = SEQ_TILE
    gw = GROUP_WIDTH
    ntl, ntc = t_lat // tt, t_ctx // tt
    n = pscan.shape[0]

    def tile(b, j, d):
        is_ctx = j < ntc
        if reverse:
            k = jnp.where(is_ctx, ntc - 1 - j, ntl - 1 - (j - ntc))
        else:
            k = jnp.where(is_ctx, j, j - ntc)
        seg = jnp.where(is_ctx, ntc, ntl)
        k = jnp.clip(k + d, 0, seg - 1)
        return jnp.where(is_ctx, n_batch * ntl + b * ntc + k, b * ntl + k)

    vec = pl.BlockSpec((1, gw), lambda b, j: (0, 0))
    mat = pl.BlockSpec((gw, gw), lambda b, j: (0, 0))
    return pl.pallas_call(
        functools.partial(_lru_kernel, reverse=reverse, ntl=ntl, ntc=ntc),
        grid=(n_batch, ntl + ntc),
        in_specs=[
            pl.BlockSpec((tt, gw), lambda b, j: (tile(b, j, -1), 1)),
            pl.BlockSpec((tt, gw), lambda b, j: (tile(b, j, 0), 1)),
            pl.BlockSpec((tt, gw), lambda b, j: (tile(b, j, 1), 1)),
            pl.BlockSpec((LRU_CONV, gw), lambda b, j: (0, 0)),
            vec, mat, vec, mat, vec, vec,
        ],
        out_specs=pl.BlockSpec((tt, gw), lambda b, j: (tile(b, j, 0), 0)),
        out_shape=jax.ShapeDtypeStruct((n, gw), F32),
        scratch_shapes=[
            pltpu.VMEM((tt + 16, gw), F32),
            pltpu.VMEM((gw // 128, 8 * LRU_SEG_PITCH, 128), F32),
            pltpu.VMEM((gw // 128, 8 * LRU_SEG_PITCH, 128), F32),
            pltpu.VMEM((1, gw), F32),
        ],
        compiler_params=_cparams(("parallel", "arbitrary")),
        name="lru_rev" if reverse else "lru_fwd",
    )(pscan, pscan, pscan, conv_w, conv_b.reshape(1, gw), w_a, b_a.reshape(1, gw), w_x, b_x.reshape(1, gw),
      lam.reshape(1, gw))


def _block_diag(w):
    n, d, e = w.shape
    eye = jnp.eye(n, dtype=w.dtype)
    return (eye[:, None, :, None] * w[:, :, None, :]).reshape(n * d, n * e)


def _natten_bias(rpb):
    W, KR, KC, QR, NB = GRID_W, NA_KR, NA_KC, NA_QROWS, NA_BAND
    col = np.arange(W)
    cs = np.clip(col - KC // 2, 0, W - KC)
    inwin = (col[None, :] >= cs[:, None]) & (col[None, :] < cs[:, None] + KC)
    coff = np.clip(col[None, :] - col[:, None] + (KC - 1), 0, 2 * KC - 2)
    a = np.arange(QR)[:, None]
    i = np.arange(NB)[None, :]
    first = (i - a, (i < KR) & (a >= 0))
    mid = (i - a - KR // 2, (i >= a) & (i < a + KR))
    last = (i - a - NB + QR, (i >= NB - KR) & (a >= 0))
    coh = (coff[:, :, None] == np.arange(2 * KC - 1)).astype(np.float32)
    t = jnp.einsum('hrc,qkc->hrqk', rpb.astype(F32), coh, precision=lax.Precision.HIGHEST)
    t = t * LOG2E + np.where(inwin, 0.0, NEG_BIG).astype(np.float32)
    plan = tuple(
        tuple(tuple(int(np.clip(delta[ai, ii] + KR - 1, 0, 2 * KR - 2)) if valid[ai, ii] else -1
                    for ii in range(NB)) for ai in range(QR))
        for delta, valid in (first, mid, last))

    def expand(t_ref, o_ref):
        for v in range(3):
            for ai in range(QR):
                for ii in range(NB):
                    r = plan[v][ai][ii]
                    blk = t_ref[r] if r >= 0 else jnp.full((W, W), NEG_BIG, F32)
                    o_ref[v, ai * W:(ai + 1) * W, ii * W:(ii + 1) * W] = blk

    nh = rpb.shape[0]
    return pl.pallas_call(
        expand,
        grid=(nh,),
        in_specs=[pl.BlockSpec((None, 2 * KR - 1, W, W), lambda h: (h, 0, 0, 0))],
        out_specs=pl.BlockSpec((3, None, QR * W, NB * W), lambda h: (0, h, 0, 0)),
        out_shape=jax.ShapeDtypeStruct((3, nh, QR * W, NB * W), F32),
        compiler_params=_cparams(("parallel",)),
        name="natten_bias",
    )(t)


def _natten_kernel(q_ref, k_ref, v_ref, kc_ref, vc_ref, bias_ref, o_ref, *, rows):
    W, KR, hd = GRID_W, NA_KR, HEAD_DIM
    blk = pl.program_id(1)
    nblk = rows // NA_QROWS
    bs = jnp.clip(blk * NA_QROWS - KR // 2, 0, rows - NA_BAND)
    var = jnp.where(blk == 0, 0, jnp.where(blk == nblk - 1, 2, 1))
    start = pl.multiple_of(bs * W, W)
    q = q_ref[...]
    kw = k_ref[pl.ds(start, NA_BAND * W), :]
    vw = v_ref[pl.ds(start, NA_BAND * W), :]
    kc = kc_ref[...]
    vc = vc_ref[...]
    scale2 = (hd ** -0.5) * LOG2E
    for h in range(NA_HEADS):
        sl = slice(h * hd, (h + 1) * hd)
        qh = q[:, sl]
        s = _dot_nt(qh, kw[:, sl]) * scale2 + bias_ref[var, h]
        sc = _dot_nt(qh, kc[:, sl]) * scale2
        m = jnp.maximum(jnp.max(s, axis=-1, keepdims=True), jnp.max(sc, axis=-1, keepdims=True))
        p = jnp.exp2(s - m)
        pc = jnp.exp2(sc - m)
        l = jnp.sum(p, axis=-1, keepdims=True) + jnp.sum(pc, axis=-1, keepdims=True)
        o = _dot(p.astype(BF16), vw[:, sl]) + _dot(pc.astype(BF16), vc[:, sl])
        o_ref[:, sl] = o / l


def natten_latent(pattn, bias, n_batch, t_lat, t_ctx):
    gw = GROUP_WIDTH
    rows = t_lat // GRID_W
    assert NA_QROWS == NA_KR // 2 and NA_BAND == NA_KR + NA_QROWS and rows % NA_QROWS == 0 and rows >= NA_BAND
    nblk = rows // NA_QROWS
    tq = NA_QROWS * GRID_W
    ctx0 = n_batch * t_lat // t_ctx
    return pl.pallas_call(
        functools.partial(_natten_kernel, rows=rows),
        grid=(n_batch, nblk),
        in_specs=[
            pl.BlockSpec((tq, gw), lambda b, r: (b * nblk + r, 2)),
            pl.BlockSpec((t_lat, gw), lambda b, r: (b, 3)),
            pl.BlockSpec((t_lat, gw), lambda b, r: (b, 4)),
            pl.BlockSpec((t_ctx, gw), lambda b, r: (ctx0 + b, 3)),
            pl.BlockSpec((t_ctx, gw), lambda b, r: (ctx0 + b, 4)),
            pl.BlockSpec(bias.shape, lambda b, r: (0, 0, 0, 0)),
        ],
        out_specs=pl.BlockSpec((tq, gw), lambda b, r: (b * nblk + r, 0)),
        out_shape=jax.ShapeDtypeStruct((n_batch * t_lat, gw), F32),
        compiler_params=_cparams(("parallel", "arbitrary")),
        name="natten_latent",
    )(pattn, pattn, pattn, pattn, pattn, bias)


def _ctx_attn_kernel(q_ref, k_ref, v_ref, o_ref):
    hd = HEAD_DIM
    q, k, v = q_ref[...], k_ref[...], v_ref[...]
    for h in range(NA_HEADS):
        sl = slice(h * hd, (h + 1) * hd)
        s = _dot_nt(q[:, sl], k[:, sl]) * (hd ** -0.5)
        m = jnp.max(s, axis=-1, keepdims=True)
        p = jnp.exp(s - m)
        l = jnp.sum(p, axis=-1, keepdims=True)
        o_ref[:, sl] = _dot(p.astype(BF16), v[:, sl]) / l


def natten_context(pattn, n_batch, t_lat, t_ctx):
    gw = GROUP_WIDTH
    ctx0 = n_batch * t_lat // t_ctx
    return pl.pallas_call(
        _ctx_attn_kernel,
        grid=(n_batch,),
        in_specs=[pl.BlockSpec((t_ctx, gw), lambda b, c=c: (ctx0 + b, c)) for c in (2, 3, 4)],
        out_specs=pl.BlockSpec((t_ctx, gw), lambda b: (b, 0)),
        out_shape=jax.ShapeDtypeStruct((n_batch * t_ctx, gw), F32),
        compiler_params=_cparams(("parallel",)),
        name="natten_context",
    )(pattn, pattn, pattn)


def _rope_tables(t_lat, t_ctx):
    half = HEAD_DIM // 4
    freqs = ROPE_BASE ** (-jnp.arange(half, dtype=F32) / half)
    t = jnp.arange(t_lat)
    ang_r = (t // GRID_W).astype(F32)[:, None] * freqs[None, :]
    ang_c = (t % GRID_W).astype(F32)[:, None] * freqs[None, :]
    ang = jnp.concatenate([ang_r, ang_r, ang_c, ang_c], axis=1)
    cos = jnp.concatenate([jnp.cos(ang), jnp.ones((t_ctx, HEAD_DIM), F32)], axis=0)
    sin = jnp.concatenate([jnp.sin(ang), jnp.zeros((t_ctx, HEAD_DIM), F32)], axis=0)
    return jnp.tile(cos, (1, GQA_Q_HEADS)), jnp.tile(sin, (1, GQA_Q_HEADS))


def _gqa_prep_kernel(x_ref, cos_ref, sin_ref, gq_ref, gk_ref, gm_ref, q_ref, k_ref, v_ref):
    hd = HEAD_DIM
    nq, nk = GQA_Q_HEADS * hd, GQA_KV_HEADS * hd
    x = x_ref[...].astype(F32)
    cos, sin = cos_ref[...], sin_ref[...]
    gm = gm_ref[...]

    def norm_rope(t, gain, width):
        t2 = t * t
        hi = t2.astype(BF16)
        lo = (t2 - hi.astype(F32)).astype(BF16)
        ms = _dot(hi, gm[:width, :width]) + _dot(lo, gm[:width, :width])
        tn = t * lax.rsqrt(ms + EPS) * gain
        lane = lax.broadcasted_iota(jnp.int32, tn.shape, 1)
        first = (lane % (hd // 2)) < (hd // 4)
        rot = jnp.where(first, -pltpu.roll(tn, width - hd // 4, 1), pltpu.roll(tn, hd // 4, 1))
        return tn * cos[:, :width] + rot * sin[:, :width]

    q = norm_rope(x[:, :nq], gq_ref[...], nq) * ((hd ** -0.5) * LOG2E)
    kt = norm_rope(x[:, nq:nq + nk], gk_ref[...], nk).T
    v = x_ref[:, nq + nk:nq + 2 * nk]
    for h in range(GQA_Q_HEADS):
        q_ref[h] = q[:, h * hd:(h + 1) * hd].astype(BF16)
    ones_col = (lax.broadcasted_iota(jnp.int32, (v.shape[0], hd), 1) == 0).astype(BF16)
    for h in range(GQA_KV_HEADS):
        k_ref[h] = kt[h * hd:(h + 1) * hd, :].astype(BF16)
        v_ref[h] = jnp.concatenate([v[:, h * hd:(h + 1) * hd], ones_col], axis=1)


def gqa_prep(pattn, gq, gk, n_batch, t_lat, t_ctx):
    tt = SEQ_TILE
    assert tt == KEY_CHUNK
    hd = HEAD_DIM
    ntl, ntc = t_lat // tt, t_ctx // tt
    s_len = t_lat + t_ctx
    cos, sin = _rope_tables(t_lat, t_ctx)
    nq = GQA_Q_HEADS * hd
    gmat = _block_diag(jnp.full((GQA_Q_HEADS, hd, hd), 1.0 / hd, F32)).astype(BF16)

    def pos(i):
        is_ctx = i >= n_batch * ntl
        c = i - n_batch * ntl
        return jnp.where(is_ctx, ntl + c % ntc, i % ntl)

    def bat(i):
        is_ctx = i >= n_batch * ntl
        return jnp.where(is_ctx, (i - n_batch * ntl) // ntc, i // ntl)

    return pl.pallas_call(
        _gqa_prep_kernel,
        grid=(n_batch * (ntl + ntc),),
        in_specs=[
            pl.BlockSpec((tt, 2 * nq), lambda i: (i, 0)),
            pl.BlockSpec((tt, nq), lambda i: (pos(i), 0)),
            pl.BlockSpec((tt, nq), lambda i: (pos(i), 0)),
            pl.BlockSpec((1, nq), lambda i: (0, 0)),
            pl.BlockSpec((1, nq // 2), lambda i: (0, 0)),
            pl.BlockSpec((nq, nq), lambda i: (0, 0)),
        ],
        out_specs=[
            pl.BlockSpec((None, GQA_Q_HEADS, tt, hd), lambda i: (bat(i), 0, pos(i), 0)),
            pl.BlockSpec((None, GQA_KV_HEADS, None, hd, tt), lambda i: (bat(i), 0, pos(i), 0, 0)),
            pl.BlockSpec((None, GQA_KV_HEADS, tt, 2 * hd), lambda i: (bat(i), 0, pos(i), 0)),
        ],
        out_shape=[
            jax.ShapeDtypeStruct((n_batch, GQA_Q_HEADS, s_len, hd), BF16),
            jax.ShapeDtypeStruct((n_batch, GQA_KV_HEADS, s_len // tt, hd, tt), BF16),
            jax.ShapeDtypeStruct((n_batch, GQA_KV_HEADS, s_len, 2 * hd), BF16),
        ],
        compiler_params=_cparams(("parallel",)),
        name="gqa_prep",
    )(pattn, cos, sin, jnp.tile(gq.astype(F32), GQA_Q_HEADS).reshape(1, nq),
      jnp.tile(gk.astype(F32), GQA_KV_HEADS).reshape(1, nq // 2), gmat)


def _gqa_kernel(q_ref, kt_ref, v_ref, o_ref, s_s, m_s, acc_s, *, unroll):
    g, tq, hd = q_ref.shape
    nch, _, ck = kt_ref.shape
    nl = ck // 128
    q = q_ref[...].reshape(g * tq, hd)
    m_s[...] = jnp.full_like(m_s, -jnp.inf)

    def scores(c, carry):
        s = _dot(q, kt_ref[c])
        s_s[c] = s
        fold = s[:, 0:128]
        for j in range(1, nl):
            fold = jnp.maximum(fold, s[:, j * 128:(j + 1) * 128])
        m_s[...] = jnp.maximum(m_s[...], fold)
        return carry

    lax.fori_loop(0, nch, scores, 0, unroll=unroll)
    m_s[...] = jnp.broadcast_to(jnp.max(m_s[...], axis=-1, keepdims=True), m_s.shape)
    acc_s[...] = jnp.zeros_like(acc_s)

    def values(grp, carry):
        m = m_s[...]
        ps = []
        for u in range(unroll):
            s = s_s[grp * unroll + u]
            ps += [jnp.exp2(s[:, j * 128:(j + 1) * 128] - m).astype(BF16) for j in range(nl)]
        off = pl.multiple_of(grp * (unroll * ck), unroll * ck)
        acc_s[...] += _dot(jnp.concatenate(ps, axis=1), v_ref[pl.ds(off, unroll * ck), :])
        return carry

    lax.fori_loop(0, nch // unroll, values, 0)
    acc = acc_s[...]
    o = acc[:, :hd] / acc[:, hd:hd + 1]
    o_ref[...] = jnp.concatenate([o[i * tq:(i + 1) * tq] for i in range(g)], axis=1)


def gqa_attention(q, kt, v, tq, q_blk0, nq, c_blk, nch):
    n_batch = q.shape[0]
    hd = HEAD_DIM
    g = GQA_GROUP
    ck = KEY_CHUNK
    unroll = next(u for u in (11, 3, 1) if nch % u == 0)
    return pl.pallas_call(
        functools.partial(_gqa_kernel, unroll=unroll),
        grid=(n_batch, GQA_KV_HEADS, nq),
        in_specs=[
            pl.BlockSpec((None, g, tq, hd), lambda b, h, i: (b, h, q_blk0 + i, 0)),
            pl.BlockSpec((None, None, nch, hd, ck), lambda b, h, i: (b, h, c_blk, 0, 0)),
            pl.BlockSpec((None, None, nch * ck, 2 * hd), lambda b, h, i: (b, h, c_blk, 0)),
        ],
        out_specs=pl.BlockSpec((tq, g * hd), lambda b, h, i: (b * nq + i, h)),
        out_shape=jax.ShapeDtypeStruct((n_batch * nq * tq, GQA_Q_HEADS * hd), F32),
        scratch_shapes=[
            pltpu.VMEM((nch, g * tq, ck), F32),
            pltpu.VMEM((g * tq, 128), F32),
            pltpu.VMEM((g * tq, 2 * hd), F32),
        ],
        compiler_params=_cparams(("parallel", "parallel", "arbitrary")),
        name="gqa_attention",
    )(q, kt, v)


def kernel(x, c, ctx, c_ctx, w_ada, b_ada, g_ffn1, w_ffn1_in, w_ffn1_out, g_mix, w_in, w_out, s5_lambda_re, s5_lambda_im, s5_log_step, s5_b_re, s5_b_im, s5_c_re, s5_c_im, s5_d, s5_w_glu, na_rpb, gqa_q_norm, gqa_k_norm, lru_conv_w, lru_conv_b, lru_w_a, lru_b_a, lru_w_x, lru_b_x, lru_lambda, g_ffn2, w_ffn2_in, w_ffn2_out, g_final):
    n_batch, t_lat, d = x.shape
    t_ctx = ctx.shape[1]
    depth = w_ada.shape[0]
    assert d == D_MODEL and t_lat % TOKEN_TILE == 0 and (n_batch * t_ctx) % TOKEN_TILE == 0
    assert t_lat % t_ctx == 0 and t_ctx % SEQ_TILE == 0 and t_lat % FLASH_TQ == 0
    n_lat = n_batch * t_lat
    n_all = n_lat + n_batch * t_ctx
    tiles_per_type = t_lat // TOKEN_TILE
    assert (n_batch * t_ctx) // TOKEN_TILE <= tiles_per_type

    c8 = jnp.zeros((8, D_MODEL), F32).at[:n_batch].set(c.astype(F32)).at[n_batch].set(c_ctx.astype(F32))
    mods_all = ada_mods(c8, w_ada, b_ada).reshape(depth, 8, N_MOD, D_MODEL)

    s_len = t_lat + t_ctx
    n_lat_tiles = n_lat // TOKEN_TILE
    n_ctx_tiles = (n_all - n_lat) // TOKEN_TILE
    hs = [x.reshape(n_lat, D_MODEL), ctx.reshape(n_batch * t_ctx, D_MODEL)]

    for l in range(depth):
        need_ctx = l < depth - 1
        mods = mods_all[l]
        w1i, w1o = w_ffn1_in[l].astype(BF16), w_ffn1_out[l].astype(BF16)
        w2i, w2o = w_ffn2_in[l].astype(BF16), w_ffn2_out[l].astype(BF16)
        wl = w_in[l]
        w_in_perm = jnp.concatenate([wl[:, 0:256], wl[:, 1536:2048], wl[:, 1024:1536], wl[:, 256:1024]],
                                    axis=1).astype(BF16)

        h, pscan, pattn = ffn_in_projection(hs, mods, g_ffn1[l], w1i, w1o, g_mix[l], w_in_perm,
                                            n_lat_tiles, n_ctx_tiles, tiles_per_type)

        tables = _s5_tables(s5_lambda_re[l], s5_lambda_im[l], s5_log_step[l], s5_b_re[l], s5_b_im[l],
                            s5_c_re[l], s5_c_im[l])
        ys5 = list(s5_scan(pscan, tables, n_batch, t_lat, t_ctx))

        lru_args = lambda dr: (lru_conv_w[l], lru_conv_b[l], _block_diag(lru_w_a[l, dr]).astype(BF16), lru_b_a[l, dr],
                               _block_diag(lru_w_x[l, dr]).astype(BF16), lru_b_x[l, dr], lru_lambda[l, dr])
        hf = lru_scan(pscan, *lru_args(0), False, n_batch, t_lat, t_ctx)
        hr = lru_scan(pscan, *lru_args(1), True, n_batch, t_lat, t_ctx)

        yb = [natten_latent(pattn, _natten_bias(na_rpb[l]), n_batch, t_lat, t_ctx)]

        qh, kth, vh = gqa_prep(pattn, gqa_q_norm[l], gqa_k_norm[l], n_batch, t_lat, t_ctx)
        yc = [gqa_attention(qh, kth, vh, FLASH_TQ, 0, t_lat // FLASH_TQ, 0, s_len // KEY_CHUNK)]

        if need_ctx:
            yb.append(natten_context(pattn, n_batch, t_lat, t_ctx))
            yc.append(gqa_attention(qh, kth, vh, t_ctx, t_lat // t_ctx, 1, t_lat // t_ctx, t_ctx // KEY_CHUNK))
        else:
            ys5 = ys5[:1]

        h = mix_out_ffn(h, mods, ys5, yb, yc, pscan, hf, hr, s5_d[l], s5_w_glu[l].astype(BF16),
                        w_out[l].astype(BF16), g_ffn2[l], w2i, w2o, n_lat_tiles,
                        n_ctx_tiles if need_ctx else 0, tiles_per_type, g_final=None if need_ctx else g_final)
        hs = [h]
    return h.reshape(n_batch, t_lat, D_MODEL)
```

```python
import functools
import math

import numpy as np
import jax
import jax.numpy as jnp
from jax import lax
from jax.experimental import pallas as pl
from jax.experimental.pallas import tpu as pltpu

F32 = jnp.float32
BF16 = jnp.bfloat16

D_MODEL = 1024
GRID_W = 64
HEAD_DIM = 64
GROUP_WIDTH = D_MODEL // 4
S5_CH = 16
S5_GROUPS = GROUP_WIDTH // S5_CH
S5_STATE = 64
NA_HEADS = GROUP_WIDTH // HEAD_DIM
NA_KR = 8
NA_KC = 16
GQA_Q_HEADS = GROUP_WIDTH // HEAD_DIM
GQA_KV_HEADS = GQA_Q_HEADS // 2
GQA_GROUP = GQA_Q_HEADS // GQA_KV_HEADS
LRU_BLOCKS = GROUP_WIDTH // HEAD_DIM
LRU_CONV = 4
LRU_C = 8.0
D_FF = ((8 * D_MODEL // 3 + 127) // 128) * 128
FFN_RES = 0.5
ROPE_BASE = 10000.0
EPS = 1e-6
N_MOD = 9

TOKEN_TILE = 512
SEQ_TILE = 256
LRU_SEG_PITCH = SEQ_TILE // 8 + 4
S5_CHUNK = 32
S5_OCT = 8
MXU_DEPTH = 256
FF_CHUNKS = ((0, 6 * MXU_DEPTH), (6 * MXU_DEPTH, D_FF))
FLASH_TQ = 512
KEY_CHUNK = 256
NA_QROWS = NA_KR // 2
NA_BAND = NA_KR + NA_QROWS
LOG2E = math.log2(math.e)
NEG_BIG = -1e30
VMEM_LIMIT = 56 * 1024 * 1024


def _cparams(sem):
    return pltpu.CompilerParams(dimension_semantics=sem, vmem_limit_bytes=VMEM_LIMIT)


def _dot(a, b):
    return jnp.dot(a, b, preferred_element_type=F32)


def _dot_nt(a, b):
    return lax.dot_general(a, b, (((1,), (1,)), ((), ())), preferred_element_type=F32)


def _ada_kernel(c_ref, w_ref, b_ref, o_ref):
    c = c_ref[...]
    s = c * jax.nn.sigmoid(c)
    o_ref[...] = _dot(s.astype(BF16), w_ref[...].astype(BF16)) + b_ref[...]


def ada_mods(c8, w_ada, b_ada):
    L = w_ada.shape[0]
    tn = D_MODEL
    return pl.pallas_call(
        _ada_kernel,
        grid=(L, N_MOD * D_MODEL // tn),
        in_specs=[
            pl.BlockSpec((8, D_MODEL), lambda l, j: (0, 0)),
            pl.BlockSpec((None, D_MODEL, tn), lambda l, j: (l, 0, j)),
            pl.BlockSpec((None, 1, tn), lambda l, j: (l, 0, j)),
        ],
        out_specs=pl.BlockSpec((None, 8, tn), lambda l, j: (l, 0, j)),
        out_shape=jax.ShapeDtypeStruct((L, 8, N_MOD * D_MODEL), F32),
        compiler_params=_cparams(("parallel", "parallel")),
        name="ada_mods",
    )(c8, w_ada, b_ada.reshape(L, 1, N_MOD * D_MODEL))


def _norm_mod(x, g, shift, scale):
    ms = jnp.mean(x * x, axis=-1, keepdims=True)
    y = x * lax.rsqrt(ms + EPS) * g
    return y * (1.0 + scale) + shift


def _ffn_core(x, mod_ref, g, wi_ref, wo_ref, k0):
    y = _norm_mod(x, g, mod_ref[k0:k0 + 1, :], mod_ref[k0 + 1:k0 + 2, :]).astype(BF16)
    acc = jnp.zeros(x.shape, F32)
    for lo, hi in FF_CHUNKS:
        a = _dot(y, wi_ref[:, lo:hi])
        b = _dot(y, wi_ref[:, D_FF + lo:D_FF + hi])
        h = (a * jax.nn.sigmoid(a) * b).astype(BF16)
        acc = acc + _dot(h, wo_ref[lo:hi, :])
    return x + FFN_RES * mod_ref[k0 + 2:k0 + 3, :] * acc


def _resident(shape):
    return pl.BlockSpec(shape, lambda i: (0,) * len(shape), pipeline_mode=pl.Buffered(1))


def _row_specs(n_lat_tiles, n_ctx_tiles, width, col=0):
    lat = pl.BlockSpec((TOKEN_TILE, width), lambda i: (jnp.minimum(i, n_lat_tiles - 1), col))
    if n_ctx_tiles == 0:
        return [lat]
    ctx = pl.BlockSpec((TOKEN_TILE, width), lambda i: (jnp.clip(i - n_lat_tiles, 0, n_ctx_tiles - 1), col))
    return [lat, ctx]


def _pick(refs, n_lat_tiles):
    if len(refs) == 1:
        return refs[0][...]
    return jnp.where(pl.program_id(0) >= n_lat_tiles, refs[1][...], refs[0][...])


def _ffn_in_kernel(*refs, n_src, n_lat_tiles):
    x_refs = refs[:n_src]
    mod_ref, g_ref, wi_ref, wo_ref, gm_ref, win_ref, h_ref, ps_ref, pa_ref = refs[n_src:]
    h = _ffn_core(_pick(x_refs, n_lat_tiles), mod_ref, g_ref[...], wi_ref, wo_ref, 0)
    h_ref[...] = h
    y = _norm_mod(h, gm_ref[...], mod_ref[3:4, :], mod_ref[4:5, :]).astype(BF16)
    p = _dot(y, win_ref[...])
    gw = GROUP_WIDTH
    ps_ref[...] = jnp.concatenate([p[:, 0:gw], p[:, 6 * gw:8 * gw]], axis=1)
    pa_ref[...] = jnp.concatenate([p[:, 4 * gw:6 * gw], p[:, gw:4 * gw]], axis=1).astype(BF16)


def ffn_in_projection(xs, mods, g, w_i, w_o, g_mix, w_in_l, n_lat_tiles, n_ctx_tiles, tiles_per_type):
    nt = n_lat_tiles + n_ctx_tiles
    n = nt * TOKEN_TILE
    n_scan = 3 * GROUP_WIDTH
    n_attn = w_in_l.shape[1] - n_scan
    x_specs = (_row_specs(n_lat_tiles, n_ctx_tiles, D_MODEL) if len(xs) == 2
               else [pl.BlockSpec((TOKEN_TILE, D_MODEL), lambda i: (i, 0))])
    vec = pl.BlockSpec((1, D_MODEL), lambda i: (0, 0))
    return pl.pallas_call(
        functools.partial(_ffn_in_kernel, n_src=len(xs), n_lat_tiles=n_lat_tiles),
        grid=(nt,),
        in_specs=x_specs + [
            pl.BlockSpec((None, N_MOD, D_MODEL), lambda i: (i // tiles_per_type, 0, 0)),
            vec, _resident((D_MODEL, 2 * D_FF)), _resident((D_FF, D_MODEL)),
            vec, _resident((D_MODEL, n_scan + n_attn)),
        ],
        out_specs=[
            pl.BlockSpec((TOKEN_TILE, D_MODEL), lambda i: (i, 0)),
            pl.BlockSpec((TOKEN_TILE, n_scan), lambda i: (i, 0)),
            pl.BlockSpec((TOKEN_TILE, n_attn), lambda i: (i, 0)),
        ],
        out_shape=[
            jax.ShapeDtypeStruct((n, D_MODEL), F32),
            jax.ShapeDtypeStruct((n, n_scan), F32),
            jax.ShapeDtypeStruct((n, n_attn), BF16),
        ],
        compiler_params=_cparams(("parallel",)),
        name="ffn_in_projection",
    )(*xs, mods, g.reshape(1, D_MODEL), w_i, w_o, g_mix.reshape(1, D_MODEL), w_in_l)


def _mix_ffn_kernel(*refs, n_src, n_lat_tiles, final):
    gw = GROUP_WIDTH
    h_ref, mod_ref = refs[:2]
    ys5_refs, yb_refs, yc_refs, hf_refs, hr_refs = (refs[2 + k * n_src:2 + (k + 1) * n_src] for k in range(5))
    rest = refs[2 + 5 * n_src:]
    u_ref, gl_ref, dsk_ref, wglu_ref, wout_ref, g_ref, wi_ref, wo_ref = rest[:8]
    o_ref = rest[-1]
    ya = jax.nn.gelu(_pick(ys5_refs, n_lat_tiles) + dsk_ref[...] * u_ref[...])
    ya = ya * jax.nn.sigmoid(_dot(ya.astype(BF16), wglu_ref[...]))
    yd = (_pick(hf_refs, n_lat_tiles) + _pick(hr_refs, n_lat_tiles)) * jax.nn.gelu(gl_ref[...])
    acc = _dot(ya.astype(BF16), wout_ref[0:gw, :])
    acc = acc + _dot(_pick(yb_refs, n_lat_tiles).astype(BF16), wout_ref[gw:2 * gw, :])
    acc = acc + _dot(_pick(yc_refs, n_lat_tiles).astype(BF16), wout_ref[2 * gw:3 * gw, :])
    acc = acc + _dot(yd.astype(BF16), wout_ref[3 * gw:4 * gw, :])
    h = h_ref[...] + mod_ref[5:6, :] * acc
    out = _ffn_core(h, mod_ref, g_ref[...], wi_ref, wo_ref, 6)
    if final:
        ms = jnp.mean(out * out, axis=-1, keepdims=True)
        out = out * lax.rsqrt(ms + EPS) * rest[8][...]
    o_ref[...] = out


def mix_out_ffn(h, mods, ys5, yb, yc, pscan, hf, hr, d_skip, w_glu, w_out, g, w_i, w_o,
                n_lat_tiles, n_ctx_tiles, tiles_per_type, g_final=None):
    gw = GROUP_WIDTH
    final = g_final is not None
    nt = n_lat_tiles + n_ctx_tiles
    row = lambda c: pl.BlockSpec((TOKEN_TILE, gw), lambda i: (i, c))
    vec = pl.BlockSpec((1, D_MODEL), lambda i: (0, 0))
    mixer_specs = _row_specs(n_lat_tiles, n_ctx_tiles, gw)
    in_specs = ([pl.BlockSpec((TOKEN_TILE, D_MODEL), lambda i: (i, 0)),
                 pl.BlockSpec((None, N_MOD, D_MODEL), lambda i: (i // tiles_per_type, 0, 0))]
                + mixer_specs * 5
                + [row(0), row(2),
                   pl.BlockSpec((1, gw), lambda i: (0, 0)), _resident((gw, gw)), _resident((D_MODEL, D_MODEL)),
                   vec, _resident((D_MODEL, 2 * D_FF)), _resident((D_FF, D_MODEL))])
    args = [h, mods, *ys5, *yb, *yc, *hf, *hr, pscan, pscan, d_skip.reshape(1, gw), w_glu, w_out,
            g.reshape(1, D_MODEL), w_i, w_o]
    if final:
        in_specs.append(vec)
        args.append(g_final.reshape(1, D_MODEL))
    return pl.pallas_call(
        functools.partial(_mix_ffn_kernel, n_src=len(ys5), n_lat_tiles=n_lat_tiles, final=final),
        grid=(nt,),
        in_specs=in_specs,
        out_specs=pl.BlockSpec((TOKEN_TILE, D_MODEL), lambda i: (i, 0)),
        out_shape=jax.ShapeDtypeStruct((nt * TOKEN_TILE, D_MODEL), F32),
        compiler_params=_cparams(("parallel",)),
        name="mix_out_ffn",
    )(*args)


def _s5_tables(lam_re, lam_im, log_step, b_re, b_im, c_re, c_im):
    L = S5_CHUNK
    G, P, H = S5_GROUPS, S5_STATE, S5_CH
    hp = lax.Precision.HIGHEST
    W = L * H
    lr, li = lam_re.astype(F32), lam_im.astype(F32)
    dt = jnp.exp(log_step.astype(F32))[..., None]
    mag = jnp.exp(lr * dt)
    ar, ai = mag * jnp.cos(li * dt), mag * jnp.sin(li * dt)
    den = lr * lr + li * li
    fr = ((ar - 1) * lr + ai * li) / den
    fi = (ai * lr - (ar - 1) * li) / den
    br, bi = b_re.astype(F32), b_im.astype(F32)
    bbr = jnp.swapaxes(fr[..., None] * br - fi[..., None] * bi, 2, 3)
    bbi = jnp.swapaxes(fr[..., None] * bi + fi[..., None] * br, 2, 3)
    cr, ci = c_re.astype(F32), c_im.astype(F32)
    tau = jnp.arange(L + 1, dtype=F32)[:, None, None, None]
    pmag = jnp.exp(tau * (lr * dt)[None])
    pw_r, pw_i = pmag * jnp.cos(tau * (li * dt)[None]), pmag * jnp.sin(tau * (li * dt)[None])
    abr = pw_r[:L, :, :, None, :] * bbr[None] - pw_i[:L, :, :, None, :] * bbi[None]
    abi = pw_r[:L, :, :, None, :] * bbi[None] + pw_i[:L, :, :, None, :] * bbr[None]
    kk = (jnp.einsum('dgop,ldgip->dgilo', cr, abr, precision=hp)
          - jnp.einsum('dgop,ldgip->dgilo', ci, abi, precision=hp))
    mt = _s5_toeplitz(kk[0].reshape(G, H, W), kk[1][:, :, ::-1, :].reshape(G, H, W))

    def rows(x, flip):
        x = x[::-1] if flip else x
        return jnp.transpose(x, (1, 0, 2, 3)).reshape(G, W, P)

    bs = jnp.concatenate([rows(abr[:, 0], True), rows(abi[:, 0], True),
                          rows(abr[:, 1], False), rows(abi[:, 1], False)], axis=-1).astype(BF16)
    eo = np.tile(np.eye(H, dtype=np.float32), (1, L))
    et = np.repeat(np.eye(L, dtype=np.float32), H, axis=1)
    qr = jnp.stack([pw_r[1:, 0], pw_r[:0:-1, 1]], axis=1)
    qi = jnp.stack([pw_i[1:, 0], pw_i[:0:-1, 1]], axis=1)
    crx = jnp.einsum('dgop,ox->dgpx', cr, eo, precision=hp)
    cix = jnp.einsum('dgop,ox->dgpx', ci, eo, precision=hp)
    qrx = jnp.einsum('tdgp,tx->dgpx', qr, et, precision=hp)
    qix = jnp.einsum('tdgp,tx->dgpx', qi, et, precision=hp)
    ccr = crx * qrx - cix * qix
    cci = -crx * qix - cix * qrx
    cc = jnp.stack([ccr[0], cci[0], ccr[1], cci[1]], axis=1).astype(BF16)
    al = jnp.stack([pw_r[L, 0], pw_i[L, 0], pw_r[L, 1], pw_i[L, 1]])
    return mt, bs, cc, al


def _s5_toeplitz_kernel(pf_ref, pr_ref, o_ref):
    L, H = S5_CHUNK, S5_CH
    W = L * H
    pf, pr = pf_ref[...], pr_ref[...]
    lane = lax.broadcasted_iota(jnp.int32, (H, W), 1)
    for s in range(L):
        f = pf if s == 0 else jnp.where(lane >= s * H, pltpu.roll(pf, s * H, 1), 0.0)
        back = (L - 1 - s) * H
        r = pr if back == 0 else jnp.where(lane < (s + 1) * H, pltpu.roll(pr, W - back, 1), 0.0)
        o_ref[s * H:(s + 1) * H, :] = (f + r).astype(BF16)


def _s5_toeplitz(panel_f, panel_r):
    G, H, W = panel_f.shape
    return pl.pallas_call(
        _s5_toeplitz_kernel,
        grid=(G,),
        in_specs=[pl.BlockSpec((None, H, W), lambda g: (g, 0, 0))] * 2,
        out_specs=pl.BlockSpec((None, W, W), lambda g: (g, 0, 0)),
        out_shape=jax.ShapeDtypeStruct((G, W, W), BF16),
        compiler_params=_cparams(("parallel",)),
        name="s5_toeplitz",
    )(panel_f, panel_r)


def _s5_select_tables():
    H, O = S5_CH, S5_OCT
    pack = np.zeros((O // 2, O * 128, 256), np.float32)
    unpack = np.zeros((O // 2, O * 128, 256), np.float32)
    for j in range(8):
        for g in range(O):
            for h in range(H):
                pack[g // 2, j * 128 + g * H + h, (g % 2) * 128 + j * H + h] = 1.0
                unpack[j // 2, g * 128 + j * H + h, (j % 2) * 128 + g * H + h] = 1.0
    return jnp.asarray(pack, BF16), jnp.asarray(unpack, BF16)


def _s5_kernel(ul_ref, uc_ref, pack_ref, unpack_ref, mt_ref, bs_ref, cc_ref, al_ref, ol_ref, oc_ref,
               ug_s, y_s, s_s, h_s, *, ncl, ncc):
    L, P, O = S5_CHUNK, S5_STATE, S5_OCT
    nc = ncl + ncc
    nblk = L // 8

    for tb in range(nblk):
        zs = [jnp.concatenate([ul_ref[pl.ds(8 * tb + j, ncl, stride=L), :],
                               uc_ref[pl.ds(8 * tb + j, ncc, stride=L), :]], axis=0) for j in range(8)]
        zc = jnp.concatenate(zs, axis=1).astype(BF16)
        for gp in range(O // 2):
            r = _dot(zc, pack_ref[gp])
            ug_s[2 * gp, :, tb * 128:(tb + 1) * 128] = r[:, :128]
            ug_s[2 * gp + 1, :, tb * 128:(tb + 1) * 128] = r[:, 128:]

    for g in range(O):
        ug = ug_s[g].astype(BF16)
        y_s[g] = _dot(ug, mt_ref[g])
        st = _dot(ug, bs_ref[g])
        for k in range(4):
            s_s[k, :, g, :] = st[:, k * P:(k + 1) * P]

    afr, afi, arr, ari = al_ref[0], al_ref[1], al_ref[2], al_ref[3]

    def step(j, carry):
        hfr, hfi, hrr, hri = carry
        cf = jnp.where(j < ncc, ncl + j, j - ncc)
        cr = nc - 1 - j
        h_s[0, cf] = hfr
        h_s[1, cf] = hfi
        h_s[2, cr] = hrr
        h_s[3, cr] = hri
        return (afr * hfr - afi * hfi + s_s[0, cf], afr * hfi + afi * hfr + s_s[1, cf],
                arr * hrr - ari * hri + s_s[2, cr], arr * hri + ari * hrr + s_s[3, cr])

    zero = jnp.zeros((O, P), F32)
    lax.fori_loop(0, nc, step, (zero, zero, zero, zero))

    for g in range(O):
        y = y_s[g]
        for k in range(4):
            y = y + _dot(h_s[k, :, g, :].astype(BF16), cc_ref[g, k])
        y_s[g] = y

    for tb in range(nblk):
        yc = jnp.concatenate([y_s[g, :, tb * 128:(tb + 1) * 128] for g in range(O)], axis=1)
        hi = yc.astype(BF16)
        lo = (yc - hi.astype(F32)).astype(BF16)
        for jp in range(4):
            z = _dot(hi, unpack_ref[jp]) + _dot(lo, unpack_ref[jp])
            for e in range(2):
                t = 8 * tb + 2 * jp + e
                zt = z[:, e * 128:(e + 1) * 128]
                ol_ref[pl.ds(t, ncl, stride=L), :] = zt[:ncl]
                oc_ref[pl.ds(t, ncc, stride=L), :] = zt[ncl:]


def s5_scan(pscan, tables, n_batch, t_lat, t_ctx):
    mt, bs, cc, al = tables
    pack, unpack = _s5_select_tables()
    L, P, O = S5_CHUNK, S5_STATE, S5_OCT
    W = L * S5_CH
    ncl, ncc = t_lat // L, t_ctx // L
    nc = ncl + ncc
    ctx0 = n_batch * t_lat // t_ctx
    once = pl.Buffered(1)
    return pl.pallas_call(
        functools.partial(_s5_kernel, ncl=ncl, ncc=ncc),
        grid=(S5_GROUPS // O, n_batch),
        in_specs=[
            pl.BlockSpec((t_lat, O * S5_CH), lambda o, b: (b, o)),
            pl.BlockSpec((t_ctx, O * S5_CH), lambda o, b: (ctx0 + b, o)),
            pl.BlockSpec(pack.shape, lambda o, b: (0, 0, 0), pipeline_mode=once),
            pl.BlockSpec(unpack.shape, lambda o, b: (0, 0, 0), pipeline_mode=once),
            pl.BlockSpec((O, W, W), lambda o, b: (o, 0, 0), pipeline_mode=once),
            pl.BlockSpec((O, W, 4 * P), lambda o, b: (o, 0, 0), pipeline_mode=once),
            pl.BlockSpec((O, 4, P, W), lambda o, b: (o, 0, 0, 0), pipeline_mode=once),
            pl.BlockSpec((4, O, P), lambda o, b: (0, o, 0)),
        ],
        out_specs=[
            pl.BlockSpec((t_lat, O * S5_CH), lambda o, b: (b, o)),
            pl.BlockSpec((t_ctx, O * S5_CH), lambda o, b: (b, o)),
        ],
        out_shape=[
            jax.ShapeDtypeStruct((n_batch * t_lat, GROUP_WIDTH), F32),
            jax.ShapeDtypeStruct((n_batch * t_ctx, GROUP_WIDTH), F32),
        ],
        scratch_shapes=[
            pltpu.VMEM((O, nc, W), F32),
            pltpu.VMEM((O, nc, W), F32),
            pltpu.VMEM((4, nc, O, P), F32),
            pltpu.VMEM((4, nc, O, P), F32),
        ],
        compiler_params=_cparams(("arbitrary", "arbitrary")),
        name="s5_scan",
    )(pscan, pscan, pack, unpack, mt, bs, cc, al)


def _lru_tile(x_ref, o_ref, row0, n_rows, carry, dr, w, xs_s, a_s, b_s):
    cw_ref, cb_ref, wa_ref, ba_ref, wx_ref, bx_ref, lam_ref = w
    tt = SEQ_TILE
    reverse = dr == 1
    row0 = pl.multiple_of(row0, tt)
    has_prev = (row0 > 0).astype(F32)
    has_next = (row0 + tt < n_rows).astype(F32)
    prev0 = pl.multiple_of(jnp.maximum(row0 - 8, 0), 8)
    next0 = pl.multiple_of(jnp.minimum(row0 + tt, n_rows - 8), 8)
    xs_s[dr, 0:8, :] = x_ref[pl.ds(prev0, 8), :] * has_prev
    xs_s[dr, 8:8 + tt, :] = x_ref[pl.ds(row0, tt), :]
    xs_s[dr, 8 + tt:16 + tt, :] = x_ref[pl.ds(next0, 8), :] * has_next
    xc = cb_ref[...] + jnp.zeros((tt, 128), F32)
    for tap in range(LRU_CONV):
        xc = xc + cw_ref[tap:tap + 1, :] * xs_s[dr, pl.ds(6 + tap, tt), :]
    xb = xc.astype(BF16)
    r = jax.nn.sigmoid(_dot(xb, wa_ref[dr]) + ba_ref[dr])
    i = jax.nn.sigmoid(_dot(xb, wx_ref[dr]) + bx_ref[dr])
    z = -lam_ref[dr]
    softplus = jnp.maximum(z, 0.0) + jnp.log1p(jnp.exp(-jnp.abs(z)))
    log_a = -LRU_C * r * softplus
    a_all = jnp.exp(log_a)
    th = jnp.tanh(log_a)
    b_all = jnp.sqrt(-2.0 * th / (1.0 - th)) * (i * xc)

    nseg = 8
    slen = tt // nseg
    pitch = LRU_SEG_PITCH
    for sgm in range(nseg):
        a_s[dr, sgm * pitch:sgm * pitch + slen, :] = a_all[sgm * slen:(sgm + 1) * slen, :]
        b_s[dr, sgm * pitch:sgm * pitch + slen, :] = b_all[sgm * slen:(sgm + 1) * slen, :]
    h = jnp.zeros((nseg, 128), F32)
    p = jnp.ones((nseg, 128), F32)
    for i in (range(slen - 1, -1, -1) if reverse else range(slen)):
        pos = pl.ds(i, nseg, stride=pitch)
        a = a_s[dr, pos, :]
        h = a * h + b_s[dr, pos, :]
        p = a * p
        b_s[dr, pos, :] = h
        a_s[dr, pos, :] = p
    e = carry
    ins = [None] * nseg
    for sgm in (range(nseg - 1, -1, -1) if reverse else range(nseg)):
        ins[sgm] = e
        e = h[sgm:sgm + 1, :] + p[sgm:sgm + 1, :] * e
    e_in = jnp.concatenate(ins, axis=0)
    for i in range(slen):
        pos = pl.ds(i, nseg, stride=pitch)
        b_s[dr, pos, :] = b_s[dr, pos, :] + a_s[dr, pos, :] * e_in
    for sgm in range(nseg):
        o_ref[pl.ds(row0 + sgm * slen, slen), :] = b_s[dr, sgm * pitch:sgm * pitch + slen, :]
    return e


def _lru_kernel(xl_ref, xc_ref, cw_ref, cb_ref, wa_ref, ba_ref, wx_ref, bx_ref, lam_ref,
                hfl_ref, hfc_ref, hrl_ref, hrc_ref, xs_s, a_s, b_s, *, ntl, ntc):
    w = (cw_ref, cb_ref, wa_ref, ba_ref, wx_ref, bx_ref, lam_ref)
    tt = SEQ_TILE

    def run(x_ref, of_ref, or_ref, nt, carry):
        def body(j, c):
            cf = _lru_tile(x_ref, of_ref, j * tt, nt * tt, c[0], 0, w, xs_s, a_s, b_s)
            cr = _lru_tile(x_ref, or_ref, (nt - 1 - j) * tt, nt * tt, c[1], 1, w, xs_s, a_s, b_s)
            return cf, cr
        return lax.fori_loop(0, nt, body, carry)

    zero = jnp.zeros((1, 128), F32)
    carry = run(xc_ref, hfc_ref, hrc_ref, ntc, (zero, zero))
    run(xl_ref, hfl_ref, hrl_ref, ntl, carry)


def lru_scan(pscan, conv_w, conv_b, w_a, b_a, w_x, b_x, lam, n_batch, t_lat, t_ctx):
    tt = SEQ_TILE
    gw = GROUP_WIDTH
    nh = gw // 128
    ntl, ntc = t_lat // tt, t_ctx // tt
    ctx0 = n_batch * t_lat // t_ctx
    per_half = LRU_BLOCKS // nh

    def halves(w):
        return jnp.stack([jnp.stack([_block_diag(w[dr, k * per_half:(k + 1) * per_half]) for k in range(nh)])
                          for dr in range(2)]).astype(BF16)

    vec = lambda a: a.reshape(2, 1, gw)
    vspec = pl.BlockSpec((2, 1, 128), lambda b, k: (0, 0, k))
    mspec = pl.BlockSpec((2, None, 128, 128), lambda b, k: (0, k, 0, 0))
    lat = pl.BlockSpec((t_lat, 128), lambda b, k: (b, k))
    ctx = pl.BlockSpec((t_ctx, 128), lambda b, k: (b, k))
    lat_shape = jax.ShapeDtypeStruct((n_batch * t_lat, gw), F32)
    ctx_shape = jax.ShapeDtypeStruct((n_batch * t_ctx, gw), F32)
    hfl, hfc, hrl, hrc = pl.pallas_call(
        functools.partial(_lru_kernel, ntl=ntl, ntc=ntc),
        grid=(n_batch, nh),
        in_specs=[
            pl.BlockSpec((t_lat, 128), lambda b, k: (b, nh + k)),
            pl.BlockSpec((t_ctx, 128), lambda b, k: (ctx0 + b, nh + k)),
            pl.BlockSpec((LRU_CONV, 128), lambda b, k: (0, k)),
            pl.BlockSpec((1, 128), lambda b, k: (0, k)),
            mspec, vspec, mspec, vspec, vspec,
        ],
        out_specs=[lat, ctx, lat, ctx],
        out_shape=[lat_shape, ctx_shape, lat_shape, ctx_shape],
        scratch_shapes=[
            pltpu.VMEM((2, tt + 16, 128), F32),
            pltpu.VMEM((2, 8 * LRU_SEG_PITCH, 128), F32),
            pltpu.VMEM((2, 8 * LRU_SEG_PITCH, 128), F32),
        ],
        compiler_params=_cparams(("parallel", "parallel")),
        name="lru_scan",
    )(pscan, pscan, conv_w, conv_b.reshape(1, gw), halves(w_a), vec(b_a), halves(w_x), vec(b_x), vec(lam))
    return [hfl, hfc], [hrl, hrc]


def _block_diag(w):
    n, d, e = w.shape
    eye = jnp.eye(n, dtype=w.dtype)
    return (eye[:, None, :, None] * w[:, :, None, :]).reshape(n * d, n * e)


def _natten_bias(rpb):
    W, KR, KC, QR, NB = GRID_W, NA_KR, NA_KC, NA_QROWS, NA_BAND
    col = np.arange(W)
    cs = np.clip(col - KC // 2, 0, W - KC)
    inwin = (col[None, :] >= cs[:, None]) & (col[None, :] < cs[:, None] + KC)
    coff = np.clip(col[None, :] - col[:, None] + (KC - 1), 0, 2 * KC - 2)
    a = np.arange(QR)[:, None]
    i = np.arange(NB)[None, :]
    first = (i - a, (i < KR) & (a >= 0))
    mid = (i - a - KR // 2, (i >= a) & (i < a + KR))
    last = (i - a - NB + QR, (i >= NB - KR) & (a >= 0))
    coh = (coff[:, :, None] == np.arange(2 * KC - 1)).astype(np.float32)
    t = jnp.einsum('hrc,qkc->hrqk', rpb.astype(F32), coh, precision=lax.Precision.HIGHEST)
    t = t * LOG2E + np.where(inwin, 0.0, NEG_BIG).astype(np.float32)
    plan = tuple(
        tuple(tuple(int(np.clip(delta[ai, ii] + KR - 1, 0, 2 * KR - 2)) if valid[ai, ii] else -1
                    for ii in range(NB)) for ai in range(QR))
        for delta, valid in (first, mid, last))

    def expand(t_ref, o_ref):
        for v in range(3):
            for ai in range(QR):
                for ii in range(NB):
                    r = plan[v][ai][ii]
                    blk = t_ref[r] if r >= 0 else jnp.full((W, W), NEG_BIG, F32)
                    o_ref[v, ai * W:(ai + 1) * W, ii * W:(ii + 1) * W] = blk

    nh = rpb.shape[0]
    return pl.pallas_call(
        expand,
        grid=(nh,),
        in_specs=[pl.BlockSpec((None, 2 * KR - 1, W, W), lambda h: (h, 0, 0, 0))],
        out_specs=pl.BlockSpec((3, None, QR * W, NB * W), lambda h: (0, h, 0, 0)),
        out_shape=jax.ShapeDtypeStruct((3, nh, QR * W, NB * W), F32),
        compiler_params=_cparams(("parallel",)),
        name="natten_bias",
    )(t)


def _natten_kernel(q_ref, k_ref, v_ref, kc_ref, vc_ref, bias_ref, o_ref, *, rows):
    W, KR, hd = GRID_W, NA_KR, HEAD_DIM
    blk = pl.program_id(1)
    nblk = rows // NA_QROWS
    bs = jnp.clip(blk * NA_QROWS - KR // 2, 0, rows - NA_BAND)
    var = jnp.where(blk == 0, 0, jnp.where(blk == nblk - 1, 2, 1))
    start = pl.multiple_of(bs * W, W)
    nb = NA_BAND * W
    q = (q_ref[...].astype(F32) * ((hd ** -0.5) * LOG2E)).astype(BF16)
    k_all = jnp.concatenate([k_ref[pl.ds(start, nb), :], kc_ref[...]], axis=0)
    v_all = jnp.concatenate([v_ref[pl.ds(start, nb), :], vc_ref[...]], axis=0)
    ones_col = (lax.broadcasted_iota(jnp.int32, (v_all.shape[0], hd), 1) == 0).astype(BF16)
    for h in range(NA_HEADS):
        sl = slice(h * hd, (h + 1) * hd)
        s = _dot_nt(q[:, sl], k_all[:, sl])
        sb = s[:, :nb] + bias_ref[var, h]
        sc = s[:, nb:]
        m = jnp.maximum(jnp.max(sb, axis=-1, keepdims=True), jnp.max(sc, axis=-1, keepdims=True))
        p = jnp.concatenate([jnp.exp2(sb - m), jnp.exp2(sc - m)], axis=1).astype(BF16)
        o = _dot(p, jnp.concatenate([v_all[:, sl], ones_col], axis=1))
        o_ref[:, sl] = o[:, :hd] / o[:, hd:hd + 1]


def natten_latent(pattn, bias, n_batch, t_lat, t_ctx):
    gw = GROUP_WIDTH
    rows = t_lat // GRID_W
    assert NA_QROWS == NA_KR // 2 and NA_BAND == NA_KR + NA_QROWS and rows % NA_QROWS == 0 and rows >= NA_BAND
    nblk = rows // NA_QROWS
    tq = NA_QROWS * GRID_W
    ctx0 = n_batch * t_lat // t_ctx
    return pl.pallas_call(
        functools.partial(_natten_kernel, rows=rows),
        grid=(n_batch, nblk),
        in_specs=[
            pl.BlockSpec((tq, gw), lambda b, r: (b * nblk + r, 2)),
            pl.BlockSpec((t_lat, gw), lambda b, r: (b, 3)),
            pl.BlockSpec((t_lat, gw), lambda b, r: (b, 4)),
            pl.BlockSpec((t_ctx, gw), lambda b, r: (ctx0 + b, 3)),
            pl.BlockSpec((t_ctx, gw), lambda b, r: (ctx0 + b, 4)),
            pl.BlockSpec(bias.shape, lambda b, r: (0, 0, 0, 0)),
        ],
        out_specs=pl.BlockSpec((tq, gw), lambda b, r: (b * nblk + r, 0)),
        out_shape=jax.ShapeDtypeStruct((n_batch * t_lat, gw), F32),
        compiler_params=_cparams(("parallel", "arbitrary")),
        name="natten_latent",
    )(pattn, pattn, pattn, pattn, pattn, bias)


def _ctx_attn_kernel(q_ref, k_ref, v_ref, o_ref):
    hd = HEAD_DIM
    q, k, v = q_ref[...], k_ref[...], v_ref[...]
    for h in range(NA_HEADS):
        sl = slice(h * hd, (h + 1) * hd)
        s = _dot_nt(q[:, sl], k[:, sl]) * (hd ** -0.5)
        m = jnp.max(s, axis=-1, keepdims=True)
        p = jnp.exp(s - m)
        l = jnp.sum(p, axis=-1, keepdims=True)
        o_ref[:, sl] = _dot(p.astype(BF16), v[:, sl]) / l


def natten_context(pattn, n_batch, t_lat, t_ctx):
    gw = GROUP_WIDTH
    ctx0 = n_batch * t_lat // t_ctx
    return pl.pallas_call(
        _ctx_attn_kernel,
        grid=(n_batch,),
        in_specs=[pl.BlockSpec((t_ctx, gw), lambda b, c=c: (ctx0 + b, c)) for c in (2, 3, 4)],
        out_specs=pl.BlockSpec((t_ctx, gw), lambda b: (b, 0)),
        out_shape=jax.ShapeDtypeStruct((n_batch * t_ctx, gw), F32),
        compiler_params=_cparams(("parallel",)),
        name="natten_context",
    )(pattn, pattn, pattn)


def _rope_tables(t_lat, t_ctx):
    half = HEAD_DIM // 4
    freqs = ROPE_BASE ** (-jnp.arange(half, dtype=F32) / half)
    t = jnp.arange(t_lat)
    ang_r = (t // GRID_W).astype(F32)[:, None] * freqs[None, :]
    ang_c = (t % GRID_W).astype(F32)[:, None] * freqs[None, :]
    ang = jnp.concatenate([ang_r, ang_r, ang_c, ang_c], axis=1)
    cos = jnp.concatenate([jnp.cos(ang), jnp.ones((t_ctx, HEAD_DIM), F32)], axis=0)
    sin = jnp.concatenate([jnp.sin(ang), jnp.zeros((t_ctx, HEAD_DIM), F32)], axis=0)
    return jnp.tile(cos, (1, GQA_Q_HEADS)), jnp.tile(sin, (1, GQA_Q_HEADS))


def _gqa_prep_kernel(x_ref, cos_ref, sin_ref, gq_ref, gk_ref, gm_ref, q_ref, k_ref, v_ref):
    hd = HEAD_DIM
    nq, nk = GQA_Q_HEADS * hd, GQA_KV_HEADS * hd
    x = x_ref[...].astype(F32)
    cos, sin = cos_ref[...], sin_ref[...]
    gm = gm_ref[...]

    def norm_rope(t, gain, width):
        t2 = t * t
        hi = t2.astype(BF16)
        lo = (t2 - hi.astype(F32)).astype(BF16)
        ms = _dot(hi, gm[:width, :width]) + _dot(lo, gm[:width, :width])
        tn = t * lax.rsqrt(ms + EPS) * gain
        lane = lax.broadcasted_iota(jnp.int32, tn.shape, 1)
        first = (lane % (hd // 2)) < (hd // 4)
        rot = jnp.where(first, -pltpu.roll(tn, width - hd // 4, 1), pltpu.roll(tn, hd // 4, 1))
        return tn * cos[:, :width] + rot * sin[:, :width]

    q = norm_rope(x[:, :nq], gq_ref[...], nq) * ((hd ** -0.5) * LOG2E)
    kt = norm_rope(x[:, nq:nq + nk], gk_ref[...], nk).T
    v = x_ref[:, nq + nk:nq + 2 * nk]
    for h in range(GQA_Q_HEADS):
        q_ref[h] = q[:, h * hd:(h + 1) * hd].astype(BF16)
    ones_col = (lax.broadcasted_iota(jnp.int32, (v.shape[0], hd), 1) == 0).astype(BF16)
    for h in range(GQA_KV_HEADS):
        k_ref[h] = kt[h * hd:(h + 1) * hd, :].astype(BF16)
        v_ref[h] = jnp.concatenate([v[:, h * hd:(h + 1) * hd], ones_col], axis=1)


def gqa_prep(pattn, gq, gk, n_batch, t_lat, t_ctx):
    tt = SEQ_TILE
    assert tt == KEY_CHUNK
    hd = HEAD_DIM
    ntl, ntc = t_lat // tt, t_ctx // tt
    s_len = t_lat + t_ctx
    cos, sin = _rope_tables(t_lat, t_ctx)
    nq = GQA_Q_HEADS * hd
    gmat = _block_diag(jnp.full((GQA_Q_HEADS, hd, hd), 1.0 / hd, F32)).astype(BF16)

    def pos(i):
        is_ctx = i >= n_batch * ntl
        c = i - n_batch * ntl
        return jnp.where(is_ctx, ntl + c % ntc, i % ntl)

    def bat(i):
        is_ctx = i >= n_batch * ntl
        return jnp.where(is_ctx, (i - n_batch * ntl) // ntc, i // ntl)

    return pl.pallas_call(
        _gqa_prep_kernel,
        grid=(n_batch * (ntl + ntc),),
        in_specs=[
            pl.BlockSpec((tt, 2 * nq), lambda i: (i, 0)),
            pl.BlockSpec((tt, nq), lambda i: (pos(i), 0)),
            pl.BlockSpec((tt, nq), lambda i: (pos(i), 0)),
            pl.BlockSpec((1, nq), lambda i: (0, 0)),
            pl.BlockSpec((1, nq // 2), lambda i: (0, 0)),
            pl.BlockSpec((nq, nq), lambda i: (0, 0)),
        ],
        out_specs=[
            pl.BlockSpec((None, GQA_Q_HEADS, tt, hd), lambda i: (bat(i), 0, pos(i), 0)),
            pl.BlockSpec((None, GQA_KV_HEADS, None, hd, tt), lambda i: (bat(i), 0, pos(i), 0, 0)),
            pl.BlockSpec((None, GQA_KV_HEADS, tt, 2 * hd), lambda i: (bat(i), 0, pos(i), 0)),
        ],
        out_shape=[
            jax.ShapeDtypeStruct((n_batch, GQA_Q_HEADS, s_len, hd), BF16),
            jax.ShapeDtypeStruct((n_batch, GQA_KV_HEADS, s_len // tt, hd, tt), BF16),
            jax.ShapeDtypeStruct((n_batch, GQA_KV_HEADS, s_len, 2 * hd), BF16),
        ],
        compiler_params=_cparams(("parallel",)),
        name="gqa_prep",
    )(pattn, cos, sin, jnp.tile(gq.astype(F32), GQA_Q_HEADS).reshape(1, nq),
      jnp.tile(gk.astype(F32), GQA_KV_HEADS).reshape(1, nq // 2), gmat)


def _gqa_kernel(q_ref, kt_ref, v_ref, o_ref, s_s, m_s, acc_s, *, unroll):
    g, tq, hd = q_ref.shape
    nch, _, ck = kt_ref.shape
    nl = ck // 128
    q = q_ref[...].reshape(g * tq, hd)
    m_s[...] = jnp.full_like(m_s, -jnp.inf)

    def scores(c, carry):
        s = _dot(q, kt_ref[c])
        s_s[c] = s
        fold = s[:, 0:128]
        for j in range(1, nl):
            fold = jnp.maximum(fold, s[:, j * 128:(j + 1) * 128])
        m_s[...] = jnp.maximum(m_s[...], fold)
        return carry

    lax.fori_loop(0, nch, scores, 0, unroll=unroll)
    m_s[...] = jnp.broadcast_to(jnp.max(m_s[...], axis=-1, keepdims=True), m_s.shape)
    acc_s[...] = jnp.zeros_like(acc_s)

    def values(grp, carry):
        m = m_s[...]
        ps = []
        for u in range(unroll):
            s = s_s[grp * unroll + u]
            ps += [jnp.exp2(s[:, j * 128:(j + 1) * 128] - m).astype(BF16) for j in range(nl)]
        off = pl.multiple_of(grp * (unroll * ck), unroll * ck)
        acc_s[...] += _dot(jnp.concatenate(ps, axis=1), v_ref[pl.ds(off, unroll * ck), :])
        return carry

    lax.fori_loop(0, nch // unroll, values, 0)
    acc = acc_s[...]
    o = acc[:, :hd] / acc[:, hd:hd + 1]
    o_ref[...] = jnp.concatenate([o[i * tq:(i + 1) * tq] for i in range(g)], axis=1)


def gqa_attention(q, kt, v, tq, q_blk0, nq, c_blk, nch):
    n_batch = q.shape[0]
    hd = HEAD_DIM
    g = GQA_GROUP
    ck = KEY_CHUNK
    unroll = next(u for u in (11, 3, 1) if nch % u == 0)
    return pl.pallas_call(
        functools.partial(_gqa_kernel, unroll=unroll),
        grid=(n_batch, GQA_KV_HEADS, nq),
        in_specs=[
            pl.BlockSpec((None, g, tq, hd), lambda b, h, i: (b, h, q_blk0 + i, 0)),
            pl.BlockSpec((None, None, nch, hd, ck), lambda b, h, i: (b, h, c_blk, 0, 0)),
            pl.BlockSpec((None, None, nch * ck, 2 * hd), lambda b, h, i: (b, h, c_blk, 0)),
        ],
        out_specs=pl.BlockSpec((tq, g * hd), lambda b, h, i: (b * nq + i, h)),
        out_shape=jax.ShapeDtypeStruct((n_batch * nq * tq, GQA_Q_HEADS * hd), F32),
        scratch_shapes=[
            pltpu.VMEM((nch, g * tq, ck), F32),
            pltpu.VMEM((g * tq, 128), F32),
            pltpu.VMEM((g * tq, 2 * hd), F32),
        ],
        compiler_params=_cparams(("parallel", "parallel", "arbitrary")),
        name="gqa_attention",
    )(q, kt, v)


def kernel(x, c, ctx, c_ctx, w_ada, b_ada, g_ffn1, w_ffn1_in, w_ffn1_out, g_mix, w_in, w_out, s5_lambda_re, s5_lambda_im, s5_log_step, s5_b_re, s5_b_im, s5_c_re, s5_c_im, s5_d, s5_w_glu, na_rpb, gqa_q_norm, gqa_k_norm, lru_conv_w, lru_conv_b, lru_w_a, lru_b_a, lru_w_x, lru_b_x, lru_lambda, g_ffn2, w_ffn2_in, w_ffn2_out, g_final):
    n_batch, t_lat, d = x.shape
    t_ctx = ctx.shape[1]
    depth = w_ada.shape[0]
    assert d == D_MODEL and t_lat % TOKEN_TILE == 0 and (n_batch * t_ctx) % TOKEN_TILE == 0
    assert t_lat % t_ctx == 0 and t_ctx % SEQ_TILE == 0 and t_lat % FLASH_TQ == 0
    n_lat = n_batch * t_lat
    n_all = n_lat + n_batch * t_ctx
    tiles_per_type = t_lat // TOKEN_TILE
    assert (n_batch * t_ctx) // TOKEN_TILE <= tiles_per_type

    c8 = jnp.zeros((8, D_MODEL), F32).at[:n_batch].set(c.astype(F32)).at[n_batch].set(c_ctx.astype(F32))
    mods_all = ada_mods(c8, w_ada, b_ada).reshape(depth, 8, N_MOD, D_MODEL)

    s_len = t_lat + t_ctx
    n_lat_tiles = n_lat // TOKEN_TILE
    n_ctx_tiles = (n_all - n_lat) // TOKEN_TILE
    hs = [x.reshape(n_lat, D_MODEL), ctx.reshape(n_batch * t_ctx, D_MODEL)]

    for l in range(depth):
        need_ctx = l < depth - 1
        mods = mods_all[l]
        w1i, w1o = w_ffn1_in[l].astype(BF16), w_ffn1_out[l].astype(BF16)
        w2i, w2o = w_ffn2_in[l].astype(BF16), w_ffn2_out[l].astype(BF16)
        w_in_l = w_in[l].astype(BF16)

        h, pscan, pattn = ffn_in_projection(hs, mods, g_ffn1[l], w1i, w1o, g_mix[l], w_in_l,
                                            n_lat_tiles, n_ctx_tiles, tiles_per_type)

        tables = _s5_tables(s5_lambda_re[l], s5_lambda_im[l], s5_log_step[l], s5_b_re[l], s5_b_im[l],
                            s5_c_re[l], s5_c_im[l])
        ys5 = list(s5_scan(pscan, tables, n_batch, t_lat, t_ctx))

        hf, hr = lru_scan(pscan, lru_conv_w[l], lru_conv_b[l], lru_w_a[l], lru_b_a[l], lru_w_x[l], lru_b_x[l],
                          lru_lambda[l], n_batch, t_lat, t_ctx)

        yb = [natten_latent(pattn, _natten_bias(na_rpb[l]), n_batch, t_lat, t_ctx)]

        qh, kth, vh = gqa_prep(pattn, gqa_q_norm[l], gqa_k_norm[l], n_batch, t_lat, t_ctx)
        yc = [gqa_attention(qh, kth, vh, FLASH_TQ, 0, t_lat // FLASH_TQ, 0, s_len // KEY_CHUNK)]

        if need_ctx:
            yb.append(natten_context(pattn, n_batch, t_lat, t_ctx))
            yc.append(gqa_attention(qh, kth, vh, t_ctx, t_lat // t_ctx, 1, t_lat // t_ctx, t_ctx // KEY_CHUNK))
        else:
            ys5, hf, hr = ys5[:1], hf[:1], hr[:1]

        h = mix_out_ffn(h, mods, ys5, yb, yc, pscan, hf, hr, s5_d[l], s5_w_glu[l].astype(BF16),
                        w_out[l].astype(BF16), g_ffn2[l], w2i, w2o, n_lat_tiles,
                        n_ctx_tiles if need_ctx else 0, tiles_per_type, g_final=None if need_ctx else g_final)
        hs = [h]
    return h.reshape(n_batch, t_lat, D_MODEL)
```

```python
import functools
import math

import numpy as np
import jax
import jax.numpy as jnp
from jax import lax
from jax.experimental import pallas as pl
from jax.experimental.pallas import tpu as pltpu

F32 = jnp.float32
BF16 = jnp.bfloat16

D_MODEL = 1024
GRID_W = 64
HEAD_DIM = 64
GROUP_WIDTH = D_MODEL // 4
S5_CH = 16
S5_GROUPS = GROUP_WIDTH // S5_CH
S5_STATE = 64
NA_HEADS = GROUP_WIDTH // HEAD_DIM
NA_KR = 8
NA_KC = 16
GQA_Q_HEADS = GROUP_WIDTH // HEAD_DIM
GQA_KV_HEADS = GQA_Q_HEADS // 2
GQA_GROUP = GQA_Q_HEADS // GQA_KV_HEADS
LRU_BLOCKS = GROUP_WIDTH // HEAD_DIM
LRU_CONV = 4
LRU_C = 8.0
D_FF = ((8 * D_MODEL // 3 + 127) // 128) * 128
FFN_RES = 0.5
ROPE_BASE = 10000.0
EPS = 1e-6
N_MOD = 9

TOKEN_TILE = 512
SEQ_TILE = 256
LRU_SEG_PITCH = SEQ_TILE // 8 + 4
S5_CHUNK = 32
S5_OCT = 8
MXU_DEPTH = 256
FF_CHUNKS = ((0, 6 * MXU_DEPTH), (6 * MXU_DEPTH, D_FF))
FLASH_TQ = 512
KEY_CHUNK = 256
NA_QROWS = NA_KR // 2
NA_BAND = NA_KR + NA_QROWS
NA_STEP_BLOCKS = 2
LOG2E = math.log2(math.e)
NEG_BIG = -1e30
VMEM_LIMIT = 56 * 1024 * 1024


def _cparams(sem):
    return pltpu.CompilerParams(dimension_semantics=sem, vmem_limit_bytes=VMEM_LIMIT)


def _dot(a, b):
    return jnp.dot(a, b, preferred_element_type=F32)


def _dot_nt(a, b):
    return lax.dot_general(a, b, (((1,), (1,)), ((), ())), preferred_element_type=F32)


def _ada_kernel(c_ref, w_ref, b_ref, o_ref):
    c = c_ref[...]
    s = c * jax.nn.sigmoid(c)
    o_ref[...] = _dot(s.astype(BF16), w_ref[...].astype(BF16)) + b_ref[...]


def ada_mods(c8, w_ada, b_ada):
    L = w_ada.shape[0]
    tn = D_MODEL
    return pl.pallas_call(
        _ada_kernel,
        grid=(L, N_MOD * D_MODEL // tn),
        in_specs=[
            pl.BlockSpec((8, D_MODEL), lambda l, j: (0, 0)),
            pl.BlockSpec((None, D_MODEL, tn), lambda l, j: (l, 0, j)),
            pl.BlockSpec((None, 1, tn), lambda l, j: (l, 0, j)),
        ],
        out_specs=pl.BlockSpec((None, 8, tn), lambda l, j: (l, 0, j)),
        out_shape=jax.ShapeDtypeStruct((L, 8, N_MOD * D_MODEL), F32),
        compiler_params=_cparams(("parallel", "parallel")),
        name="ada_mods",
    )(c8, w_ada, b_ada.reshape(L, 1, N_MOD * D_MODEL))


def _norm_mod(x, g, shift, scale):
    ms = jnp.mean(x * x, axis=-1, keepdims=True)
    y = x * lax.rsqrt(ms + EPS) * g
    return y * (1.0 + scale) + shift


def _ffn_core(x, mod_ref, g, wi_ref, wo_ref, k0):
    y = _norm_mod(x, g, mod_ref[k0:k0 + 1, :], mod_ref[k0 + 1:k0 + 2, :]).astype(BF16)
    acc = jnp.zeros(x.shape, F32)
    for lo, hi in FF_CHUNKS:
        a = _dot(y, wi_ref[:, lo:hi])
        b = _dot(y, wi_ref[:, D_FF + lo:D_FF + hi])
        h = (a * jax.nn.sigmoid(a) * b).astype(BF16)
        acc = acc + _dot(h, wo_ref[lo:hi, :])
    return x + FFN_RES * mod_ref[k0 + 2:k0 + 3, :] * acc


def _resident(shape):
    return pl.BlockSpec(shape, lambda i: (0,) * len(shape), pipeline_mode=pl.Buffered(1))


def _row_specs(n_lat_tiles, n_ctx_tiles, width, col=0):
    lat = pl.BlockSpec((TOKEN_TILE, width), lambda i: (jnp.minimum(i, n_lat_tiles - 1), col))
    if n_ctx_tiles == 0:
        return [lat]
    ctx = pl.BlockSpec((TOKEN_TILE, width), lambda i: (jnp.clip(i - n_lat_tiles, 0, n_ctx_tiles - 1), col))
    return [lat, ctx]


def _pick(refs, n_lat_tiles):
    if len(refs) == 1:
        return refs[0][...]
    return jnp.where(pl.program_id(0) >= n_lat_tiles, refs[1][...], refs[0][...])


def _ffn_in_kernel(*refs, n_src, n_lat_tiles):
    x_refs = refs[:n_src]
    mod_ref, g_ref, wi_ref, wo_ref, gm_ref, win_ref, h_ref, ps_ref, pa_ref = refs[n_src:]
    h = _ffn_core(_pick(x_refs, n_lat_tiles), mod_ref, g_ref[...], wi_ref, wo_ref, 0)
    h_ref[...] = h
    y = _norm_mod(h, gm_ref[...], mod_ref[3:4, :], mod_ref[4:5, :]).astype(BF16)
    p = _dot(y, win_ref[...])
    gw = GROUP_WIDTH
    ps_ref[...] = jnp.concatenate([p[:, 0:gw], p[:, 6 * gw:8 * gw]], axis=1)
    pa_ref[...] = jnp.concatenate([p[:, 4 * gw:6 * gw], p[:, gw:4 * gw]], axis=1).astype(BF16)


def ffn_in_projection(xs, mods, g, w_i, w_o, g_mix, w_in_l, n_lat_tiles, n_ctx_tiles, tiles_per_type):
    nt = n_lat_tiles + n_ctx_tiles
    n = nt * TOKEN_TILE
    n_scan = 3 * GROUP_WIDTH
    n_attn = w_in_l.shape[1] - n_scan
    x_specs = (_row_specs(n_lat_tiles, n_ctx_tiles, D_MODEL) if len(xs) == 2
               else [pl.BlockSpec((TOKEN_TILE, D_MODEL), lambda i: (i, 0))])
    vec = pl.BlockSpec((1, D_MODEL), lambda i: (0, 0))
    return pl.pallas_call(
        functools.partial(_ffn_in_kernel, n_src=len(xs), n_lat_tiles=n_lat_tiles),
        grid=(nt,),
        in_specs=x_specs + [
            pl.BlockSpec((None, N_MOD, D_MODEL), lambda i: (i // tiles_per_type, 0, 0)),
            vec, _resident((D_MODEL, 2 * D_FF)), _resident((D_FF, D_MODEL)),
            vec, _resident((D_MODEL, n_scan + n_attn)),
        ],
        out_specs=[
            pl.BlockSpec((TOKEN_TILE, D_MODEL), lambda i: (i, 0)),
            pl.BlockSpec((TOKEN_TILE, n_scan), lambda i: (i, 0)),
            pl.BlockSpec((TOKEN_TILE, n_attn), lambda i: (i, 0)),
        ],
        out_shape=[
            jax.ShapeDtypeStruct((n, D_MODEL), F32),
            jax.ShapeDtypeStruct((n, n_scan), F32),
            jax.ShapeDtypeStruct((n, n_attn), BF16),
        ],
        compiler_params=_cparams(("parallel",)),
        name="ffn_in_projection",
    )(*xs, mods, g.reshape(1, D_MODEL), w_i, w_o, g_mix.reshape(1, D_MODEL), w_in_l)


def _mix_ffn_kernel(*refs, n_src, n_lat_tiles, final):
    gw = GROUP_WIDTH
    h_ref, mod_ref = refs[:2]
    ys5_refs, yb_refs, yc_refs, hf_refs, hr_refs = (refs[2 + k * n_src:2 + (k + 1) * n_src] for k in range(5))
    rest = refs[2 + 5 * n_src:]
    u_ref, gl_ref, dsk_ref, wglu_ref, wout_ref, g_ref, wi_ref, wo_ref = rest[:8]
    o_ref = rest[-1]
    ya = jax.nn.gelu(_pick(ys5_refs, n_lat_tiles) + dsk_ref[...] * u_ref[...])
    ya = ya * jax.nn.sigmoid(_dot(ya.astype(BF16), wglu_ref[...]))
    yd = (_pick(hf_refs, n_lat_tiles) + _pick(hr_refs, n_lat_tiles)) * jax.nn.gelu(gl_ref[...])
    acc = _dot(ya.astype(BF16), wout_ref[0:gw, :])
    acc = acc + _dot(_pick(yb_refs, n_lat_tiles).astype(BF16), wout_ref[gw:2 * gw, :])
    acc = acc + _dot(_pick(yc_refs, n_lat_tiles).astype(BF16), wout_ref[2 * gw:3 * gw, :])
    acc = acc + _dot(yd.astype(BF16), wout_ref[3 * gw:4 * gw, :])
    h = h_ref[...] + mod_ref[5:6, :] * acc
    out = _ffn_core(h, mod_ref, g_ref[...], wi_ref, wo_ref, 6)
    if final:
        ms = jnp.mean(out * out, axis=-1, keepdims=True)
        out = out * lax.rsqrt(ms + EPS) * rest[8][...]
    o_ref[...] = out


def mix_out_ffn(h, mods, ys5, yb, yc, pscan, hf, hr, d_skip, w_glu, w_out, g, w_i, w_o,
                n_lat_tiles, n_ctx_tiles, tiles_per_type, g_final=None):
    gw = GROUP_WIDTH
    final = g_final is not None
    nt = n_lat_tiles + n_ctx_tiles
    row = lambda c: pl.BlockSpec((TOKEN_TILE, gw), lambda i: (i, c))
    vec = pl.BlockSpec((1, D_MODEL), lambda i: (0, 0))
    mixer_specs = _row_specs(n_lat_tiles, n_ctx_tiles, gw)
    in_specs = ([pl.BlockSpec((TOKEN_TILE, D_MODEL), lambda i: (i, 0)),
                 pl.BlockSpec((None, N_MOD, D_MODEL), lambda i: (i // tiles_per_type, 0, 0))]
                + mixer_specs * 5
                + [row(0), row(2),
                   pl.BlockSpec((1, gw), lambda i: (0, 0)), _resident((gw, gw)), _resident((D_MODEL, D_MODEL)),
                   vec, _resident((D_MODEL, 2 * D_FF)), _resident((D_FF, D_MODEL))])
    args = [h, mods, *ys5, *yb, *yc, *hf, *hr, pscan, pscan, d_skip.reshape(1, gw), w_glu, w_out,
            g.reshape(1, D_MODEL), w_i, w_o]
    if final:
        in_specs.append(vec)
        args.append(g_final.reshape(1, D_MODEL))
    return pl.pallas_call(
        functools.partial(_mix_ffn_kernel, n_src=len(ys5), n_lat_tiles=n_lat_tiles, final=final),
        grid=(nt,),
        in_specs=in_specs,
        out_specs=pl.BlockSpec((TOKEN_TILE, D_MODEL), lambda i: (i, 0)),
        out_shape=jax.ShapeDtypeStruct((nt * TOKEN_TILE, D_MODEL), F32),
        compiler_params=_cparams(("parallel",)),
        name="mix_out_ffn",
    )(*args)


def _s5_tables(lam_re, lam_im, log_step, b_re, b_im, c_re, c_im):
    L = S5_CHUNK
    G, P, H = S5_GROUPS, S5_STATE, S5_CH
    hp = lax.Precision.HIGHEST
    W = L * H
    lr, li = lam_re.astype(F32), lam_im.astype(F32)
    dt = jnp.exp(log_step.astype(F32))[..., None]
    mag = jnp.exp(lr * dt)
    ar, ai = mag * jnp.cos(li * dt), mag * jnp.sin(li * dt)
    den = lr * lr + li * li
    fr = ((ar - 1) * lr + ai * li) / den
    fi = (ai * lr - (ar - 1) * li) / den
    br, bi = b_re.astype(F32), b_im.astype(F32)
    bbr = jnp.swapaxes(fr[..., None] * br - fi[..., None] * bi, 2, 3)
    bbi = jnp.swapaxes(fr[..., None] * bi + fi[..., None] * br, 2, 3)
    cr, ci = c_re.astype(F32), c_im.astype(F32)
    tau = jnp.arange(L + 1, dtype=F32)[None, None, :, None]
    pmag = jnp.exp(tau * (lr * dt)[:, :, None, :])
    pang = tau * (li * dt)[:, :, None, :]
    pw_r, pw_i = pmag * jnp.cos(pang), pmag * jnp.sin(pang)
    abr = pw_r[:, :, :L, None, :] * bbr[:, :, None] - pw_i[:, :, :L, None, :] * bbi[:, :, None]
    abi = pw_r[:, :, :L, None, :] * bbi[:, :, None] + pw_i[:, :, :L, None, :] * bbr[:, :, None]
    kk = (jnp.einsum('dgop,dglip->dgilo', cr, abr, precision=hp)
          - jnp.einsum('dgop,dglip->dgilo', ci, abi, precision=hp))
    mt = _s5_toeplitz(kk[0].reshape(G, H, W), kk[1][:, :, ::-1, :].reshape(G, H, W))

    def rows(x, flip):
        return (x[:, ::-1] if flip else x).reshape(G, W, P)

    bs = jnp.concatenate([rows(abr[0], True), rows(abi[0], True),
                          rows(abr[1], False), rows(abi[1], False)], axis=-1).astype(BF16)
    eo = np.tile(np.eye(H, dtype=np.float32), (1, L))
    et = np.repeat(np.eye(L, dtype=np.float32), H, axis=1)
    qr = jnp.stack([pw_r[0, :, 1:], pw_r[1, :, :0:-1]])
    qi = jnp.stack([pw_i[0, :, 1:], pw_i[1, :, :0:-1]])
    crx = jnp.einsum('dgop,ox->dgpx', cr, eo, precision=hp)
    cix = jnp.einsum('dgop,ox->dgpx', ci, eo, precision=hp)
    qrx = jnp.einsum('dgtp,tx->dgpx', qr, et, precision=hp)
    qix = jnp.einsum('dgtp,tx->dgpx', qi, et, precision=hp)
    ccr = crx * qrx - cix * qix
    cci = -crx * qix - cix * qrx
    cc = jnp.stack([ccr[0], cci[0], ccr[1], cci[1]], axis=1).astype(BF16)
    al = jnp.stack([pw_r[0, :, L], pw_i[0, :, L], pw_r[1, :, L], pw_i[1, :, L]])
    return mt, bs, cc, al


def _s5_toeplitz_kernel(pf_ref, pr_ref, o_ref):
    L, H = S5_CHUNK, S5_CH
    W = L * H
    pf, pr = pf_ref[...], pr_ref[...]
    lane = lax.broadcasted_iota(jnp.int32, (H, W), 1)
    for s in range(L):
        f = pf if s == 0 else jnp.where(lane >= s * H, pltpu.roll(pf, s * H, 1), 0.0)
        back = (L - 1 - s) * H
        r = pr if back == 0 else jnp.where(lane < (s + 1) * H, pltpu.roll(pr, W - back, 1), 0.0)
        o_ref[s * H:(s + 1) * H, :] = (f + r).astype(BF16)


def _s5_toeplitz(panel_f, panel_r):
    G, H, W = panel_f.shape
    return pl.pallas_call(
        _s5_toeplitz_kernel,
        grid=(G,),
        in_specs=[pl.BlockSpec((None, H, W), lambda g: (g, 0, 0))] * 2,
        out_specs=pl.BlockSpec((None, W, W), lambda g: (g, 0, 0)),
        out_shape=jax.ShapeDtypeStruct((G, W, W), BF16),
        compiler_params=_cparams(("parallel",)),
        name="s5_toeplitz",
    )(panel_f, panel_r)


def _s5_select_tables():
    H, O = S5_CH, S5_OCT
    pack = np.zeros((O // 2, O * 128, 256), np.float32)
    unpack = np.zeros((O // 2, O * 128, 256), np.float32)
    for j in range(8):
        for g in range(O):
            for h in range(H):
                pack[g // 2, j * 128 + g * H + h, (g % 2) * 128 + j * H + h] = 1.0
                unpack[j // 2, g * 128 + j * H + h, (j % 2) * 128 + g * H + h] = 1.0
    return jnp.asarray(pack, BF16), jnp.asarray(unpack, BF16)


def _s5_kernel(ul_ref, uc_ref, pack_ref, unpack_ref, mt_ref, bs_ref, cc_ref, al_ref, ol_ref, oc_ref,
               ug_s, y_s, s_s, h_s, *, ncl, ncc):
    L, P, O = S5_CHUNK, S5_STATE, S5_OCT
    nc = ncl + ncc
    nblk = L // 8

    for tb in range(nblk):
        zs = [jnp.concatenate([ul_ref[pl.ds(8 * tb + j, ncl, stride=L), :],
                               uc_ref[pl.ds(8 * tb + j, ncc, stride=L), :]], axis=0) for j in range(8)]
        zc = jnp.concatenate(zs, axis=1).astype(BF16)
        for gp in range(O // 2):
            r = _dot(zc, pack_ref[gp])
            ug_s[2 * gp, :, tb * 128:(tb + 1) * 128] = r[:, :128]
            ug_s[2 * gp + 1, :, tb * 128:(tb + 1) * 128] = r[:, 128:]

    for g in range(O):
        ug = ug_s[g].astype(BF16)
        y_s[g] = _dot(ug, mt_ref[g])
        st = _dot(ug, bs_ref[g])
        for k in range(4):
            s_s[k, :, g, :] = st[:, k * P:(k + 1) * P]

    afr, afi, arr, ari = al_ref[0], al_ref[1], al_ref[2], al_ref[3]

    def step(j, carry):
        hfr, hfi, hrr, hri = carry
        cf = jnp.where(j < ncc, ncl + j, j - ncc)
        cr = nc - 1 - j
        h_s[0, cf] = hfr
        h_s[1, cf] = hfi
        h_s[2, cr] = hrr
        h_s[3, cr] = hri
        return (afr * hfr - afi * hfi + s_s[0, cf], afr * hfi + afi * hfr + s_s[1, cf],
                arr * hrr - ari * hri + s_s[2, cr], arr * hri + ari * hrr + s_s[3, cr])

    zero = jnp.zeros((O, P), F32)
    lax.fori_loop(0, nc, step, (zero, zero, zero, zero))

    for g in range(O):
        y = y_s[g]
        for k in range(4):
            y = y + _dot(h_s[k, :, g, :].astype(BF16), cc_ref[g, k])
        y_s[g] = y

    for tb in range(nblk):
        yc = jnp.concatenate([y_s[g, :, tb * 128:(tb + 1) * 128] for g in range(O)], axis=1)
        hi = yc.astype(BF16)
        lo = (yc - hi.astype(F32)).astype(BF16)
        for jp in range(4):
            z = _dot(hi, unpack_ref[jp]) + _dot(lo, unpack_ref[jp])
            for e in range(2):
                t = 8 * tb + 2 * jp + e
                zt = z[:, e * 128:(e + 1) * 128]
                ol_ref[pl.ds(t, ncl, stride=L), :] = zt[:ncl]
                oc_ref[pl.ds(t, ncc, stride=L), :] = zt[ncl:]


def s5_scan(pscan, tables, n_batch, t_lat, t_ctx):
    mt, bs, cc, al = tables
    pack, unpack = _s5_select_tables()
    L, P, O = S5_CHUNK, S5_STATE, S5_OCT
    W = L * S5_CH
    ncl, ncc = t_lat // L, t_ctx // L
    nc = ncl + ncc
    ctx0 = n_batch * t_lat // t_ctx
    once = pl.Buffered(1)
    return pl.pallas_call(
        functools.partial(_s5_kernel, ncl=ncl, ncc=ncc),
        grid=(S5_GROUPS // O, n_batch),
        in_specs=[
            pl.BlockSpec((t_lat, O * S5_CH), lambda o, b: (b, o)),
            pl.BlockSpec((t_ctx, O * S5_CH), lambda o, b: (ctx0 + b, o)),
            pl.BlockSpec(pack.shape, lambda o, b: (0, 0, 0), pipeline_mode=once),
            pl.BlockSpec(unpack.shape, lambda o, b: (0, 0, 0), pipeline_mode=once),
            pl.BlockSpec((O, W, W), lambda o, b: (o, 0, 0), pipeline_mode=once),
            pl.BlockSpec((O, W, 4 * P), lambda o, b: (o, 0, 0), pipeline_mode=once),
            pl.BlockSpec((O, 4, P, W), lambda o, b: (o, 0, 0, 0), pipeline_mode=once),
            pl.BlockSpec((4, O, P), lambda o, b: (0, o, 0)),
        ],
        out_specs=[
            pl.BlockSpec((t_lat, O * S5_CH), lambda o, b: (b, o)),
            pl.BlockSpec((t_ctx, O * S5_CH), lambda o, b: (b, o)),
        ],
        out_shape=[
            jax.ShapeDtypeStruct((n_batch * t_lat, GROUP_WIDTH), F32),
            jax.ShapeDtypeStruct((n_batch * t_ctx, GROUP_WIDTH), F32),
        ],
        scratch_shapes=[
            pltpu.VMEM((O, nc, W), F32),
            pltpu.VMEM((O, nc, W), F32),
            pltpu.VMEM((4, nc, O, P), F32),
            pltpu.VMEM((4, nc, O, P), F32),
        ],
        compiler_params=_cparams(("arbitrary", "arbitrary")),
        name="s5_scan",
    )(pscan, pscan, pack, unpack, mt, bs, cc, al)


def _lru_tile(x_ref, o_ref, row0, n_rows, carry, dr, w, xs_s, a_s, b_s):
    cw_ref, cb_ref, wa_ref, ba_ref, wx_ref, bx_ref, lam_ref = w
    tt = SEQ_TILE
    reverse = dr == 1
    row0 = pl.multiple_of(row0, tt)
    has_prev = (row0 > 0).astype(F32)
    has_next = (row0 + tt < n_rows).astype(F32)
    prev0 = pl.multiple_of(jnp.maximum(row0 - 8, 0), 8)
    next0 = pl.multiple_of(jnp.minimum(row0 + tt, n_rows - 8), 8)
    xs_s[dr, 0:8, :] = x_ref[pl.ds(prev0, 8), :] * has_prev
    xs_s[dr, 8:8 + tt, :] = x_ref[pl.ds(row0, tt), :]
    xs_s[dr, 8 + tt:16 + tt, :] = x_ref[pl.ds(next0, 8), :] * has_next
    xc = cb_ref[...] + jnp.zeros((tt, 128), F32)
    for tap in range(LRU_CONV):
        xc = xc + cw_ref[tap:tap + 1, :] * xs_s[dr, pl.ds(6 + tap, tt), :]
    xb = xc.astype(BF16)
    r = jax.nn.sigmoid(_dot(xb, wa_ref[dr]) + ba_ref[dr])
    i = jax.nn.sigmoid(_dot(xb, wx_ref[dr]) + bx_ref[dr])
    z = -lam_ref[dr]
    softplus = jnp.maximum(z, 0.0) + jnp.log1p(jnp.exp(-jnp.abs(z)))
    log_a = -LRU_C * r * softplus
    a_all = jnp.exp(log_a)
    th = jnp.tanh(log_a)
    b_all = jnp.sqrt(-2.0 * th / (1.0 - th)) * (i * xc)

    nseg = 8
    slen = tt // nseg
    pitch = LRU_SEG_PITCH
    for sgm in range(nseg):
        a_s[dr, sgm * pitch:sgm * pitch + slen, :] = a_all[sgm * slen:(sgm + 1) * slen, :]
        b_s[dr, sgm * pitch:sgm * pitch + slen, :] = b_all[sgm * slen:(sgm + 1) * slen, :]
    h = jnp.zeros((nseg, 128), F32)
    p = jnp.ones((nseg, 128), F32)
    for i in (range(slen - 1, -1, -1) if reverse else range(slen)):
        pos = pl.ds(i, nseg, stride=pitch)
        a = a_s[dr, pos, :]
        h = a * h + b_s[dr, pos, :]
        p = a * p
        b_s[dr, pos, :] = h
        a_s[dr, pos, :] = p
    e = carry
    ins = [None] * nseg
    for sgm in (range(nseg - 1, -1, -1) if reverse else range(nseg)):
        ins[sgm] = e
        e = h[sgm:sgm + 1, :] + p[sgm:sgm + 1, :] * e
    e_in = jnp.concatenate(ins, axis=0)
    for i in range(slen):
        pos = pl.ds(i, nseg, stride=pitch)
        b_s[dr, pos, :] = b_s[dr, pos, :] + a_s[dr, pos, :] * e_in
    for sgm in range(nseg):
        o_ref[pl.ds(row0 + sgm * slen, slen), :] = b_s[dr, sgm * pitch:sgm * pitch + slen, :]
    return e


def _lru_kernel(xl_ref, xc_ref, cw_ref, cb_ref, wa_ref, ba_ref, wx_ref, bx_ref, lam_ref,
                hfl_ref, hfc_ref, hrl_ref, hrc_ref, xs_s, a_s, b_s, *, ntl, ntc):
    w = (cw_ref, cb_ref, wa_ref, ba_ref, wx_ref, bx_ref, lam_ref)
    tt = SEQ_TILE

    def run(x_ref, of_ref, or_ref, nt, carry):
        def body(j, c):
            cf = _lru_tile(x_ref, of_ref, j * tt, nt * tt, c[0], 0, w, xs_s, a_s, b_s)
            cr = _lru_tile(x_ref, or_ref, (nt - 1 - j) * tt, nt * tt, c[1], 1, w, xs_s, a_s, b_s)
            return cf, cr
        return lax.fori_loop(0, nt, body, carry)

    zero = jnp.zeros((1, 128), F32)
    carry = run(xc_ref, hfc_ref, hrc_ref, ntc, (zero, zero))
    run(xl_ref, hfl_ref, hrl_ref, ntl, carry)


def lru_scan(pscan, conv_w, conv_b, w_a, b_a, w_x, b_x, lam, n_batch, t_lat, t_ctx):
    tt = SEQ_TILE
    gw = GROUP_WIDTH
    nh = gw // 128
    ntl, ntc = t_lat // tt, t_ctx // tt
    ctx0 = n_batch * t_lat // t_ctx
    per_half = LRU_BLOCKS // nh

    def halves(w):
        return jnp.stack([jnp.stack([_block_diag(w[dr, k * per_half:(k + 1) * per_half]) for k in range(nh)])
                          for dr in range(2)]).astype(BF16)

    vec = lambda a: a.reshape(2, 1, gw)
    vspec = pl.BlockSpec((2, 1, 128), lambda b, k: (0, 0, k))
    mspec = pl.BlockSpec((2, None, 128, 128), lambda b, k: (0, k, 0, 0))
    lat = pl.BlockSpec((t_lat, 128), lambda b, k: (b, k))
    ctx = pl.BlockSpec((t_ctx, 128), lambda b, k: (b, k))
    lat_shape = jax.ShapeDtypeStruct((n_batch * t_lat, gw), F32)
    ctx_shape = jax.ShapeDtypeStruct((n_batch * t_ctx, gw), F32)
    hfl, hfc, hrl, hrc = pl.pallas_call(
        functools.partial(_lru_kernel, ntl=ntl, ntc=ntc),
        grid=(n_batch, nh),
        in_specs=[
            pl.BlockSpec((t_lat, 128), lambda b, k: (b, nh + k)),
            pl.BlockSpec((t_ctx, 128), lambda b, k: (ctx0 + b, nh + k)),
            pl.BlockSpec((LRU_CONV, 128), lambda b, k: (0, k)),
            pl.BlockSpec((1, 128), lambda b, k: (0, k)),
            mspec, vspec, mspec, vspec, vspec,
        ],
        out_specs=[lat, ctx, lat, ctx],
        out_shape=[lat_shape, ctx_shape, lat_shape, ctx_shape],
        scratch_shapes=[
            pltpu.VMEM((2, tt + 16, 128), F32),
            pltpu.VMEM((2, 8 * LRU_SEG_PITCH, 128), F32),
            pltpu.VMEM((2, 8 * LRU_SEG_PITCH, 128), F32),
        ],
        compiler_params=_cparams(("parallel", "parallel")),
        name="lru_scan",
    )(pscan, pscan, conv_w, conv_b.reshape(1, gw), halves(w_a), vec(b_a), halves(w_x), vec(b_x), vec(lam))
    return [hfl, hfc], [hrl, hrc]


def _block_diag(w):
    n, d, e = w.shape
    eye = jnp.eye(n, dtype=w.dtype)
    return (eye[:, None, :, None] * w[:, :, None, :]).reshape(n * d, n * e)


def _natten_bias(rpb):
    W, KR, KC, QR, NB = GRID_W, NA_KR, NA_KC, NA_QROWS, NA_BAND
    col = np.arange(W)
    cs = np.clip(col - KC // 2, 0, W - KC)
    inwin = (col[None, :] >= cs[:, None]) & (col[None, :] < cs[:, None] + KC)
    coff = np.clip(col[None, :] - col[:, None] + (KC - 1), 0, 2 * KC - 2)
    a = np.arange(QR)[:, None]
    i = np.arange(NB)[None, :]
    first = (i - a, (i < KR) & (a >= 0))
    mid = (i - a - KR // 2, (i >= a) & (i < a + KR))
    last = (i - a - NB + QR, (i >= NB - KR) & (a >= 0))
    coh = (coff[:, :, None] == np.arange(2 * KC - 1)).astype(np.float32)
    t = jnp.einsum('hrc,qkc->hrqk', rpb.astype(F32), coh, precision=lax.Precision.HIGHEST)
    t = t * LOG2E + np.where(inwin, 0.0, NEG_BIG).astype(np.float32)
    plan = tuple(
        tuple(tuple(int(np.clip(delta[ai, ii] + KR - 1, 0, 2 * KR - 2)) if valid[ai, ii] else -1
                    for ii in range(NB)) for ai in range(QR))
        for delta, valid in (first, mid, last))

    def expand(t_ref, o_ref):
        for v in range(3):
            for ai in range(QR):
                for ii in range(NB):
                    r = plan[v][ai][ii]
                    blk = t_ref[r] if r >= 0 else jnp.full((W, W), NEG_BIG, F32)
                    o_ref[v, ai * W:(ai + 1) * W, ii * W:(ii + 1) * W] = blk

    nh = rpb.shape[0]
    return pl.pallas_call(
        expand,
        grid=(nh,),
        in_specs=[pl.BlockSpec((None, 2 * KR - 1, W, W), lambda h: (h, 0, 0, 0))],
        out_specs=pl.BlockSpec((3, None, QR * W, NB * W), lambda h: (0, h, 0, 0)),
        out_shape=jax.ShapeDtypeStruct((3, nh, QR * W, NB * W), F32),
        compiler_params=_cparams(("parallel",)),
        name="natten_bias",
    )(t)


def _natten_kernel(q_ref, k_ref, v_ref, kc_ref, vc_ref, bias_ref, o_ref, *, rows):
    W, KR, hd = GRID_W, NA_KR, HEAD_DIM
    nblk = rows // NA_QROWS
    nb = NA_BAND * W
    tq = NA_QROWS * W
    ones_col = (lax.broadcasted_iota(jnp.int32, (nb + kc_ref.shape[0], hd), 1) == 0).astype(BF16)
    for u in range(NA_STEP_BLOCKS):
        blk = pl.program_id(1) * NA_STEP_BLOCKS + u
        bs = jnp.clip(blk * NA_QROWS - KR // 2, 0, rows - NA_BAND)
        var = jnp.where(blk == 0, 0, jnp.where(blk == nblk - 1, 2, 1))
        start = pl.multiple_of(bs * W, W)
        q = (q_ref[u * tq:(u + 1) * tq, :].astype(F32) * ((hd ** -0.5) * LOG2E)).astype(BF16)
        k_all = jnp.concatenate([k_ref[pl.ds(start, nb), :], kc_ref[...]], axis=0)
        v_all = jnp.concatenate([v_ref[pl.ds(start, nb), :], vc_ref[...]], axis=0)
        for h in range(NA_HEADS):
            sl = slice(h * hd, (h + 1) * hd)
            s = _dot_nt(q[:, sl], k_all[:, sl])
            sb = s[:, :nb] + bias_ref[var, h]
            sc = s[:, nb:]
            m = jnp.maximum(jnp.max(sb, axis=-1, keepdims=True), jnp.max(sc, axis=-1, keepdims=True))
            p = jnp.concatenate([jnp.exp2(sb - m), jnp.exp2(sc - m)], axis=1).astype(BF16)
            o = _dot(p, jnp.concatenate([v_all[:, sl], ones_col], axis=1))
            o_ref[u * tq:(u + 1) * tq, sl] = o[:, :hd] / o[:, hd:hd + 1]


def natten_latent(pattn, bias, n_batch, t_lat, t_ctx):
    gw = GROUP_WIDTH
    rows = t_lat // GRID_W
    assert NA_QROWS == NA_KR // 2 and NA_BAND == NA_KR + NA_QROWS and rows % NA_QROWS == 0 and rows >= NA_BAND
    assert (rows // NA_QROWS) % NA_STEP_BLOCKS == 0
    nblk = rows // NA_QROWS // NA_STEP_BLOCKS
    tq = NA_STEP_BLOCKS * NA_QROWS * GRID_W
    ctx0 = n_batch * t_lat // t_ctx
    return pl.pallas_call(
        functools.partial(_natten_kernel, rows=rows),
        grid=(n_batch, nblk),
        in_specs=[
            pl.BlockSpec((tq, gw), lambda b, r: (b * nblk + r, 2)),
            pl.BlockSpec((t_lat, gw), lambda b, r: (b, 3)),
            pl.BlockSpec((t_lat, gw), lambda b, r: (b, 4)),
            pl.BlockSpec((t_ctx, gw), lambda b, r: (ctx0 + b, 3)),
            pl.BlockSpec((t_ctx, gw), lambda b, r: (ctx0 + b, 4)),
            pl.BlockSpec(bias.shape, lambda b, r: (0, 0, 0, 0)),
        ],
        out_specs=pl.BlockSpec((tq, gw), lambda b, r: (b * nblk + r, 0)),
        out_shape=jax.ShapeDtypeStruct((n_batch * t_lat, gw), F32),
        compiler_params=_cparams(("parallel", "arbitrary")),
        name="natten_latent",
    )(pattn, pattn, pattn, pattn, pattn, bias)


def _ctx_attn_kernel(q_ref, k_ref, v_ref, o_ref):
    hd = HEAD_DIM
    q, k, v = q_ref[...], k_ref[...], v_ref[...]
    for h in range(NA_HEADS):
        sl = slice(h * hd, (h + 1) * hd)
        s = _dot_nt(q[:, sl], k[:, sl]) * (hd ** -0.5)
        m = jnp.max(s, axis=-1, keepdims=True)
        p = jnp.exp(s - m)
        l = jnp.sum(p, axis=-1, keepdims=True)
        o_ref[:, sl] = _dot(p.astype(BF16), v[:, sl]) / l


def natten_context(pattn, n_batch, t_lat, t_ctx):
    gw = GROUP_WIDTH
    ctx0 = n_batch * t_lat // t_ctx
    return pl.pallas_call(
        _ctx_attn_kernel,
        grid=(n_batch,),
        in_specs=[pl.BlockSpec((t_ctx, gw), lambda b, c=c: (ctx0 + b, c)) for c in (2, 3, 4)],
        out_specs=pl.BlockSpec((t_ctx, gw), lambda b: (b, 0)),
        out_shape=jax.ShapeDtypeStruct((n_batch * t_ctx, gw), F32),
        compiler_params=_cparams(("parallel",)),
        name="natten_context",
    )(pattn, pattn, pattn)


def _rope_tables(t_lat, t_ctx):
    half = HEAD_DIM // 4
    freqs = ROPE_BASE ** (-jnp.arange(half, dtype=F32) / half)
    t = jnp.arange(t_lat)
    ang_r = (t // GRID_W).astype(F32)[:, None] * freqs[None, :]
    ang_c = (t % GRID_W).astype(F32)[:, None] * freqs[None, :]
    ang = jnp.concatenate([ang_r, ang_r, ang_c, ang_c], axis=1)
    cos = jnp.concatenate([jnp.cos(ang), jnp.ones((t_ctx, HEAD_DIM), F32)], axis=0)
    sin = jnp.concatenate([jnp.sin(ang), jnp.zeros((t_ctx, HEAD_DIM), F32)], axis=0)
    return jnp.tile(cos, (1, GQA_Q_HEADS)), jnp.tile(sin, (1, GQA_Q_HEADS))


def _gqa_prep_kernel(x_ref, cos_ref, sin_ref, gq_ref, gk_ref, gm_ref, q_ref, k_ref, v_ref):
    hd = HEAD_DIM
    nq, nk = GQA_Q_HEADS * hd, GQA_KV_HEADS * hd
    x = x_ref[...].astype(F32)
    cos, sin = cos_ref[...], sin_ref[...]
    gm = gm_ref[...]

    def norm_rope(t, gain, width):
        ms = _dot((t * t).astype(BF16), gm[:width, :width])
        tn = t * lax.rsqrt(ms + EPS) * gain
        lane = lax.broadcasted_iota(jnp.int32, tn.shape, 1)
        first = (lane % (hd // 2)) < (hd // 4)
        rot = jnp.where(first, -pltpu.roll(tn, width - hd // 4, 1), pltpu.roll(tn, hd // 4, 1))
        return tn * cos[:, :width] + rot * sin[:, :width]

    q = norm_rope(x[:, :nq], gq_ref[...], nq) * ((hd ** -0.5) * LOG2E)
    kt = norm_rope(x[:, nq:nq + nk], gk_ref[...], nk).T
    v = x_ref[:, nq + nk:nq + 2 * nk]
    for h in range(GQA_Q_HEADS):
        q_ref[h] = q[:, h * hd:(h + 1) * hd].astype(BF16)
    ones_col = (lax.broadcasted_iota(jnp.int32, (v.shape[0], hd), 1) == 0).astype(BF16)
    for h in range(GQA_KV_HEADS):
        k_ref[h] = kt[h * hd:(h + 1) * hd, :].astype(BF16)
        v_ref[h] = jnp.concatenate([v[:, h * hd:(h + 1) * hd], ones_col], axis=1)


def gqa_prep(pattn, gq, gk, n_batch, t_lat, t_ctx):
    tt = SEQ_TILE
    assert tt == KEY_CHUNK
    hd = HEAD_DIM
    ntl, ntc = t_lat // tt, t_ctx // tt
    s_len = t_lat + t_ctx
    cos, sin = _rope_tables(t_lat, t_ctx)
    nq = GQA_Q_HEADS * hd
    gmat = _block_diag(jnp.full((GQA_Q_HEADS, hd, hd), 1.0 / hd, F32)).astype(BF16)

    def pos(i):
        is_ctx = i >= n_batch * ntl
        c = i - n_batch * ntl
        return jnp.where(is_ctx, ntl + c % ntc, i % ntl)

    def bat(i):
        is_ctx = i >= n_batch * ntl
        return jnp.where(is_ctx, (i - n_batch * ntl) // ntc, i // ntl)

    return pl.pallas_call(
        _gqa_prep_kernel,
        grid=(n_batch * (ntl + ntc),),
        in_specs=[
            pl.BlockSpec((tt, 2 * nq), lambda i: (i, 0)),
            pl.BlockSpec((tt, nq), lambda i: (pos(i), 0)),
            pl.BlockSpec((tt, nq), lambda i: (pos(i), 0)),
            pl.BlockSpec((1, nq), lambda i: (0, 0)),
            pl.BlockSpec((1, nq // 2), lambda i: (0, 0)),
            pl.BlockSpec((nq, nq), lambda i: (0, 0)),
        ],
        out_specs=[
            pl.BlockSpec((None, GQA_Q_HEADS, tt, hd), lambda i: (bat(i), 0, pos(i), 0)),
            pl.BlockSpec((None, GQA_KV_HEADS, None, hd, tt), lambda i: (bat(i), 0, pos(i), 0, 0)),
            pl.BlockSpec((None, GQA_KV_HEADS, tt, 2 * hd), lambda i: (bat(i), 0, pos(i), 0)),
        ],
        out_shape=[
            jax.ShapeDtypeStruct((n_batch, GQA_Q_HEADS, s_len, hd), BF16),
            jax.ShapeDtypeStruct((n_batch, GQA_KV_HEADS, s_len // tt, hd, tt), BF16),
            jax.ShapeDtypeStruct((n_batch, GQA_KV_HEADS, s_len, 2 * hd), BF16),
        ],
        compiler_params=_cparams(("parallel",)),
        name="gqa_prep",
    )(pattn, cos, sin, jnp.tile(gq.astype(F32), GQA_Q_HEADS).reshape(1, nq),
      jnp.tile(gk.astype(F32), GQA_KV_HEADS).reshape(1, nq // 2), gmat)


def _gqa_kernel(q_ref, kt_ref, v_ref, o_ref, s_s, m_s, acc_s, *, unroll):
    g, tq, hd = q_ref.shape
    nch, _, ck = kt_ref.shape
    nl = ck // 128
    q = q_ref[...].reshape(g * tq, hd)
    m_s[...] = jnp.full_like(m_s, -jnp.inf)

    def scores(c, carry):
        s = _dot(q, kt_ref[c])
        s_s[c] = s
        fold = s[:, 0:128]
        for j in range(1, nl):
            fold = jnp.maximum(fold, s[:, j * 128:(j + 1) * 128])
        m_s[...] = jnp.maximum(m_s[...], fold)
        return carry

    lax.fori_loop(0, nch, scores, 0, unroll=unroll)
    m_s[...] = jnp.broadcast_to(jnp.max(m_s[...], axis=-1, keepdims=True), m_s.shape)
    acc_s[...] = jnp.zeros_like(acc_s)

    def values(grp, carry):
        m = m_s[...]
        ps = []
        for u in range(unroll):
            s = s_s[grp * unroll + u]
            ps += [jnp.exp2(s[:, j * 128:(j + 1) * 128] - m).astype(BF16) for j in range(nl)]
        off = pl.multiple_of(grp * (unroll * ck), unroll * ck)
        acc_s[...] += _dot(jnp.concatenate(ps, axis=1), v_ref[pl.ds(off, unroll * ck), :])
        return carry

    lax.fori_loop(0, nch // unroll, values, 0)
    acc = acc_s[...]
    o = acc[:, :hd] / acc[:, hd:hd + 1]
    o_ref[...] = jnp.concatenate([o[i * tq:(i + 1) * tq] for i in range(g)], axis=1)


def gqa_attention(q, kt, v, tq, q_blk0, nq, c_blk, nch):
    n_batch = q.shape[0]
    hd = HEAD_DIM
    g = GQA_GROUP
    ck = KEY_CHUNK
    unroll = next(u for u in (11, 3, 1) if nch % u == 0)
    return pl.pallas_call(
        functools.partial(_gqa_kernel, unroll=unroll),
        grid=(n_batch, GQA_KV_HEADS, nq),
        in_specs=[
            pl.BlockSpec((None, g, tq, hd), lambda b, h, i: (b, h, q_blk0 + i, 0)),
            pl.BlockSpec((None, None, nch, hd, ck), lambda b, h, i: (b, h, c_blk, 0, 0)),
            pl.BlockSpec((None, None, nch * ck, 2 * hd), lambda b, h, i: (b, h, c_blk, 0)),
        ],
        out_specs=pl.BlockSpec((tq, g * hd), lambda b, h, i: (b * nq + i, h)),
        out_shape=jax.ShapeDtypeStruct((n_batch * nq * tq, GQA_Q_HEADS * hd), F32),
        scratch_shapes=[
            pltpu.VMEM((nch, g * tq, ck), F32),
            pltpu.VMEM((g * tq, 128), F32),
            pltpu.VMEM((g * tq, 2 * hd), F32),
        ],
        compiler_params=_cparams(("parallel", "parallel", "arbitrary")),
        name="gqa_attention",
    )(q, kt, v)


def kernel(x, c, ctx, c_ctx, w_ada, b_ada, g_ffn1, w_ffn1_in, w_ffn1_out, g_mix, w_in, w_out, s5_lambda_re, s5_lambda_im, s5_log_step, s5_b_re, s5_b_im, s5_c_re, s5_c_im, s5_d, s5_w_glu, na_rpb, gqa_q_norm, gqa_k_norm, lru_conv_w, lru_conv_b, lru_w_a, lru_b_a, lru_w_x, lru_b_x, lru_lambda, g_ffn2, w_ffn2_in, w_ffn2_out, g_final):
    n_batch, t_lat, d = x.shape
    t_ctx = ctx.shape[1]
    depth = w_ada.shape[0]
    assert d == D_MODEL and t_lat % TOKEN_TILE == 0 and (n_batch * t_ctx) % TOKEN_TILE == 0
    assert t_lat % t_ctx == 0 and t_ctx % SEQ_TILE == 0 and t_lat % FLASH_TQ == 0
    n_lat = n_batch * t_lat
    n_all = n_lat + n_batch * t_ctx
    tiles_per_type = t_lat // TOKEN_TILE
    assert (n_batch * t_ctx) // TOKEN_TILE <= tiles_per_type

    c8 = jnp.zeros((8, D_MODEL), F32).at[:n_batch].set(c.astype(F32)).at[n_batch].set(c_ctx.astype(F32))
    mods_all = ada_mods(c8, w_ada, b_ada).reshape(depth, 8, N_MOD, D_MODEL)

    s_len = t_lat + t_ctx
    n_lat_tiles = n_lat // TOKEN_TILE
    n_ctx_tiles = (n_all - n_lat) // TOKEN_TILE
    hs = [x.reshape(n_lat, D_MODEL), ctx.reshape(n_batch * t_ctx, D_MODEL)]

    for l in range(depth):
        need_ctx = l < depth - 1
        mods = mods_all[l]
        w1i, w1o = w_ffn1_in[l].astype(BF16), w_ffn1_out[l].astype(BF16)
        w2i, w2o = w_ffn2_in[l].astype(BF16), w_ffn2_out[l].astype(BF16)
        w_in_l = w_in[l].astype(BF16)

        h, pscan, pattn = ffn_in_projection(hs, mods, g_ffn1[l], w1i, w1o, g_mix[l], w_in_l,
                                            n_lat_tiles, n_ctx_tiles, tiles_per_type)

        tables = _s5_tables(s5_lambda_re[l], s5_lambda_im[l], s5_log_step[l], s5_b_re[l], s5_b_im[l],
                            s5_c_re[l], s5_c_im[l])
        ys5 = list(s5_scan(pscan, tables, n_batch, t_lat, t_ctx))

        hf, hr = lru_scan(pscan, lru_conv_w[l], lru_conv_b[l], lru_w_a[l], lru_b_a[l], lru_w_x[l], lru_b_x[l],
                          lru_lambda[l], n_batch, t_lat, t_ctx)

        yb = [natten_latent(pattn, _natten_bias(na_rpb[l]), n_batch, t_lat, t_ctx)]

        qh, kth, vh = gqa_prep(pattn, gqa_q_norm[l], gqa_k_norm[l], n_batch, t_lat, t_ctx)
        yc = [gqa_attention(qh, kth, vh, FLASH_TQ, 0, t_lat // FLASH_TQ, 0, s_len // KEY_CHUNK)]

        if need_ctx:
            yb.append(natten_context(pattn, n_batch, t_lat, t_ctx))
            yc.append(gqa_attention(qh, kth, vh, t_ctx, t_lat // t_ctx, 1, t_lat // t_ctx, t_ctx // KEY_CHUNK))
        else:
            ys5, hf, hr = ys5[:1], hf[:1], hr[:1]

        h = mix_out_ffn(h, mods, ys5, yb, yc, pscan, hf, hr, s5_d[l], s5_w_glu[l].astype(BF16),
                        w_out[l].astype(BF16), g_ffn2[l], w2i, w2o, n_lat_tiles,
                        n_ctx_tiles if need_ctx else 0, tiles_per_type, g_final=None if need_ctx else g_final)
        hs = [h]
    return h.reshape(n_batch, t_lat, D_MODEL)
```

```python
import functools
import math

import numpy as np
import jax
import jax.numpy as jnp
from jax import lax
from jax.experimental import pallas as pl
from jax.experimental.pallas import tpu as pltpu

F32 = jnp.float32
BF16 = jnp.bfloat16

D_MODEL = 1024
GRID_W = 64
HEAD_DIM = 64
GROUP_WIDTH = D_MODEL // 4
S5_CH = 16
S5_GROUPS = GROUP_WIDTH // S5_CH
S5_STATE = 64
NA_HEADS = GROUP_WIDTH // HEAD_DIM
NA_KR = 8
NA_KC = 16
GQA_Q_HEADS = GROUP_WIDTH // HEAD_DIM
GQA_KV_HEADS = GQA_Q_HEADS // 2
GQA_GROUP = GQA_Q_HEADS // GQA_KV_HEADS
LRU_BLOCKS = GROUP_WIDTH // HEAD_DIM
LRU_CONV = 4
LRU_C = 8.0
D_FF = ((8 * D_MODEL // 3 + 127) // 128) * 128
FFN_RES = 0.5
ROPE_BASE = 10000.0
EPS = 1e-6
N_MOD = 9

TOKEN_TILE = 512
SEQ_TILE = 256
LRU_SEG_PITCH = SEQ_TILE // 8 + 4
S5_CHUNK = 32
S5_OCT = 8
MXU_DEPTH = 256
FF_CHUNKS = ((0, 6 * MXU_DEPTH), (6 * MXU_DEPTH, D_FF))
FLASH_TQ = 512
KEY_CHUNK = 256
NA_QROWS = NA_KR // 2
NA_BAND = NA_KR + NA_QROWS
NA_STEP_BLOCKS = 2
LOG2E = math.log2(math.e)
NEG_BIG = -1e30
VMEM_LIMIT = 56 * 1024 * 1024


def _cparams(sem):
    return pltpu.CompilerParams(dimension_semantics=sem, vmem_limit_bytes=VMEM_LIMIT)


def _dot(a, b):
    return jnp.dot(a, b, preferred_element_type=F32)


def _dot_nt(a, b):
    return lax.dot_general(a, b, (((1,), (1,)), ((), ())), preferred_element_type=F32)


def _ada_kernel(c_ref, w_ref, b_ref, o_ref):
    c = c_ref[...]
    s = c * jax.nn.sigmoid(c)
    o_ref[...] = _dot(s.astype(BF16), w_ref[...].astype(BF16)) + b_ref[...]


def ada_mods(c8, w_ada, b_ada):
    L = w_ada.shape[0]
    tn = D_MODEL
    return pl.pallas_call(
        _ada_kernel,
        grid=(L, N_MOD * D_MODEL // tn),
        in_specs=[
            pl.BlockSpec((8, D_MODEL), lambda l, j: (0, 0)),
            pl.BlockSpec((None, D_MODEL, tn), lambda l, j: (l, 0, j)),
            pl.BlockSpec((None, 1, tn), lambda l, j: (l, 0, j)),
        ],
        out_specs=pl.BlockSpec((None, 8, tn), lambda l, j: (l, 0, j)),
        out_shape=jax.ShapeDtypeStruct((L, 8, N_MOD * D_MODEL), F32),
        compiler_params=_cparams(("parallel", "parallel")),
        name="ada_mods",
    )(c8, w_ada, b_ada.reshape(L, 1, N_MOD * D_MODEL))


def _norm_mod(x, g, shift, scale):
    ms = jnp.mean(x * x, axis=-1, keepdims=True)
    y = x * lax.rsqrt(ms + EPS) * g
    return y * (1.0 + scale) + shift


def _ffn_core(x, mod_ref, g, wi_ref, wo_ref, k0):
    y = _norm_mod(x, g, mod_ref[k0:k0 + 1, :], mod_ref[k0 + 1:k0 + 2, :]).astype(BF16)
    acc = jnp.zeros(x.shape, F32)
    for lo, hi in FF_CHUNKS:
        a = _dot(y, wi_ref[:, lo:hi])
        b = _dot(y, wi_ref[:, D_FF + lo:D_FF + hi])
        h = (a * jax.nn.sigmoid(a) * b).astype(BF16)
        acc = acc + _dot(h, wo_ref[lo:hi, :])
    return x + FFN_RES * mod_ref[k0 + 2:k0 + 3, :] * acc


def _resident(shape, layer):
    return pl.BlockSpec((None,) + shape, lambda i: (layer,) + (0,) * len(shape), pipeline_mode=pl.Buffered(1))


def _row_specs(n_lat_tiles, n_ctx_tiles, width, col=0):
    lat = pl.BlockSpec((TOKEN_TILE, width), lambda i: (jnp.minimum(i, n_lat_tiles - 1), col))
    if n_ctx_tiles == 0:
        return [lat]
    ctx = pl.BlockSpec((TOKEN_TILE, width), lambda i: (jnp.clip(i - n_lat_tiles, 0, n_ctx_tiles - 1), col))
    return [lat, ctx]


def _pick(refs, n_lat_tiles):
    if len(refs) == 1:
        return refs[0][...]
    return jnp.where(pl.program_id(0) >= n_lat_tiles, refs[1][...], refs[0][...])


def _ffn_in_kernel(*refs, n_src, n_lat_tiles):
    x_refs = refs[:n_src]
    mod_ref, g_ref, wi_ref, wo_ref, gm_ref, win_ref, h_ref, ps_ref, pa_ref = refs[n_src:]
    h = _ffn_core(_pick(x_refs, n_lat_tiles), mod_ref, g_ref[...], wi_ref, wo_ref, 0)
    h_ref[...] = h
    y = _norm_mod(h, gm_ref[...], mod_ref[3:4, :], mod_ref[4:5, :]).astype(BF16)
    p = _dot(y, win_ref[...])
    gw = GROUP_WIDTH
    ps_ref[...] = jnp.concatenate([p[:, 0:gw], p[:, 6 * gw:8 * gw]], axis=1)
    pa_ref[...] = jnp.concatenate([p[:, 4 * gw:6 * gw], p[:, gw:4 * gw]], axis=1).astype(BF16)


def ffn_in_projection(xs, mods, g, w_i, w_o, g_mix, w_in_l, layer, n_lat_tiles, n_ctx_tiles, tiles_per_type):
    nt = n_lat_tiles + n_ctx_tiles
    n = nt * TOKEN_TILE
    n_scan = 3 * GROUP_WIDTH
    n_attn = w_in_l.shape[-1] - n_scan
    x_specs = (_row_specs(n_lat_tiles, n_ctx_tiles, D_MODEL) if len(xs) == 2
               else [pl.BlockSpec((TOKEN_TILE, D_MODEL), lambda i: (i, 0))])
    vec = pl.BlockSpec((1, D_MODEL), lambda i: (0, 0))
    return pl.pallas_call(
        functools.partial(_ffn_in_kernel, n_src=len(xs), n_lat_tiles=n_lat_tiles),
        grid=(nt,),
        in_specs=x_specs + [
            pl.BlockSpec((None, N_MOD, D_MODEL), lambda i: (i // tiles_per_type, 0, 0)),
            vec, _resident((D_MODEL, 2 * D_FF), layer), _resident((D_FF, D_MODEL), layer),
            vec, _resident((D_MODEL, n_scan + n_attn), layer),
        ],
        out_specs=[
            pl.BlockSpec((TOKEN_TILE, D_MODEL), lambda i: (i, 0)),
            pl.BlockSpec((TOKEN_TILE, n_scan), lambda i: (i, 0)),
            pl.BlockSpec((TOKEN_TILE, n_attn), lambda i: (i, 0)),
        ],
        out_shape=[
            jax.ShapeDtypeStruct((n, D_MODEL), F32),
            jax.ShapeDtypeStruct((n, n_scan), F32),
            jax.ShapeDtypeStruct((n, n_attn), BF16),
        ],
        compiler_params=_cparams(("parallel",)),
        name="ffn_in_projection",
    )(*xs, mods, g.reshape(1, D_MODEL), w_i, w_o, g_mix.reshape(1, D_MODEL), w_in_l)


def _mix_ffn_kernel(*refs, n_src, n_lat_tiles, final):
    gw = GROUP_WIDTH
    h_ref, mod_ref = refs[:2]
    ys5_refs, yb_refs, yc_refs, hf_refs, hr_refs = (refs[2 + k * n_src:2 + (k + 1) * n_src] for k in range(5))
    rest = refs[2 + 5 * n_src:]
    u_ref, gl_ref, dsk_ref, wglu_ref, wout_ref, g_ref, wi_ref, wo_ref = rest[:8]
    o_ref = rest[-1]
    ya = jax.nn.gelu(_pick(ys5_refs, n_lat_tiles) + dsk_ref[...] * u_ref[...])
    ya = ya * jax.nn.sigmoid(_dot(ya.astype(BF16), wglu_ref[...]))
    yd = (_pick(hf_refs, n_lat_tiles) + _pick(hr_refs, n_lat_tiles)) * jax.nn.gelu(gl_ref[...])
    acc = _dot(ya.astype(BF16), wout_ref[0:gw, :])
    acc = acc + _dot(_pick(yb_refs, n_lat_tiles).astype(BF16), wout_ref[gw:2 * gw, :])
    acc = acc + _dot(_pick(yc_refs, n_lat_tiles).astype(BF16), wout_ref[2 * gw:3 * gw, :])
    acc = acc + _dot(yd.astype(BF16), wout_ref[3 * gw:4 * gw, :])
    h = h_ref[...] + mod_ref[5:6, :] * acc
    out = _ffn_core(h, mod_ref, g_ref[...], wi_ref, wo_ref, 6)
    if final:
        ms = jnp.mean(out * out, axis=-1, keepdims=True)
        out = out * lax.rsqrt(ms + EPS) * rest[8][...]
    o_ref[...] = out


def mix_out_ffn(h, mods, ys5, yb, yc, pscan, hf, hr, d_skip, w_glu, w_out, g, w_i, w_o, layer,
                n_lat_tiles, n_ctx_tiles, tiles_per_type, g_final=None):
    gw = GROUP_WIDTH
    final = g_final is not None
    nt = n_lat_tiles + n_ctx_tiles
    row = lambda c: pl.BlockSpec((TOKEN_TILE, gw), lambda i: (i, c))
    vec = pl.BlockSpec((1, D_MODEL), lambda i: (0, 0))
    mixer_specs = _row_specs(n_lat_tiles, n_ctx_tiles, gw)
    in_specs = ([pl.BlockSpec((TOKEN_TILE, D_MODEL), lambda i: (i, 0)),
                 pl.BlockSpec((None, N_MOD, D_MODEL), lambda i: (i // tiles_per_type, 0, 0))]
                + mixer_specs * 5
                + [row(0), row(2),
                   pl.BlockSpec((1, gw), lambda i: (0, 0)), _resident((gw, gw), layer),
                   _resident((D_MODEL, D_MODEL), layer),
                   vec, _resident((D_MODEL, 2 * D_FF), layer), _resident((D_FF, D_MODEL), layer)])
    args = [h, mods, *ys5, *yb, *yc, *hf, *hr, pscan, pscan, d_skip.reshape(1, gw), w_glu, w_out,
            g.reshape(1, D_MODEL), w_i, w_o]
    if final:
        in_specs.append(vec)
        args.append(g_final.reshape(1, D_MODEL))
    return pl.pallas_call(
        functools.partial(_mix_ffn_kernel, n_src=len(ys5), n_lat_tiles=n_lat_tiles, final=final),
        grid=(nt,),
        in_specs=in_specs,
        out_specs=pl.BlockSpec((TOKEN_TILE, D_MODEL), lambda i: (i, 0)),
        out_shape=jax.ShapeDtypeStruct((nt * TOKEN_TILE, D_MODEL), F32),
        compiler_params=_cparams(("parallel",)),
        name="mix_out_ffn",
    )(*args)


def _s5_tables(lam_re, lam_im, log_step, b_re, b_im, c_re, c_im):
    L = S5_CHUNK
    G, P, H = S5_GROUPS, S5_STATE, S5_CH
    hp = lax.Precision.HIGHEST
    W = L * H
    lr, li = lam_re.astype(F32), lam_im.astype(F32)
    dt = jnp.exp(log_step.astype(F32))[..., None]
    mag = jnp.exp(lr * dt)
    ar, ai = mag * jnp.cos(li * dt), mag * jnp.sin(li * dt)
    den = lr * lr + li * li
    fr = ((ar - 1) * lr + ai * li) / den
    fi = (ai * lr - (ar - 1) * li) / den
    br, bi = b_re.astype(F32), b_im.astype(F32)
    bbr = jnp.swapaxes(fr[..., None] * br - fi[..., None] * bi, 2, 3)
    bbi = jnp.swapaxes(fr[..., None] * bi + fi[..., None] * br, 2, 3)
    cr, ci = c_re.astype(F32), c_im.astype(F32)
    tau = jnp.arange(L + 1, dtype=F32)[None, None, :, None]
    pmag = jnp.exp(tau * (lr * dt)[:, :, None, :])
    pang = tau * (li * dt)[:, :, None, :]
    pw_r, pw_i = pmag * jnp.cos(pang), pmag * jnp.sin(pang)
    abr = pw_r[:, :, :L, None, :] * bbr[:, :, None] - pw_i[:, :, :L, None, :] * bbi[:, :, None]
    abi = pw_r[:, :, :L, None, :] * bbi[:, :, None] + pw_i[:, :, :L, None, :] * bbr[:, :, None]
    mt = _s5_toeplitz(abr.reshape(2, G, W, P), abi.reshape(2, G, W, P), cr, ci)

    def rows(x, flip):
        return (x[:, ::-1] if flip else x).reshape(G, W, P)

    bs = jnp.concatenate([rows(abr[0], True), rows(abi[0], True),
                          rows(abr[1], False), rows(abi[1], False)], axis=-1).astype(BF16)
    eo = np.tile(np.eye(H, dtype=np.float32), (1, L))
    et = np.repeat(np.eye(L, dtype=np.float32), H, axis=1)
    qr = jnp.stack([pw_r[0, :, 1:], pw_r[1, :, :0:-1]])
    qi = jnp.stack([pw_i[0, :, 1:], pw_i[1, :, :0:-1]])
    crx = jnp.einsum('dgop,ox->dgpx', cr, eo, precision=hp)
    cix = jnp.einsum('dgop,ox->dgpx', ci, eo, precision=hp)
    qrx = jnp.einsum('dgtp,tx->dgpx', qr, et, precision=hp)
    qix = jnp.einsum('dgtp,tx->dgpx', qi, et, precision=hp)
    ccr = crx * qrx - cix * qix
    cci = -crx * qix - cix * qrx
    cc = jnp.stack([ccr[0], cci[0], ccr[1], cci[1]], axis=1).astype(BF16)
    al = jnp.stack([pw_r[0, :, L], pw_i[0, :, L], pw_r[1, :, L], pw_i[1, :, L]])
    return mt, bs, cc, al


def _s5_toeplitz_kernel(ar_ref, ai_ref, cr_ref, ci_ref, o_ref):
    L, H = S5_CHUNK, S5_CH
    W = L * H

    def split(x):
        hi = x.astype(BF16)
        return hi, (x - hi.astype(F32)).astype(BF16)

    panels = []
    for d in range(2):
        kk = None
        for a_ref, c_ref in ((ar_ref, cr_ref), (ai_ref, ci_ref)):
            a_hi, a_lo = split(a_ref[d])
            c_hi, c_lo = split(c_ref[d])
            t = _dot_nt(a_hi, c_hi) + _dot_nt(a_hi, c_lo) + _dot_nt(a_lo, c_hi)
            kk = t if kk is None else kk - t
        order = range(L) if d == 0 else range(L - 1, -1, -1)
        panels.append(jnp.concatenate([kk[l * H:(l + 1) * H, :] for l in order], axis=1))
    pf, pr = panels
    lane = lax.broadcasted_iota(jnp.int32, (H, W), 1)
    for s in range(L):
        f = pf if s == 0 else jnp.where(lane >= s * H, pltpu.roll(pf, s * H, 1), 0.0)
        back = (L - 1 - s) * H
        r = pr if back == 0 else jnp.where(lane < (s + 1) * H, pltpu.roll(pr, W - back, 1), 0.0)
        o_ref[s * H:(s + 1) * H, :] = (f + r).astype(BF16)


def _s5_toeplitz(abr, abi, cr, ci):
    _, G, W, P = abr.shape
    H = cr.shape[2]
    a_spec = pl.BlockSpec((2, None, W, P), lambda g: (0, g, 0, 0))
    c_spec = pl.BlockSpec((2, None, H, P), lambda g: (0, g, 0, 0))
    return pl.pallas_call(
        _s5_toeplitz_kernel,
        grid=(G,),
        in_specs=[a_spec, a_spec, c_spec, c_spec],
        out_specs=pl.BlockSpec((None, W, W), lambda g: (g, 0, 0)),
        out_shape=jax.ShapeDtypeStruct((G, W, W), BF16),
        compiler_params=_cparams(("parallel",)),
        name="s5_toeplitz",
    )(abr, abi, cr, ci)


def _s5_select_tables():
    H, O = S5_CH, S5_OCT
    pack = np.zeros((O // 2, O * 128, 256), np.float32)
    unpack = np.zeros((O // 2, O * 128, 256), np.float32)
    for j in range(8):
        for g in range(O):
            for h in range(H):
                pack[g // 2, j * 128 + g * H + h, (g % 2) * 128 + j * H + h] = 1.0
                unpack[j // 2, g * 128 + j * H + h, (j % 2) * 128 + g * H + h] = 1.0
    return jnp.asarray(pack, BF16), jnp.asarray(unpack, BF16)


def _s5_kernel(ul_ref, uc_ref, pack_ref, unpack_ref, mt_ref, bs_ref, cc_ref, al_ref, ol_ref, oc_ref,
               ug_s, y_s, s_s, h_s, *, ncl, ncc):
    L, P, O = S5_CHUNK, S5_STATE, S5_OCT
    nc = ncl + ncc
    nblk = L // 8

    for tb in range(nblk):
        zs = [jnp.concatenate([ul_ref[pl.ds(8 * tb + j, ncl, stride=L), :],
                               uc_ref[pl.ds(8 * tb + j, ncc, stride=L), :]], axis=0) for j in range(8)]
        zc = jnp.concatenate(zs, axis=1).astype(BF16)
        for gp in range(O // 2):
            r = _dot(zc, pack_ref[gp])
            ug_s[2 * gp, :, tb * 128:(tb + 1) * 128] = r[:, :128]
            ug_s[2 * gp + 1, :, tb * 128:(tb + 1) * 128] = r[:, 128:]

    for g in range(O):
        ug = ug_s[g].astype(BF16)
        y_s[g] = _dot(ug, mt_ref[g])
        st = _dot(ug, bs_ref[g])
        for k in range(4):
            s_s[k, :, g, :] = st[:, k * P:(k + 1) * P]

    afr, afi, arr, ari = al_ref[0], al_ref[1], al_ref[2], al_ref[3]

    def step(j, carry):
        hfr, hfi, hrr, hri = carry
        cf = jnp.where(j < ncc, ncl + j, j - ncc)
        cr = nc - 1 - j
        h_s[0, cf] = hfr
        h_s[1, cf] = hfi
        h_s[2, cr] = hrr
        h_s[3, cr] = hri
        return (afr * hfr - afi * hfi + s_s[0, cf], afr * hfi + afi * hfr + s_s[1, cf],
                arr * hrr - ari * hri + s_s[2, cr], arr * hri + ari * hrr + s_s[3, cr])

    zero = jnp.zeros((O, P), F32)
    lax.fori_loop(0, nc, step, (zero, zero, zero, zero))

    for g in range(O):
        y = y_s[g]
        for k in range(4):
            y = y + _dot(h_s[k, :, g, :].astype(BF16), cc_ref[g, k])
        y_s[g] = y

    for tb in range(nblk):
        yc = jnp.concatenate([y_s[g, :, tb * 128:(tb + 1) * 128] for g in range(O)], axis=1)
        hi = yc.astype(BF16)
        lo = (yc - hi.astype(F32)).astype(BF16)
        for jp in range(4):
            z = _dot(hi, unpack_ref[jp]) + _dot(lo, unpack_ref[jp])
            for e in range(2):
                t = 8 * tb + 2 * jp + e
                zt = z[:, e * 128:(e + 1) * 128]
                ol_ref[pl.ds(t, ncl, stride=L), :] = zt[:ncl]
                oc_ref[pl.ds(t, ncc, stride=L), :] = zt[ncl:]


def s5_scan(pscan, tables, n_batch, t_lat, t_ctx):
    mt, bs, cc, al = tables
    pack, unpack = _s5_select_tables()
    L, P, O = S5_CHUNK, S5_STATE, S5_OCT
    W = L * S5_CH
    ncl, ncc = t_lat // L, t_ctx // L
    nc = ncl + ncc
    ctx0 = n_batch * t_lat // t_ctx
    once = pl.Buffered(1)
    return pl.pallas_call(
        functools.partial(_s5_kernel, ncl=ncl, ncc=ncc),
        grid=(S5_GROUPS // O, n_batch),
        in_specs=[
            pl.BlockSpec((t_lat, O * S5_CH), lambda o, b: (b, o)),
            pl.BlockSpec((t_ctx, O * S5_CH), lambda o, b: (ctx0 + b, o)),
            pl.BlockSpec(pack.shape, lambda o, b: (0, 0, 0), pipeline_mode=once),
            pl.BlockSpec(unpack.shape, lambda o, b: (0, 0, 0), pipeline_mode=once),
            pl.BlockSpec((O, W, W), lambda o, b: (o, 0, 0), pipeline_mode=once),
            pl.BlockSpec((O, W, 4 * P), lambda o, b: (o, 0, 0), pipeline_mode=once),
            pl.BlockSpec((O, 4, P, W), lambda o, b: (o, 0, 0, 0), pipeline_mode=once),
            pl.BlockSpec((4, O, P), lambda o, b: (0, o, 0)),
        ],
        out_specs=[
            pl.BlockSpec((t_lat, O * S5_CH), lambda o, b: (b, o)),
            pl.BlockSpec((t_ctx, O * S5_CH), lambda o, b: (b, o)),
        ],
        out_shape=[
            jax.ShapeDtypeStruct((n_batch * t_lat, GROUP_WIDTH), F32),
            jax.ShapeDtypeStruct((n_batch * t_ctx, GROUP_WIDTH), F32),
        ],
        scratch_shapes=[
            pltpu.VMEM((O, nc, W), F32),
            pltpu.VMEM((O, nc, W), F32),
            pltpu.VMEM((4, nc, O, P), F32),
            pltpu.VMEM((4, nc, O, P), F32),
        ],
        compiler_params=_cparams(("arbitrary", "arbitrary")),
        name="s5_scan",
    )(pscan, pscan, pack, unpack, mt, bs, cc, al)


def _lru_tile(x_ref, o_ref, row0, n_rows, carry, dr, w, xs_s, a_s, b_s):
    cw_ref, cb_ref, wa_ref, ba_ref, wx_ref, bx_ref, lam_ref = w
    tt = SEQ_TILE
    reverse = dr == 1
    row0 = pl.multiple_of(row0, tt)
    has_prev = (row0 > 0).astype(F32)
    has_next = (row0 + tt < n_rows).astype(F32)
    prev0 = pl.multiple_of(jnp.maximum(row0 - 8, 0), 8)
    next0 = pl.multiple_of(jnp.minimum(row0 + tt, n_rows - 8), 8)
    xs_s[dr, 0:8, :] = x_ref[pl.ds(prev0, 8), :] * has_prev
    xs_s[dr, 8:8 + tt, :] = x_ref[pl.ds(row0, tt), :]
    xs_s[dr, 8 + tt:16 + tt, :] = x_ref[pl.ds(next0, 8), :] * has_next
    xc = cb_ref[...] + jnp.zeros((tt, 128), F32)
    for tap in range(LRU_CONV):
        xc = xc + cw_ref[tap:tap + 1, :] * xs_s[dr, pl.ds(6 + tap, tt), :]
    xb = xc.astype(BF16)
    r = jax.nn.sigmoid(_dot(xb, wa_ref[dr]) + ba_ref[dr])
    i = jax.nn.sigmoid(_dot(xb, wx_ref[dr]) + bx_ref[dr])
    z = -lam_ref[dr]
    softplus = jnp.maximum(z, 0.0) + jnp.log1p(jnp.exp(-jnp.abs(z)))
    log_a = -LRU_C * r * softplus
    a_all = jnp.exp(log_a)
    th = jnp.tanh(log_a)
    b_all = jnp.sqrt(-2.0 * th / (1.0 - th)) * (i * xc)

    nseg = 8
    slen = tt // nseg
    pitch = LRU_SEG_PITCH
    for sgm in range(nseg):
        a_s[dr, sgm * pitch:sgm * pitch + slen, :] = a_all[sgm * slen:(sgm + 1) * slen, :]
        b_s[dr, sgm * pitch:sgm * pitch + slen, :] = b_all[sgm * slen:(sgm + 1) * slen, :]
    h = jnp.zeros((nseg, 128), F32)
    p = jnp.ones((nseg, 128), F32)
    for i in (range(slen - 1, -1, -1) if reverse else range(slen)):
        pos = pl.ds(i, nseg, stride=pitch)
        a = a_s[dr, pos, :]
        h = a * h + b_s[dr, pos, :]
        p = a * p
        b_s[dr, pos, :] = h
        a_s[dr, pos, :] = p
    e = carry
    ins = [None] * nseg
    for sgm in (range(nseg - 1, -1, -1) if reverse else range(nseg)):
        ins[sgm] = e
        e = h[sgm:sgm + 1, :] + p[sgm:sgm + 1, :] * e
    e_in = jnp.concatenate(ins, axis=0)
    for i in range(slen):
        pos = pl.ds(i, nseg, stride=pitch)
        b_s[dr, pos, :] = b_s[dr, pos, :] + a_s[dr, pos, :] * e_in
    for sgm in range(nseg):
        o_ref[pl.ds(row0 + sgm * slen, slen), :] = b_s[dr, sgm * pitch:sgm * pitch + slen, :]
    return e


def _lru_kernel(xl_ref, xc_ref, cw_ref, cb_ref, wa_ref, ba_ref, wx_ref, bx_ref, lam_ref,
                hfl_ref, hfc_ref, hrl_ref, hrc_ref, xs_s, a_s, b_s, *, ntl, ntc):
    w = (cw_ref, cb_ref, wa_ref, ba_ref, wx_ref, bx_ref, lam_ref)
    tt = SEQ_TILE

    def run(x_ref, of_ref, or_ref, nt, carry):
        def body(j, c):
            cf = _lru_tile(x_ref, of_ref, j * tt, nt * tt, c[0], 0, w, xs_s, a_s, b_s)
            cr = _lru_tile(x_ref, or_ref, (nt - 1 - j) * tt, nt * tt, c[1], 1, w, xs_s, a_s, b_s)
            return cf, cr
        return lax.fori_loop(0, nt, body, carry)

    zero = jnp.zeros((1, 128), F32)
    carry = run(xc_ref, hfc_ref, hrc_ref, ntc, (zero, zero))
    run(xl_ref, hfl_ref, hrl_ref, ntl, carry)


def lru_scan(pscan, conv_w, conv_b, w_a, b_a, w_x, b_x, lam, n_batch, t_lat, t_ctx):
    tt = SEQ_TILE
    gw = GROUP_WIDTH
    nh = gw // 128
    ntl, ntc = t_lat // tt, t_ctx // tt
    ctx0 = n_batch * t_lat // t_ctx
    per_half = LRU_BLOCKS // nh

    def halves(w):
        return jnp.stack([jnp.stack([_block_diag(w[dr, k * per_half:(k + 1) * per_half]) for k in range(nh)])
                          for dr in range(2)]).astype(BF16)

    vec = lambda a: a.reshape(2, 1, gw)
    vspec = pl.BlockSpec((2, 1, 128), lambda b, k: (0, 0, k))
    mspec = pl.BlockSpec((2, None, 128, 128), lambda b, k: (0, k, 0, 0))
    lat = pl.BlockSpec((t_lat, 128), lambda b, k: (b, k))
    ctx = pl.BlockSpec((t_ctx, 128), lambda b, k: (b, k))
    lat_shape = jax.ShapeDtypeStruct((n_batch * t_lat, gw), F32)
    ctx_shape = jax.ShapeDtypeStruct((n_batch * t_ctx, gw), F32)
    hfl, hfc, hrl, hrc = pl.pallas_call(
        functools.partial(_lru_kernel, ntl=ntl, ntc=ntc),
        grid=(n_batch, nh),
        in_specs=[
            pl.BlockSpec((t_lat, 128), lambda b, k: (b, nh + k)),
            pl.BlockSpec((t_ctx, 128), lambda b, k: (ctx0 + b, nh + k)),
            pl.BlockSpec((LRU_CONV, 128), lambda b, k: (0, k)),
            pl.BlockSpec((1, 128), lambda b, k: (0, k)),
            mspec, vspec, mspec, vspec, vspec,
        ],
        out_specs=[lat, ctx, lat, ctx],
        out_shape=[lat_shape, ctx_shape, lat_shape, ctx_shape],
        scratch_shapes=[
            pltpu.VMEM((2, tt + 16, 128), F32),
            pltpu.VMEM((2, 8 * LRU_SEG_PITCH, 128), F32),
            pltpu.VMEM((2, 8 * LRU_SEG_PITCH, 128), F32),
        ],
        compiler_params=_cparams(("parallel", "parallel")),
        name="lru_scan",
    )(pscan, pscan, conv_w, conv_b.reshape(1, gw), halves(w_a), vec(b_a), halves(w_x), vec(b_x), vec(lam))
    return [hfl, hfc], [hrl, hrc]


def _block_diag(w):
    n, d, e = w.shape
    eye = jnp.eye(n, dtype=w.dtype)
    return (eye[:, None, :, None] * w[:, :, None, :]).reshape(n * d, n * e)


def _natten_bias(rpb):
    W, KR, KC, QR, NB = GRID_W, NA_KR, NA_KC, NA_QROWS, NA_BAND
    col = np.arange(W)
    cs = np.clip(col - KC // 2, 0, W - KC)
    inwin = (col[None, :] >= cs[:, None]) & (col[None, :] < cs[:, None] + KC)
    coff = np.clip(col[None, :] - col[:, None] + (KC - 1), 0, 2 * KC - 2)
    a = np.arange(QR)[:, None]
    i = np.arange(NB)[None, :]
    first = (i - a, (i < KR) & (a >= 0))
    mid = (i - a - KR // 2, (i >= a) & (i < a + KR))
    last = (i - a - NB + QR, (i >= NB - KR) & (a >= 0))
    coh = (coff[:, :, None] == np.arange(2 * KC - 1)).astype(np.float32)
    t = jnp.einsum('hrc,qkc->hrqk', rpb.astype(F32), coh, precision=lax.Precision.HIGHEST)
    t = t * LOG2E + np.where(inwin, 0.0, NEG_BIG).astype(np.float32)
    plan = tuple(
        tuple(tuple(int(np.clip(delta[ai, ii] + KR - 1, 0, 2 * KR - 2)) if valid[ai, ii] else -1
                    for ii in range(NB)) for ai in range(QR))
        for delta, valid in (first, mid, last))

    def expand(t_ref, o_ref):
        for v in range(3):
            for ai in range(QR):
                for ii in range(NB):
                    r = plan[v][ai][ii]
                    blk = t_ref[r] if r >= 0 else jnp.full((W, W), NEG_BIG, F32)
                    o_ref[v, ai * W:(ai + 1) * W, ii * W:(ii + 1) * W] = blk

    nh = rpb.shape[0]
    return pl.pallas_call(
        expand,
        grid=(nh,),
        in_specs=[pl.BlockSpec((None, 2 * KR - 1, W, W), lambda h: (h, 0, 0, 0))],
        out_specs=pl.BlockSpec((3, None, QR * W, NB * W), lambda h: (0, h, 0, 0)),
        out_shape=jax.ShapeDtypeStruct((3, nh, QR * W, NB * W), F32),
        compiler_params=_cparams(("parallel",)),
        name="natten_bias",
    )(t)


def _natten_kernel(q_ref, k_ref, v_ref, kc_ref, vc_ref, bias_ref, o_ref, *, rows):
    W, KR, hd = GRID_W, NA_KR, HEAD_DIM
    nblk = rows // NA_QROWS
    nb = NA_BAND * W
    tq = NA_QROWS * W
    ones_col = (lax.broadcasted_iota(jnp.int32, (nb + kc_ref.shape[0], hd), 1) == 0).astype(BF16)
    for u in range(NA_STEP_BLOCKS):
        blk = pl.program_id(1) * NA_STEP_BLOCKS + u
        bs = jnp.clip(blk * NA_QROWS - KR // 2, 0, rows - NA_BAND)
        var = jnp.where(blk == 0, 0, jnp.where(blk == nblk - 1, 2, 1))
        start = pl.multiple_of(bs * W, W)
        q = (q_ref[u * tq:(u + 1) * tq, :].astype(F32) * ((hd ** -0.5) * LOG2E)).astype(BF16)
        k_all = jnp.concatenate([k_ref[pl.ds(start, nb), :], kc_ref[...]], axis=0)
        v_all = jnp.concatenate([v_ref[pl.ds(start, nb), :], vc_ref[...]], axis=0)
        for h in range(NA_HEADS):
            sl = slice(h * hd, (h + 1) * hd)
            s = _dot_nt(q[:, sl], k_all[:, sl])
            sb = s[:, :nb] + bias_ref[var, h]
            sc = s[:, nb:]
            m = jnp.maximum(jnp.max(sb, axis=-1, keepdims=True), jnp.max(sc, axis=-1, keepdims=True))
            p = jnp.concatenate([jnp.exp2(sb - m), jnp.exp2(sc - m)], axis=1).astype(BF16)
            o = _dot(p, jnp.concatenate([v_all[:, sl], ones_col], axis=1))
            o_ref[u * tq:(u + 1) * tq, sl] = o[:, :hd] / o[:, hd:hd + 1]


def natten_latent(pattn, bias, n_batch, t_lat, t_ctx):
    gw = GROUP_WIDTH
    rows = t_lat // GRID_W
    assert NA_QROWS == NA_KR // 2 and NA_BAND == NA_KR + NA_QROWS and rows % NA_QROWS == 0 and rows >= NA_BAND
    assert (rows // NA_QROWS) % NA_STEP_BLOCKS == 0
    nblk = rows // NA_QROWS // NA_STEP_BLOCKS
    tq = NA_STEP_BLOCKS * NA_QROWS * GRID_W
    ctx0 = n_batch * t_lat // t_ctx
    return pl.pallas_call(
        functools.partial(_natten_kernel, rows=rows),
        grid=(n_batch, nblk),
        in_specs=[
            pl.BlockSpec((tq, gw), lambda b, r: (b * nblk + r, 2)),
            pl.BlockSpec((t_lat, gw), lambda b, r: (b, 3)),
            pl.BlockSpec((t_lat, gw), lambda b, r: (b, 4)),
            pl.BlockSpec((t_ctx, gw), lambda b, r: (ctx0 + b, 3)),
            pl.BlockSpec((t_ctx, gw), lambda b, r: (ctx0 + b, 4)),
            pl.BlockSpec(bias.shape, lambda b, r: (0, 0, 0, 0)),
        ],
        out_specs=pl.BlockSpec((tq, gw), lambda b, r: (b * nblk + r, 0)),
        out_shape=jax.ShapeDtypeStruct((n_batch * t_lat, gw), F32),
        compiler_params=_cparams(("parallel", "arbitrary")),
        name="natten_latent",
    )(pattn, pattn, pattn, pattn, pattn, bias)


def _ctx_attn_kernel(q_ref, k_ref, v_ref, o_ref):
    hd = HEAD_DIM
    q, k, v = q_ref[...], k_ref[...], v_ref[...]
    for h in range(NA_HEADS):
        sl = slice(h * hd, (h + 1) * hd)
        s = _dot_nt(q[:, sl], k[:, sl]) * (hd ** -0.5)
        m = jnp.max(s, axis=-1, keepdims=True)
        p = jnp.exp(s - m)
        l = jnp.sum(p, axis=-1, keepdims=True)
        o_ref[:, sl] = _dot(p.astype(BF16), v[:, sl]) / l


def natten_context(pattn, n_batch, t_lat, t_ctx):
    gw = GROUP_WIDTH
    ctx0 = n_batch * t_lat // t_ctx
    return pl.pallas_call(
        _ctx_attn_kernel,
        grid=(n_batch,),
        in_specs=[pl.BlockSpec((t_ctx, gw), lambda b, c=c: (ctx0 + b, c)) for c in (2, 3, 4)],
        out_specs=pl.BlockSpec((t_ctx, gw), lambda b: (b, 0)),
        out_shape=jax.ShapeDtypeStruct((n_batch * t_ctx, gw), F32),
        compiler_params=_cparams(("parallel",)),
        name="natten_context",
    )(pattn, pattn, pattn)


def _rope_tables(t_lat, t_ctx):
    half = HEAD_DIM // 4
    freqs = ROPE_BASE ** (-np.arange(half, dtype=np.float64) / half)
    t = np.arange(t_lat)
    ang_r = (t // GRID_W)[:, None] * freqs[None, :]
    ang_c = (t % GRID_W)[:, None] * freqs[None, :]
    ang = np.concatenate([ang_r, ang_r, ang_c, ang_c], axis=1)
    cos = np.concatenate([np.cos(ang), np.ones((t_ctx, HEAD_DIM))], axis=0)
    sin = np.concatenate([np.sin(ang), np.zeros((t_ctx, HEAD_DIM))], axis=0)
    return (np.tile(cos, (1, GQA_Q_HEADS)).astype(np.float32), np.tile(sin, (1, GQA_Q_HEADS)).astype(np.float32))


def _gqa_prep_kernel(x_ref, cos_ref, sin_ref, gq_ref, gk_ref, gm_ref, q_ref, k_ref, v_ref):
    hd = HEAD_DIM
    nq, nk = GQA_Q_HEADS * hd, GQA_KV_HEADS * hd
    x = x_ref[...].astype(F32)
    cos, sin = cos_ref[...], sin_ref[...]
    gm = gm_ref[...]

    def norm_rope(t, gain, width):
        ms = _dot((t * t).astype(BF16), gm[:width, :width])
        tn = t * lax.rsqrt(ms + EPS) * gain
        lane = lax.broadcasted_iota(jnp.int32, tn.shape, 1)
        first = (lane % (hd // 2)) < (hd // 4)
        rot = jnp.where(first, -pltpu.roll(tn, width - hd // 4, 1), pltpu.roll(tn, hd // 4, 1))
        return tn * cos[:, :width] + rot * sin[:, :width]

    q = norm_rope(x[:, :nq], gq_ref[...], nq) * ((hd ** -0.5) * LOG2E)
    kt = norm_rope(x[:, nq:nq + nk], gk_ref[...], nk).T
    v = x_ref[:, nq + nk:nq + 2 * nk]
    for h in range(GQA_Q_HEADS):
        q_ref[h] = q[:, h * hd:(h + 1) * hd].astype(BF16)
    ones_col = (lax.broadcasted_iota(jnp.int32, (v.shape[0], hd), 1) == 0).astype(BF16)
    for h in range(GQA_KV_HEADS):
        k_ref[h] = kt[h * hd:(h + 1) * hd, :].astype(BF16)
        v_ref[h] = jnp.concatenate([v[:, h * hd:(h + 1) * hd], ones_col], axis=1)


def gqa_prep(pattn, gq, gk, n_batch, t_lat, t_ctx):
    tt = SEQ_TILE
    assert tt == KEY_CHUNK
    hd = HEAD_DIM
    ntl, ntc = t_lat // tt, t_ctx // tt
    s_len = t_lat + t_ctx
    cos, sin = _rope_tables(t_lat, t_ctx)
    nq = GQA_Q_HEADS * hd
    gmat = _block_diag(jnp.full((GQA_Q_HEADS, hd, hd), 1.0 / hd, F32)).astype(BF16)

    def pos(i):
        is_ctx = i >= n_batch * ntl
        c = i - n_batch * ntl
        return jnp.where(is_ctx, ntl + c % ntc, i % ntl)

    def bat(i):
        is_ctx = i >= n_batch * ntl
        return jnp.where(is_ctx, (i - n_batch * ntl) // ntc, i // ntl)

    return pl.pallas_call(
        _gqa_prep_kernel,
        grid=(n_batch * (ntl + ntc),),
        in_specs=[
            pl.BlockSpec((tt, 2 * nq), lambda i: (i, 0)),
            pl.BlockSpec((tt, nq), lambda i: (pos(i), 0)),
            pl.BlockSpec((tt, nq), lambda i: (pos(i), 0)),
            pl.BlockSpec((1, nq), lambda i: (0, 0)),
            pl.BlockSpec((1, nq // 2), lambda i: (0, 0)),
            pl.BlockSpec((nq, nq), lambda i: (0, 0)),
        ],
        out_specs=[
            pl.BlockSpec((None, GQA_Q_HEADS, tt, hd), lambda i: (bat(i), 0, pos(i), 0)),
            pl.BlockSpec((None, GQA_KV_HEADS, None, hd, tt), lambda i: (bat(i), 0, pos(i), 0, 0)),
            pl.BlockSpec((None, GQA_KV_HEADS, tt, 2 * hd), lambda i: (bat(i), 0, pos(i), 0)),
        ],
        out_shape=[
            jax.ShapeDtypeStruct((n_batch, GQA_Q_HEADS, s_len, hd), BF16),
            jax.ShapeDtypeStruct((n_batch, GQA_KV_HEADS, s_len // tt, hd, tt), BF16),
            jax.ShapeDtypeStruct((n_batch, GQA_KV_HEADS, s_len, 2 * hd), BF16),
        ],
        compiler_params=_cparams(("parallel",)),
        name="gqa_prep",
    )(pattn, cos, sin, jnp.tile(gq.astype(F32), GQA_Q_HEADS).reshape(1, nq),
      jnp.tile(gk.astype(F32), GQA_KV_HEADS).reshape(1, nq // 2), gmat)


def _gqa_kernel(q_ref, kt_ref, v_ref, o_ref, s_s, m_s, acc_s, *, unroll):
    g, tq, hd = q_ref.shape
    nch, _, ck = kt_ref.shape
    nl = ck // 128
    q = q_ref[...].reshape(g * tq, hd)
    m_s[...] = jnp.full_like(m_s, -jnp.inf)

    def scores(c, carry):
        s = _dot(q, kt_ref[c])
        s_s[c] = s
        fold = s[:, 0:128]
        for j in range(1, nl):
            fold = jnp.maximum(fold, s[:, j * 128:(j + 1) * 128])
        m_s[...] = jnp.maximum(m_s[...], fold)
        return carry

    lax.fori_loop(0, nch, scores, 0, unroll=unroll)
    m_s[...] = jnp.broadcast_to(jnp.max(m_s[...], axis=-1, keepdims=True), m_s.shape)
    acc_s[...] = jnp.zeros_like(acc_s)

    def values(grp, carry):
        m = m_s[...]
        ps = []
        for u in range(unroll):
            s = s_s[grp * unroll + u]
            ps += [jnp.exp2(s[:, j * 128:(j + 1) * 128] - m).astype(BF16) for j in range(nl)]
        off = pl.multiple_of(grp * (unroll * ck), unroll * ck)
        acc_s[...] += _dot(jnp.concatenate(ps, axis=1), v_ref[pl.ds(off, unroll * ck), :])
        return carry

    lax.fori_loop(0, nch // unroll, values, 0)
    acc = acc_s[...]
    o = acc[:, :hd] / acc[:, hd:hd + 1]
    o_ref[...] = jnp.concatenate([o[i * tq:(i + 1) * tq] for i in range(g)], axis=1)


def gqa_attention(q, kt, v, tq, q_blk0, nq, c_blk, nch):
    n_batch = q.shape[0]
    hd = HEAD_DIM
    g = GQA_GROUP
    ck = KEY_CHUNK
    unroll = next(u for u in (11, 3, 1) if nch % u == 0)
    return pl.pallas_call(
        functools.partial(_gqa_kernel, unroll=unroll),
        grid=(n_batch, GQA_KV_HEADS, nq),
        in_specs=[
            pl.BlockSpec((None, g, tq, hd), lambda b, h, i: (b, h, q_blk0 + i, 0)),
            pl.BlockSpec((None, None, nch, hd, ck), lambda b, h, i: (b, h, c_blk, 0, 0)),
            pl.BlockSpec((None, None, nch * ck, 2 * hd), lambda b, h, i: (b, h, c_blk, 0)),
        ],
        out_specs=pl.BlockSpec((tq, g * hd), lambda b, h, i: (b * nq + i, h)),
        out_shape=jax.ShapeDtypeStruct((n_batch * nq * tq, GQA_Q_HEADS * hd), F32),
        scratch_shapes=[
            pltpu.VMEM((nch, g * tq, ck), F32),
            pltpu.VMEM((g * tq, 128), F32),
            pltpu.VMEM((g * tq, 2 * hd), F32),
        ],
        compiler_params=_cparams(("parallel", "parallel", "arbitrary")),
        name="gqa_attention",
    )(q, kt, v)


def kernel(x, c, ctx, c_ctx, w_ada, b_ada, g_ffn1, w_ffn1_in, w_ffn1_out, g_mix, w_in, w_out, s5_lambda_re, s5_lambda_im, s5_log_step, s5_b_re, s5_b_im, s5_c_re, s5_c_im, s5_d, s5_w_glu, na_rpb, gqa_q_norm, gqa_k_norm, lru_conv_w, lru_conv_b, lru_w_a, lru_b_a, lru_w_x, lru_b_x, lru_lambda, g_ffn2, w_ffn2_in, w_ffn2_out, g_final):
    n_batch, t_lat, d = x.shape
    t_ctx = ctx.shape[1]
    depth = w_ada.shape[0]
    assert d == D_MODEL and t_lat % TOKEN_TILE == 0 and (n_batch * t_ctx) % TOKEN_TILE == 0
    assert t_lat % t_ctx == 0 and t_ctx % SEQ_TILE == 0 and t_lat % FLASH_TQ == 0
    n_lat = n_batch * t_lat
    n_all = n_lat + n_batch * t_ctx
    tiles_per_type = t_lat // TOKEN_TILE
    assert (n_batch * t_ctx) // TOKEN_TILE <= tiles_per_type

    c8 = jnp.zeros((8, D_MODEL), F32).at[:n_batch].set(c.astype(F32)).at[n_batch].set(c_ctx.astype(F32))
    mods_all = ada_mods(c8, w_ada, b_ada).reshape(depth, 8, N_MOD, D_MODEL)

    s_len = t_lat + t_ctx
    n_lat_tiles = n_lat // TOKEN_TILE
    n_ctx_tiles = (n_all - n_lat) // TOKEN_TILE
    hs = [x.reshape(n_lat, D_MODEL), ctx.reshape(n_batch * t_ctx, D_MODEL)]
    w1i, w1o = w_ffn1_in.astype(BF16), w_ffn1_out.astype(BF16)
    w2i, w2o = w_ffn2_in.astype(BF16), w_ffn2_out.astype(BF16)
    w_in_b, w_out_b, w_glu_b = w_in.astype(BF16), w_out.astype(BF16), s5_w_glu.astype(BF16)

    for l in range(depth):
        need_ctx = l < depth - 1
        mods = mods_all[l]
        h, pscan, pattn = ffn_in_projection(hs, mods, g_ffn1[l], w1i, w1o, g_mix[l], w_in_b, l,
                                            n_lat_tiles, n_ctx_tiles, tiles_per_type)

        tables = _s5_tables(s5_lambda_re[l], s5_lambda_im[l], s5_log_step[l], s5_b_re[l], s5_b_im[l],
                            s5_c_re[l], s5_c_im[l])
        ys5 = list(s5_scan(pscan, tables, n_batch, t_lat, t_ctx))

        hf, hr = lru_scan(pscan, lru_conv_w[l], lru_conv_b[l], lru_w_a[l], lru_b_a[l], lru_w_x[l], lru_b_x[l],
                          lru_lambda[l], n_batch, t_lat, t_ctx)

        yb = [natten_latent(pattn, _natten_bias(na_rpb[l]), n_batch, t_lat, t_ctx)]

        qh, kth, vh = gqa_prep(pattn, gqa_q_norm[l], gqa_k_norm[l], n_batch, t_lat, t_ctx)
        yc = [gqa_attention(qh, kth, vh, FLASH_TQ, 0, t_lat // FLASH_TQ, 0, s_len // KEY_CHUNK)]

        if need_ctx:
            yb.append(natten_context(pattn, n_batch, t_lat, t_ctx))
            yc.append(gqa_attention(qh, kth, vh, t_ctx, t_lat // t_ctx, 1, t_lat // t_ctx, t_ctx // KEY_CHUNK))
        else:
            ys5, hf, hr = ys5[:1], hf[:1], hr[:1]

        h = mix_out_ffn(h, mods, ys5, yb, yc, pscan, hf, hr, s5_d[l], w_glu_b, w_out_b, g_ffn2[l], w2i, w2o, l,
                        n_lat_tiles,
                        n_ctx_tiles if need_ctx else 0, tiles_per_type, g_final=None if need_ctx else g_final)
        hs = [h]
    return h.reshape(n_batch, t_lat, D_MODEL)
```

```python
import functools
import math

import numpy as np
import jax
import jax.numpy as jnp
from jax import lax
from jax.experimental import pallas as pl
from jax.experimental.pallas import tpu as pltpu

F32 = jnp.float32
BF16 = jnp.bfloat16

D_MODEL = 1024
GRID_W = 64
HEAD_DIM = 64
GROUP_WIDTH = D_MODEL // 4
S5_CH = 16
S5_GROUPS = GROUP_WIDTH // S5_CH
S5_STATE = 64
NA_HEADS = GROUP_WIDTH // HEAD_DIM
NA_KR = 8
NA_KC = 16
GQA_Q_HEADS = GROUP_WIDTH // HEAD_DIM
GQA_KV_HEADS = GQA_Q_HEADS // 2
GQA_GROUP = GQA_Q_HEADS // GQA_KV_HEADS
LRU_BLOCKS = GROUP_WIDTH // HEAD_DIM
LRU_CONV = 4
LRU_C = 8.0
D_FF = ((8 * D_MODEL // 3 + 127) // 128) * 128
FFN_RES = 0.5
ROPE_BASE = 10000.0
EPS = 1e-6
N_MOD = 9

LANES = 128
TOKEN_TILE = 512
SEQ_TILE = 256
LRU_SEG_PITCH = SEQ_TILE // 8 + 4
S5_CHUNK = 32
S5_OCT = 8
MXU_DEPTH = 256
FF_CHUNKS = ((0, 6 * MXU_DEPTH), (6 * MXU_DEPTH, D_FF))
FLASH_TQ = 512
KEY_CHUNK = 256
NA_QROWS = NA_KR // 2
NA_BAND = NA_KR + NA_QROWS
NA_STEP_BLOCKS = 4
LOG2E = math.log2(math.e)
NEG_BIG = -1e30
V7X_VMEM_BYTES = 64 * 1024 * 1024
VMEM_LIMIT = V7X_VMEM_BYTES - 8 * 1024 * 1024


def _cparams(sem):
    return pltpu.CompilerParams(dimension_semantics=sem, vmem_limit_bytes=VMEM_LIMIT)


def _dot(a, b):
    return jnp.dot(a, b, preferred_element_type=F32)


def _dot_nt(a, b):
    return lax.dot_general(a, b, (((1,), (1,)), ((), ())), preferred_element_type=F32)


def _ada_kernel(c_ref, w_ref, b_ref, o_ref):
    c = c_ref[...]
    s = c * jax.nn.sigmoid(c)
    o_ref[...] = _dot(s.astype(BF16), w_ref[...].astype(BF16)) + b_ref[...]


def ada_mods(c8, w_ada, b_ada):
    L = w_ada.shape[0]
    tn = D_MODEL
    return pl.pallas_call(
        _ada_kernel,
        grid=(L, N_MOD * D_MODEL // tn),
        in_specs=[
            pl.BlockSpec((8, D_MODEL), lambda l, j: (0, 0)),
            pl.BlockSpec((None, D_MODEL, tn), lambda l, j: (l, 0, j)),
            pl.BlockSpec((None, 1, tn), lambda l, j: (l, 0, j)),
        ],
        out_specs=pl.BlockSpec((None, 8, tn), lambda l, j: (l, 0, j)),
        out_shape=jax.ShapeDtypeStruct((L, 8, N_MOD * D_MODEL), F32),
        compiler_params=_cparams(("parallel", "parallel")),
        name="ada_mods",
    )(c8, w_ada, b_ada.reshape(L, 1, N_MOD * D_MODEL))


def _norm_mod(x, g, shift, scale):
    ms = jnp.mean(x * x, axis=-1, keepdims=True)
    y = x * lax.rsqrt(ms + EPS) * g
    return y * (1.0 + scale) + shift


def _ffn_core(x, mod_ref, g, wi_ref, wo_ref, k0):
    y = _norm_mod(x, g, mod_ref[k0:k0 + 1, :], mod_ref[k0 + 1:k0 + 2, :]).astype(BF16)
    acc = jnp.zeros(x.shape, F32)
    for lo, hi in FF_CHUNKS:
        a = _dot(y, wi_ref[:, lo:hi])
        b = _dot(y, wi_ref[:, D_FF + lo:D_FF + hi])
        h = (a * jax.nn.sigmoid(a) * b).astype(BF16)
        acc = acc + _dot(h, wo_ref[lo:hi, :])
    return x + FFN_RES * mod_ref[k0 + 2:k0 + 3, :] * acc


def _resident(shape, layer):
    return pl.BlockSpec((None,) + shape, lambda i: (layer,) + (0,) * len(shape), pipeline_mode=pl.Buffered(1))


def _row_specs(n_lat_tiles, n_ctx_tiles, width, col=0):
    lat = pl.BlockSpec((TOKEN_TILE, width), lambda i: (jnp.minimum(i, n_lat_tiles - 1), col))
    if n_ctx_tiles == 0:
        return [lat]
    ctx = pl.BlockSpec((TOKEN_TILE, width), lambda i: (jnp.clip(i - n_lat_tiles, 0, n_ctx_tiles - 1), col))
    return [lat, ctx]


def _pick(refs, n_lat_tiles):
    if len(refs) == 1:
        return refs[0][...]
    return jnp.where(pl.program_id(0) >= n_lat_tiles, refs[1][...], refs[0][...])


def _ffn_in_kernel(*refs, n_src, n_lat_tiles):
    x_refs = refs[:n_src]
    mod_ref, g_ref, wi_ref, wo_ref, gm_ref, win_ref, h_ref, ps_ref, pa_ref = refs[n_src:]
    h = _ffn_core(_pick(x_refs, n_lat_tiles), mod_ref, g_ref[...], wi_ref, wo_ref, 0)
    h_ref[...] = h
    y = _norm_mod(h, gm_ref[...], mod_ref[3:4, :], mod_ref[4:5, :]).astype(BF16)
    p = _dot(y, win_ref[...])
    gw = GROUP_WIDTH
    ps_ref[...] = jnp.concatenate([p[:, 0:gw], p[:, 6 * gw:8 * gw]], axis=1)
    pa_ref[...] = jnp.concatenate([p[:, 4 * gw:6 * gw], p[:, gw:4 * gw]], axis=1).astype(BF16)


def ffn_in_projection(xs, mods, g, w_i, w_o, g_mix, w_in_l, layer, n_lat_tiles, n_ctx_tiles, tiles_per_type):
    nt = n_lat_tiles + n_ctx_tiles
    n = nt * TOKEN_TILE
    n_scan = 3 * GROUP_WIDTH
    n_attn = w_in_l.shape[-1] - n_scan
    x_specs = (_row_specs(n_lat_tiles, n_ctx_tiles, D_MODEL) if len(xs) == 2
               else [pl.BlockSpec((TOKEN_TILE, D_MODEL), lambda i: (i, 0))])
    vec = pl.BlockSpec((1, D_MODEL), lambda i: (0, 0))
    return pl.pallas_call(
        functools.partial(_ffn_in_kernel, n_src=len(xs), n_lat_tiles=n_lat_tiles),
        grid=(nt,),
        in_specs=x_specs + [
            pl.BlockSpec((None, N_MOD, D_MODEL), lambda i: (i // tiles_per_type, 0, 0)),
            vec, _resident((D_MODEL, 2 * D_FF), layer), _resident((D_FF, D_MODEL), layer),
            vec, _resident((D_MODEL, n_scan + n_attn), layer),
        ],
        out_specs=[
            pl.BlockSpec((TOKEN_TILE, D_MODEL), lambda i: (i, 0)),
            pl.BlockSpec((TOKEN_TILE, n_scan), lambda i: (i, 0)),
            pl.BlockSpec((TOKEN_TILE, n_attn), lambda i: (i, 0)),
        ],
        out_shape=[
            jax.ShapeDtypeStruct((n, D_MODEL), F32),
            jax.ShapeDtypeStruct((n, n_scan), F32),
            jax.ShapeDtypeStruct((n, n_attn), BF16),
        ],
        compiler_params=_cparams(("parallel",)),
        name="ffn_in_projection",
    )(*xs, mods, g.reshape(1, D_MODEL), w_i, w_o, g_mix.reshape(1, D_MODEL), w_in_l)


def _mix_ffn_kernel(*refs, n_src, n_lat_tiles, final):
    gw = GROUP_WIDTH
    h_ref, mod_ref = refs[:2]
    ys5_refs, yb_refs, yc_refs, hf_refs, hr_refs = (refs[2 + k * n_src:2 + (k + 1) * n_src] for k in range(5))
    rest = refs[2 + 5 * n_src:]
    u_ref, gl_ref, dsk_ref, wglu_ref, wout_ref, g_ref, wi_ref, wo_ref = rest[:8]
    o_ref = rest[-1]
    ya = jax.nn.gelu(_pick(ys5_refs, n_lat_tiles) + dsk_ref[...] * u_ref[...])
    ya = ya * jax.nn.sigmoid(_dot(ya.astype(BF16), wglu_ref[...]))
    yd = (_pick(hf_refs, n_lat_tiles) + _pick(hr_refs, n_lat_tiles)) * jax.nn.gelu(gl_ref[...])
    acc = _dot(ya.astype(BF16), wout_ref[0:gw, :])
    acc = acc + _dot(_pick(yb_refs, n_lat_tiles).astype(BF16), wout_ref[gw:2 * gw, :])
    acc = acc + _dot(_pick(yc_refs, n_lat_tiles).astype(BF16), wout_ref[2 * gw:3 * gw, :])
    acc = acc + _dot(yd.astype(BF16), wout_ref[3 * gw:4 * gw, :])
    h = h_ref[...] + mod_ref[5:6, :] * acc
    out = _ffn_core(h, mod_ref, g_ref[...], wi_ref, wo_ref, 6)
    if final:
        ms = jnp.mean(out * out, axis=-1, keepdims=True)
        out = out * lax.rsqrt(ms + EPS) * rest[8][...]
    o_ref[...] = out


def mix_out_ffn(h, mods, ys5, yb, yc, pscan, hf, hr, d_skip, w_glu, w_out, g, w_i, w_o, layer,
                n_lat_tiles, n_ctx_tiles, tiles_per_type, g_final=None):
    gw = GROUP_WIDTH
    final = g_final is not None
    nt = n_lat_tiles + n_ctx_tiles
    row = lambda c: pl.BlockSpec((TOKEN_TILE, gw), lambda i: (i, c))
    vec = pl.BlockSpec((1, D_MODEL), lambda i: (0, 0))
    mixer_specs = _row_specs(n_lat_tiles, n_ctx_tiles, gw)
    in_specs = ([pl.BlockSpec((TOKEN_TILE, D_MODEL), lambda i: (i, 0)),
                 pl.BlockSpec((None, N_MOD, D_MODEL), lambda i: (i // tiles_per_type, 0, 0))]
                + mixer_specs * 5
                + [row(0), row(2),
                   pl.BlockSpec((1, gw), lambda i: (0, 0)), _resident((gw, gw), layer),
                   _resident((D_MODEL, D_MODEL), layer),
                   vec, _resident((D_MODEL, 2 * D_FF), layer), _resident((D_FF, D_MODEL), layer)])
    args = [h, mods, *ys5, *yb, *yc, *hf, *hr, pscan, pscan, d_skip.reshape(1, gw), w_glu, w_out,
            g.reshape(1, D_MODEL), w_i, w_o]
    if final:
        in_specs.append(vec)
        args.append(g_final.reshape(1, D_MODEL))
    return pl.pallas_call(
        functools.partial(_mix_ffn_kernel, n_src=len(ys5), n_lat_tiles=n_lat_tiles, final=final),
        grid=(nt,),
        in_specs=in_specs,
        out_specs=pl.BlockSpec((TOKEN_TILE, D_MODEL), lambda i: (i, 0)),
        out_shape=jax.ShapeDtypeStruct((nt * TOKEN_TILE, D_MODEL), F32),
        compiler_params=_cparams(("parallel",)),
        name="mix_out_ffn",
    )(*args)


def _s5_tables(lam_re, lam_im, log_step, b_re, b_im, c_re, c_im):
    L = S5_CHUNK
    G, P, H = S5_GROUPS, S5_STATE, S5_CH
    hp = lax.Precision.HIGHEST
    W = L * H
    lr, li = lam_re.astype(F32), lam_im.astype(F32)
    dt = jnp.exp(log_step.astype(F32))[..., None]
    mag = jnp.exp(lr * dt)
    ar, ai = mag * jnp.cos(li * dt), mag * jnp.sin(li * dt)
    den = lr * lr + li * li
    fr = ((ar - 1) * lr + ai * li) / den
    fi = (ai * lr - (ar - 1) * li) / den
    br, bi = b_re.astype(F32), b_im.astype(F32)
    bbr = jnp.swapaxes(fr[..., None] * br - fi[..., None] * bi, 2, 3)
    bbi = jnp.swapaxes(fr[..., None] * bi + fi[..., None] * br, 2, 3)
    cr, ci = c_re.astype(F32), c_im.astype(F32)
    tau = jnp.arange(L + 1, dtype=F32)[None, None, :, None]
    pmag = jnp.exp(tau * (lr * dt)[:, :, None, :])
    pang = tau * (li * dt)[:, :, None, :]
    pw_r, pw_i = pmag * jnp.cos(pang), pmag * jnp.sin(pang)
    abr = pw_r[:, :, :L, None, :] * bbr[:, :, None] - pw_i[:, :, :L, None, :] * bbi[:, :, None]
    abi = pw_r[:, :, :L, None, :] * bbi[:, :, None] + pw_i[:, :, :L, None, :] * bbr[:, :, None]
    mt = _s5_toeplitz(abr.reshape(2, G, W, P), abi.reshape(2, G, W, P), cr, ci)

    def rows(x, flip):
        return (x[:, ::-1] if flip else x).reshape(G, W, P)

    bs = jnp.concatenate([rows(abr[0], True), rows(abi[0], True),
                          rows(abr[1], False), rows(abi[1], False)], axis=-1).astype(BF16)
    eo = np.tile(np.eye(H, dtype=np.float32), (1, L))
    et = np.repeat(np.eye(L, dtype=np.float32), H, axis=1)
    qr = jnp.stack([pw_r[0, :, 1:], pw_r[1, :, :0:-1]])
    qi = jnp.stack([pw_i[0, :, 1:], pw_i[1, :, :0:-1]])
    crx = jnp.einsum('dgop,ox->dgpx', cr, eo, precision=hp)
    cix = jnp.einsum('dgop,ox->dgpx', ci, eo, precision=hp)
    qrx = jnp.einsum('dgtp,tx->dgpx', qr, et, precision=hp)
    qix = jnp.einsum('dgtp,tx->dgpx', qi, et, precision=hp)
    ccr = crx * qrx - cix * qix
    cci = -crx * qix - cix * qrx
    cc = jnp.stack([ccr[0], cci[0], ccr[1], cci[1]], axis=1).astype(BF16)
    al = jnp.stack([pw_r[0, :, L], pw_i[0, :, L], pw_r[1, :, L], pw_i[1, :, L]])
    return mt, bs, cc, al


def _s5_toeplitz_kernel(ar_ref, ai_ref, cr_ref, ci_ref, o_ref):
    L, H = S5_CHUNK, S5_CH
    W = L * H

    def split(x):
        hi = x.astype(BF16)
        return hi, (x - hi.astype(F32)).astype(BF16)

    panels = []
    for d in range(2):
        kk = None
        for a_ref, c_ref in ((ar_ref, cr_ref), (ai_ref, ci_ref)):
            a_hi, a_lo = split(a_ref[d])
            c_hi, c_lo = split(c_ref[d])
            t = _dot_nt(a_hi, c_hi) + _dot_nt(a_hi, c_lo) + _dot_nt(a_lo, c_hi)
            kk = t if kk is None else kk - t
        order = range(L) if d == 0 else range(L - 1, -1, -1)
        panels.append(jnp.concatenate([kk[l * H:(l + 1) * H, :] for l in order], axis=1))
    pf, pr = panels
    lane = lax.broadcasted_iota(jnp.int32, (H, W), 1)
    for s in range(L):
        f = pf if s == 0 else jnp.where(lane >= s * H, pltpu.roll(pf, s * H, 1), 0.0)
        back = (L - 1 - s) * H
        r = pr if back == 0 else jnp.where(lane < (s + 1) * H, pltpu.roll(pr, W - back, 1), 0.0)
        o_ref[s * H:(s + 1) * H, :] = (f + r).astype(BF16)


def _s5_toeplitz(abr, abi, cr, ci):
    _, G, W, P = abr.shape
    H = cr.shape[2]
    a_spec = pl.BlockSpec((2, None, W, P), lambda g: (0, g, 0, 0))
    c_spec = pl.BlockSpec((2, None, H, P), lambda g: (0, g, 0, 0))
    return pl.pallas_call(
        _s5_toeplitz_kernel,
        grid=(G,),
        in_specs=[a_spec, a_spec, c_spec, c_spec],
        out_specs=pl.BlockSpec((None, W, W), lambda g: (g, 0, 0)),
        out_shape=jax.ShapeDtypeStruct((G, W, W), BF16),
        compiler_params=_cparams(("parallel",)),
        name="s5_toeplitz",
    )(abr, abi, cr, ci)


def _s5_select_tables():
    H, O = S5_CH, S5_OCT
    pack = np.zeros((O // 2, O * LANES, 2 * LANES), np.float32)
    unpack = np.zeros((O // 2, O * LANES, 2 * LANES), np.float32)
    for j in range(8):
        for g in range(O):
            for h in range(H):
                pack[g // 2, j * LANES + g * H + h, (g % 2) * LANES + j * H + h] = 1.0
                unpack[j // 2, g * LANES + j * H + h, (j % 2) * LANES + g * H + h] = 1.0
    return jnp.asarray(pack, BF16), jnp.asarray(unpack, BF16)


def _s5_kernel(ul_ref, uc_ref, pack_ref, unpack_ref, mt_ref, bs_ref, cc_ref, al_ref, ol_ref, oc_ref,
               ug_s, y_s, s_s, h_s, *, ncl, ncc):
    L, P, O = S5_CHUNK, S5_STATE, S5_OCT
    nc = ncl + ncc
    nblk = L // 8

    for tb in range(nblk):
        zs = [jnp.concatenate([ul_ref[pl.ds(8 * tb + j, ncl, stride=L), :],
                               uc_ref[pl.ds(8 * tb + j, ncc, stride=L), :]], axis=0) for j in range(8)]
        zc = jnp.concatenate(zs, axis=1).astype(BF16)
        for gp in range(O // 2):
            r = _dot(zc, pack_ref[gp])
            ug_s[2 * gp, :, tb * LANES:(tb + 1) * LANES] = r[:, :LANES]
            ug_s[2 * gp + 1, :, tb * LANES:(tb + 1) * LANES] = r[:, LANES:]

    for g in range(O):
        ug = ug_s[g].astype(BF16)
        y_s[g] = _dot(ug, mt_ref[g])
        st = _dot(ug, bs_ref[g])
        for k in range(4):
            s_s[k, :, g, :] = st[:, k * P:(k + 1) * P]

    afr, afi, arr, ari = al_ref[0], al_ref[1], al_ref[2], al_ref[3]

    def step(j, carry):
        hfr, hfi, hrr, hri = carry
        cf = jnp.where(j < ncc, ncl + j, j - ncc)
        cr = nc - 1 - j
        h_s[0, cf] = hfr
        h_s[1, cf] = hfi
        h_s[2, cr] = hrr
        h_s[3, cr] = hri
        return (afr * hfr - afi * hfi + s_s[0, cf], afr * hfi + afi * hfr + s_s[1, cf],
                arr * hrr - ari * hri + s_s[2, cr], arr * hri + ari * hrr + s_s[3, cr])

    zero = jnp.zeros((O, P), F32)
    lax.fori_loop(0, nc, step, (zero, zero, zero, zero))

    for g in range(O):
        y = y_s[g]
        for k in range(4):
            y = y + _dot(h_s[k, :, g, :].astype(BF16), cc_ref[g, k])
        y_s[g] = y

    for tb in range(nblk):
        yc = jnp.concatenate([y_s[g, :, tb * LANES:(tb + 1) * LANES] for g in range(O)], axis=1)
        hi = yc.astype(BF16)
        lo = (yc - hi.astype(F32)).astype(BF16)
        for jp in range(4):
            z = _dot(hi, unpack_ref[jp]) + _dot(lo, unpack_ref[jp])
            for e in range(2):
                t = 8 * tb + 2 * jp + e
                zt = z[:, e * LANES:(e + 1) * LANES]
                ol_ref[pl.ds(t, ncl, stride=L), :] = zt[:ncl]
                oc_ref[pl.ds(t, ncc, stride=L), :] = zt[ncl:]


def s5_scan(pscan, tables, n_batch, t_lat, t_ctx):
    mt, bs, cc, al = tables
    pack, unpack = _s5_select_tables()
    L, P, O = S5_CHUNK, S5_STATE, S5_OCT
    W = L * S5_CH
    ncl, ncc = t_lat // L, t_ctx // L
    nc = ncl + ncc
    ctx0 = n_batch * t_lat // t_ctx
    once = pl.Buffered(1)
    return pl.pallas_call(
        functools.partial(_s5_kernel, ncl=ncl, ncc=ncc),
        grid=(S5_GROUPS // O, n_batch),
        in_specs=[
            pl.BlockSpec((t_lat, O * S5_CH), lambda o, b: (b, o)),
            pl.BlockSpec((t_ctx, O * S5_CH), lambda o, b: (ctx0 + b, o)),
            pl.BlockSpec(pack.shape, lambda o, b: (0, 0, 0), pipeline_mode=once),
            pl.BlockSpec(unpack.shape, lambda o, b: (0, 0, 0), pipeline_mode=once),
            pl.BlockSpec((O, W, W), lambda o, b: (o, 0, 0), pipeline_mode=once),
            pl.BlockSpec((O, W, 4 * P), lambda o, b: (o, 0, 0), pipeline_mode=once),
            pl.BlockSpec((O, 4, P, W), lambda o, b: (o, 0, 0, 0), pipeline_mode=once),
            pl.BlockSpec((4, O, P), lambda o, b: (0, o, 0)),
        ],
        out_specs=[
            pl.BlockSpec((t_lat, O * S5_CH), lambda o, b: (b, o)),
            pl.BlockSpec((t_ctx, O * S5_CH), lambda o, b: (b, o)),
        ],
        out_shape=[
            jax.ShapeDtypeStruct((n_batch * t_lat, GROUP_WIDTH), F32),
            jax.ShapeDtypeStruct((n_batch * t_ctx, GROUP_WIDTH), F32),
        ],
        scratch_shapes=[
            pltpu.VMEM((O, nc, W), F32),
            pltpu.VMEM((O, nc, W), F32),
            pltpu.VMEM((4, nc, O, P), F32),
            pltpu.VMEM((4, nc, O, P), F32),
        ],
        compiler_params=_cparams(("arbitrary", "arbitrary")),
        name="s5_scan",
    )(pscan, pscan, pack, unpack, mt, bs, cc, al)


def _lru_tile(x_ref, o_ref, row0, n_rows, carry, dr, w, xs_s, a_s, b_s):
    cw_ref, cb_ref, wa_ref, ba_ref, wx_ref, bx_ref, lam_ref = w
    tt = SEQ_TILE
    reverse = dr == 1
    row0 = pl.multiple_of(row0, tt)
    has_prev = (row0 > 0).astype(F32)
    has_next = (row0 + tt < n_rows).astype(F32)
    prev0 = pl.multiple_of(jnp.maximum(row0 - 8, 0), 8)
    next0 = pl.multiple_of(jnp.minimum(row0 + tt, n_rows - 8), 8)
    xs_s[dr, 0:8, :] = x_ref[pl.ds(prev0, 8), :] * has_prev
    xs_s[dr, 8:8 + tt, :] = x_ref[pl.ds(row0, tt), :]
    xs_s[dr, 8 + tt:16 + tt, :] = x_ref[pl.ds(next0, 8), :] * has_next
    xc = cb_ref[...] + jnp.zeros((tt, LANES), F32)
    for tap in range(LRU_CONV):
        xc = xc + cw_ref[tap:tap + 1, :] * xs_s[dr, pl.ds(6 + tap, tt), :]
    xb = xc.astype(BF16)
    r = jax.nn.sigmoid(_dot(xb, wa_ref[dr]) + ba_ref[dr])
    i = jax.nn.sigmoid(_dot(xb, wx_ref[dr]) + bx_ref[dr])
    z = -lam_ref[dr]
    softplus = jnp.maximum(z, 0.0) + jnp.log1p(jnp.exp(-jnp.abs(z)))
    log_a = -LRU_C * r * softplus
    a_all = jnp.exp(log_a)
    th = jnp.tanh(log_a)
    b_all = jnp.sqrt(-2.0 * th / (1.0 - th)) * (i * xc)

    nseg = 8
    slen = tt // nseg
    pitch = LRU_SEG_PITCH
    for sgm in range(nseg):
        a_s[dr, sgm * pitch:sgm * pitch + slen, :] = a_all[sgm * slen:(sgm + 1) * slen, :]
        b_s[dr, sgm * pitch:sgm * pitch + slen, :] = b_all[sgm * slen:(sgm + 1) * slen, :]
    h = jnp.zeros((nseg, LANES), F32)
    p = jnp.ones((nseg, LANES), F32)
    for i in (range(slen - 1, -1, -1) if reverse else range(slen)):
        pos = pl.ds(i, nseg, stride=pitch)
        a = a_s[dr, pos, :]
        h = a * h + b_s[dr, pos, :]
        p = a * p
        b_s[dr, pos, :] = h
        a_s[dr, pos, :] = p
    e = carry
    ins = [None] * nseg
    for sgm in (range(nseg - 1, -1, -1) if reverse else range(nseg)):
        ins[sgm] = e
        e = h[sgm:sgm + 1, :] + p[sgm:sgm + 1, :] * e
    e_in = jnp.concatenate(ins, axis=0)
    for i in range(slen):
        pos = pl.ds(i, nseg, stride=pitch)
        b_s[dr, pos, :] = b_s[dr, pos, :] + a_s[dr, pos, :] * e_in
    for sgm in range(nseg):
        o_ref[pl.ds(row0 + sgm * slen, slen), :] = b_s[dr, sgm * pitch:sgm * pitch + slen, :]
    return e


def _lru_kernel(xl_ref, xc_ref, cw_ref, cb_ref, wa_ref, ba_ref, wx_ref, bx_ref, lam_ref,
                hfl_ref, hfc_ref, hrl_ref, hrc_ref, xs_s, a_s, b_s, *, ntl, ntc):
    w = (cw_ref, cb_ref, wa_ref, ba_ref, wx_ref, bx_ref, lam_ref)
    tt = SEQ_TILE

    def run(x_ref, of_ref, or_ref, nt, carry):
        def body(j, c):
            cf = _lru_tile(x_ref, of_ref, j * tt, nt * tt, c[0], 0, w, xs_s, a_s, b_s)
            cr = _lru_tile(x_ref, or_ref, (nt - 1 - j) * tt, nt * tt, c[1], 1, w, xs_s, a_s, b_s)
            return cf, cr
        return lax.fori_loop(0, nt, body, carry)

    zero = jnp.zeros((1, LANES), F32)
    carry = run(xc_ref, hfc_ref, hrc_ref, ntc, (zero, zero))
    run(xl_ref, hfl_ref, hrl_ref, ntl, carry)


def lru_scan(pscan, conv_w, conv_b, w_a, b_a, w_x, b_x, lam, n_batch, t_lat, t_ctx):
    tt = SEQ_TILE
    gw = GROUP_WIDTH
    nh = gw // LANES
    ntl, ntc = t_lat // tt, t_ctx // tt
    ctx0 = n_batch * t_lat // t_ctx
    per_half = LRU_BLOCKS // nh

    def halves(w):
        return jnp.stack([jnp.stack([_block_diag(w[dr, k * per_half:(k + 1) * per_half]) for k in range(nh)])
                          for dr in range(2)]).astype(BF16)

    vec = lambda a: a.reshape(2, 1, gw)
    vspec = pl.BlockSpec((2, 1, LANES), lambda b, k: (0, 0, k))
    mspec = pl.BlockSpec((2, None, LANES, LANES), lambda b, k: (0, k, 0, 0))
    lat = pl.BlockSpec((t_lat, LANES), lambda b, k: (b, k))
    ctx = pl.BlockSpec((t_ctx, LANES), lambda b, k: (b, k))
    lat_shape = jax.ShapeDtypeStruct((n_batch * t_lat, gw), F32)
    ctx_shape = jax.ShapeDtypeStruct((n_batch * t_ctx, gw), F32)
    hfl, hfc, hrl, hrc = pl.pallas_call(
        functools.partial(_lru_kernel, ntl=ntl, ntc=ntc),
        grid=(n_batch, nh),
        in_specs=[
            pl.BlockSpec((t_lat, LANES), lambda b, k: (b, nh + k)),
            pl.BlockSpec((t_ctx, LANES), lambda b, k: (ctx0 + b, nh + k)),
            pl.BlockSpec((LRU_CONV, LANES), lambda b, k: (0, k)),
            pl.BlockSpec((1, LANES), lambda b, k: (0, k)),
            mspec, vspec, mspec, vspec, vspec,
        ],
        out_specs=[lat, ctx, lat, ctx],
        out_shape=[lat_shape, ctx_shape, lat_shape, ctx_shape],
        scratch_shapes=[
            pltpu.VMEM((2, tt + 16, LANES), F32),
            pltpu.VMEM((2, 8 * LRU_SEG_PITCH, LANES), F32),
            pltpu.VMEM((2, 8 * LRU_SEG_PITCH, LANES), F32),
        ],
        compiler_params=_cparams(("parallel", "parallel")),
        name="lru_scan",
    )(pscan, pscan, conv_w, conv_b.reshape(1, gw), halves(w_a), vec(b_a), halves(w_x), vec(b_x), vec(lam))
    return [hfl, hfc], [hrl, hrc]


def _block_diag(w):
    n, d, e = w.shape
    eye = jnp.eye(n, dtype=w.dtype)
    return (eye[:, None, :, None] * w[:, :, None, :]).reshape(n * d, n * e)


def _natten_bias(rpb):
    W, KR, KC, QR, NB = GRID_W, NA_KR, NA_KC, NA_QROWS, NA_BAND
    col = np.arange(W)
    cs = np.clip(col - KC // 2, 0, W - KC)
    inwin = (col[None, :] >= cs[:, None]) & (col[None, :] < cs[:, None] + KC)
    coff = np.clip(col[None, :] - col[:, None] + (KC - 1), 0, 2 * KC - 2)
    a = np.arange(QR)[:, None]
    i = np.arange(NB)[None, :]
    first = (i - a, (i < KR) & (a >= 0))
    mid = (i - a - KR // 2, (i >= a) & (i < a + KR))
    last = (i - a - NB + QR, (i >= NB - KR) & (a >= 0))
    coh = (coff[:, :, None] == np.arange(2 * KC - 1)).astype(np.float32)
    t = jnp.einsum('hrc,qkc->hrqk', rpb.astype(F32), coh, precision=lax.Precision.HIGHEST)
    t = t * LOG2E + np.where(inwin, 0.0, NEG_BIG).astype(np.float32)
    plan = tuple(
        tuple(tuple(int(np.clip(delta[ai, ii] + KR - 1, 0, 2 * KR - 2)) if valid[ai, ii] else -1
                    for ii in range(NB)) for ai in range(QR))
        for delta, valid in (first, mid, last))

    def expand(t_ref, o_ref):
        for v in range(3):
            for ai in range(QR):
                for ii in range(NB):
                    r = plan[v][ai][ii]
                    blk = t_ref[r] if r >= 0 else jnp.full((W, W), NEG_BIG, F32)
                    o_ref[v, ai * W:(ai + 1) * W, ii * W:(ii + 1) * W] = blk

    nh = rpb.shape[0]
    return pl.pallas_call(
        expand,
        grid=(nh,),
        in_specs=[pl.BlockSpec((None, 2 * KR - 1, W, W), lambda h: (h, 0, 0, 0))],
        out_specs=pl.BlockSpec((3, None, QR * W, NB * W), lambda h: (0, h, 0, 0)),
        out_shape=jax.ShapeDtypeStruct((3, nh, QR * W, NB * W), F32),
        compiler_params=_cparams(("parallel",)),
        name="natten_bias",
    )(t)


def _natten_kernel(q_ref, k_ref, v_ref, kc_ref, vc_ref, bias_ref, o_ref, *, rows):
    W, KR, hd = GRID_W, NA_KR, HEAD_DIM
    nblk = rows // NA_QROWS
    nb = NA_BAND * W
    tq = NA_QROWS * W
    ones_col = (lax.broadcasted_iota(jnp.int32, (nb + kc_ref.shape[0], hd), 1) == 0).astype(BF16)
    for u in range(NA_STEP_BLOCKS):
        blk = pl.program_id(1) * NA_STEP_BLOCKS + u
        bs = jnp.clip(blk * NA_QROWS - KR // 2, 0, rows - NA_BAND)
        var = jnp.where(blk == 0, 0, jnp.where(blk == nblk - 1, 2, 1))
        start = pl.multiple_of(bs * W, W)
        q = (q_ref[u * tq:(u + 1) * tq, :].astype(F32) * ((hd ** -0.5) * LOG2E)).astype(BF16)
        k_all = jnp.concatenate([k_ref[pl.ds(start, nb), :], kc_ref[...]], axis=0)
        v_all = jnp.concatenate([v_ref[pl.ds(start, nb), :], vc_ref[...]], axis=0)
        for h in range(NA_HEADS):
            sl = slice(h * hd, (h + 1) * hd)
            s = _dot_nt(q[:, sl], k_all[:, sl])
            sb = s[:, :nb] + bias_ref[var, h]
            sc = s[:, nb:]
            m = jnp.maximum(jnp.max(sb, axis=-1, keepdims=True), jnp.max(sc, axis=-1, keepdims=True))
            p = jnp.concatenate([jnp.exp2(sb - m), jnp.exp2(sc - m)], axis=1).astype(BF16)
            o = _dot(p, jnp.concatenate([v_all[:, sl], ones_col], axis=1))
            o_ref[u * tq:(u + 1) * tq, sl] = o[:, :hd] / o[:, hd:hd + 1]


def natten_latent(pattn, bias, n_batch, t_lat, t_ctx):
    gw = GROUP_WIDTH
    rows = t_lat // GRID_W
    assert NA_QROWS == NA_KR // 2 and NA_BAND == NA_KR + NA_QROWS and rows % NA_QROWS == 0 and rows >= NA_BAND
    assert (rows // NA_QROWS) % NA_STEP_BLOCKS == 0
    nblk = rows // NA_QROWS // NA_STEP_BLOCKS
    tq = NA_STEP_BLOCKS * NA_QROWS * GRID_W
    ctx0 = n_batch * t_lat // t_ctx
    return pl.pallas_call(
        functools.partial(_natten_kernel, rows=rows),
        grid=(n_batch, nblk),
        in_specs=[
            pl.BlockSpec((tq, gw), lambda b, r: (b * nblk + r, 2)),
            pl.BlockSpec((t_lat, gw), lambda b, r: (b, 3)),
            pl.BlockSpec((t_lat, gw), lambda b, r: (b, 4)),
            pl.BlockSpec((t_ctx, gw), lambda b, r: (ctx0 + b, 3)),
            pl.BlockSpec((t_ctx, gw), lambda b, r: (ctx0 + b, 4)),
            pl.BlockSpec(bias.shape, lambda b, r: (0, 0, 0, 0)),
        ],
        out_specs=pl.BlockSpec((tq, gw), lambda b, r: (b * nblk + r, 0)),
        out_shape=jax.ShapeDtypeStruct((n_batch * t_lat, gw), F32),
        compiler_params=_cparams(("parallel", "arbitrary")),
        name="natten_latent",
    )(pattn, pattn, pattn, pattn, pattn, bias)


def _ctx_attn_kernel(q_ref, k_ref, v_ref, o_ref):
    hd = HEAD_DIM
    q, k, v = q_ref[...], k_ref[...], v_ref[...]
    for h in range(NA_HEADS):
        sl = slice(h * hd, (h + 1) * hd)
        s = _dot_nt(q[:, sl], k[:, sl]) * (hd ** -0.5)
        m = jnp.max(s, axis=-1, keepdims=True)
        p = jnp.exp(s - m)
        l = jnp.sum(p, axis=-1, keepdims=True)
        o_ref[:, sl] = _dot(p.astype(BF16), v[:, sl]) / l


def natten_context(pattn, n_batch, t_lat, t_ctx):
    gw = GROUP_WIDTH
    ctx0 = n_batch * t_lat // t_ctx
    return pl.pallas_call(
        _ctx_attn_kernel,
        grid=(n_batch,),
        in_specs=[pl.BlockSpec((t_ctx, gw), lambda b, c=c: (ctx0 + b, c)) for c in (2, 3, 4)],
        out_specs=pl.BlockSpec((t_ctx, gw), lambda b: (b, 0)),
        out_shape=jax.ShapeDtypeStruct((n_batch * t_ctx, gw), F32),
        compiler_params=_cparams(("parallel",)),
        name="natten_context",
    )(pattn, pattn, pattn)


def _rope_tables(t_lat, t_ctx):
    half = HEAD_DIM // 4
    freqs = ROPE_BASE ** (-np.arange(half, dtype=np.float64) / half)
    t = np.arange(t_lat)
    ang_r = (t // GRID_W)[:, None] * freqs[None, :]
    ang_c = (t % GRID_W)[:, None] * freqs[None, :]
    ang = np.concatenate([ang_r, ang_r, ang_c, ang_c], axis=1)
    cos = np.concatenate([np.cos(ang), np.ones((t_ctx, HEAD_DIM))], axis=0)
    sin = np.concatenate([np.sin(ang), np.zeros((t_ctx, HEAD_DIM))], axis=0)
    return (np.tile(cos, (1, GQA_Q_HEADS)).astype(np.float32), np.tile(sin, (1, GQA_Q_HEADS)).astype(np.float32))


def _gqa_prep_kernel(x_ref, cos_ref, sin_ref, gq_ref, gk_ref, gm_ref, q_ref, k_ref, v_ref):
    hd = HEAD_DIM
    nq, nk = GQA_Q_HEADS * hd, GQA_KV_HEADS * hd
    x = x_ref[...].astype(F32)
    cos, sin = cos_ref[...], sin_ref[...]
    gm = gm_ref[...]

    def norm_rope(t, gain, width):
        ms = _dot((t * t).astype(BF16), gm[:width, :width])
        tn = t * lax.rsqrt(ms + EPS) * gain
        lane = lax.broadcasted_iota(jnp.int32, tn.shape, 1)
        first = (lane % (hd // 2)) < (hd // 4)
        rot = jnp.where(first, -pltpu.roll(tn, width - hd // 4, 1), pltpu.roll(tn, hd // 4, 1))
        return tn * cos[:, :width] + rot * sin[:, :width]

    q = norm_rope(x[:, :nq], gq_ref[...], nq) * ((hd ** -0.5) * LOG2E)
    kt = norm_rope(x[:, nq:nq + nk], gk_ref[...], nk).T
    v = x_ref[:, nq + nk:nq + 2 * nk]
    for h in range(GQA_Q_HEADS):
        q_ref[h] = q[:, h * hd:(h + 1) * hd].astype(BF16)
    ones_col = (lax.broadcasted_iota(jnp.int32, (v.shape[0], hd), 1) == 0).astype(BF16)
    for h in range(GQA_KV_HEADS):
        k_ref[h] = kt[h * hd:(h + 1) * hd, :].astype(BF16)
        v_ref[h] = jnp.concatenate([v[:, h * hd:(h + 1) * hd], ones_col], axis=1)


def gqa_prep(pattn, gq, gk, n_batch, t_lat, t_ctx):
    tt = SEQ_TILE
    assert tt == KEY_CHUNK
    hd = HEAD_DIM
    ntl, ntc = t_lat // tt, t_ctx // tt
    s_len = t_lat + t_ctx
    cos, sin = _rope_tables(t_lat, t_ctx)
    nq = GQA_Q_HEADS * hd
    gmat = _block_diag(jnp.full((GQA_Q_HEADS, hd, hd), 1.0 / hd, F32)).astype(BF16)

    def pos(i):
        is_ctx = i >= n_batch * ntl
        c = i - n_batch * ntl
        return jnp.where(is_ctx, ntl + c % ntc, i % ntl)

    def bat(i):
        is_ctx = i >= n_batch * ntl
        return jnp.where(is_ctx, (i - n_batch * ntl) // ntc, i // ntl)

    return pl.pallas_call(
        _gqa_prep_kernel,
        grid=(n_batch * (ntl + ntc),),
        in_specs=[
            pl.BlockSpec((tt, 2 * nq), lambda i: (i, 0)),
            pl.BlockSpec((tt, nq), lambda i: (pos(i), 0)),
            pl.BlockSpec((tt, nq), lambda i: (pos(i), 0)),
            pl.BlockSpec((1, nq), lambda i: (0, 0)),
            pl.BlockSpec((1, nq // 2), lambda i: (0, 0)),
            pl.BlockSpec((nq, nq), lambda i: (0, 0)),
        ],
        out_specs=[
            pl.BlockSpec((None, GQA_Q_HEADS, tt, hd), lambda i: (bat(i), 0, pos(i), 0)),
            pl.BlockSpec((None, GQA_KV_HEADS, None, hd, tt), lambda i: (bat(i), 0, pos(i), 0, 0)),
            pl.BlockSpec((None, GQA_KV_HEADS, tt, 2 * hd), lambda i: (bat(i), 0, pos(i), 0)),
        ],
        out_shape=[
            jax.ShapeDtypeStruct((n_batch, GQA_Q_HEADS, s_len, hd), BF16),
            jax.ShapeDtypeStruct((n_batch, GQA_KV_HEADS, s_len // tt, hd, tt), BF16),
            jax.ShapeDtypeStruct((n_batch, GQA_KV_HEADS, s_len, 2 * hd), BF16),
        ],
        compiler_params=_cparams(("parallel",)),
        name="gqa_prep",
    )(pattn, cos, sin, jnp.tile(gq.astype(F32), GQA_Q_HEADS).reshape(1, nq),
      jnp.tile(gk.astype(F32), GQA_KV_HEADS).reshape(1, nq // 2), gmat)


def _gqa_kernel(q_ref, kt_ref, v_ref, o_ref, s_s, m_s, acc_s, *, unroll):
    g, tq, hd = q_ref.shape
    nch, _, ck = kt_ref.shape
    nl = ck // LANES
    q = q_ref[...].reshape(g * tq, hd)
    m_s[...] = jnp.full_like(m_s, -jnp.inf)

    def scores(c, carry):
        s = _dot(q, kt_ref[c])
        s_s[c] = s
        fold = s[:, 0:LANES]
        for j in range(1, nl):
            fold = jnp.maximum(fold, s[:, j * LANES:(j + 1) * LANES])
        m_s[...] = jnp.maximum(m_s[...], fold)
        return carry

    lax.fori_loop(0, nch, scores, 0, unroll=unroll)
    m_s[...] = jnp.broadcast_to(jnp.max(m_s[...], axis=-1, keepdims=True), m_s.shape)
    acc_s[...] = jnp.zeros_like(acc_s)

    def values(grp, carry):
        m = m_s[...]
        ps = []
        for u in range(unroll):
            s = s_s[grp * unroll + u]
            ps += [jnp.exp2(s[:, j * LANES:(j + 1) * LANES] - m).astype(BF16) for j in range(nl)]
        off = pl.multiple_of(grp * (unroll * ck), unroll * ck)
        acc_s[...] += _dot(jnp.concatenate(ps, axis=1), v_ref[pl.ds(off, unroll * ck), :])
        return carry

    lax.fori_loop(0, nch // unroll, values, 0)
    acc = acc_s[...]
    o = acc[:, :hd] / acc[:, hd:hd + 1]
    o_ref[...] = jnp.concatenate([o[i * tq:(i + 1) * tq] for i in range(g)], axis=1)


def gqa_attention(q, kt, v, tq, q_blk0, nq, c_blk, nch):
    n_batch = q.shape[0]
    hd = HEAD_DIM
    g = GQA_GROUP
    ck = KEY_CHUNK
    unroll = next(u for u in (11, 3, 1) if nch % u == 0)
    return pl.pallas_call(
        functools.partial(_gqa_kernel, unroll=unroll),
        grid=(n_batch, GQA_KV_HEADS, nq),
        in_specs=[
            pl.BlockSpec((None, g, tq, hd), lambda b, h, i: (b, h, q_blk0 + i, 0)),
            pl.BlockSpec((None, None, nch, hd, ck), lambda b, h, i: (b, h, c_blk, 0, 0)),
            pl.BlockSpec((None, None, nch * ck, 2 * hd), lambda b, h, i: (b, h, c_blk, 0)),
        ],
        out_specs=pl.BlockSpec((tq, g * hd), lambda b, h, i: (b * nq + i, h)),
        out_shape=jax.ShapeDtypeStruct((n_batch * nq * tq, GQA_Q_HEADS * hd), F32),
        scratch_shapes=[
            pltpu.VMEM((nch, g * tq, ck), F32),
            pltpu.VMEM((g * tq, LANES), F32),
            pltpu.VMEM((g * tq, 2 * hd), F32),
        ],
        compiler_params=_cparams(("parallel", "parallel", "arbitrary")),
        name="gqa_attention",
    )(q, kt, v)


def kernel(x, c, ctx, c_ctx, w_ada, b_ada, g_ffn1, w_ffn1_in, w_ffn1_out, g_mix, w_in, w_out, s5_lambda_re, s5_lambda_im, s5_log_step, s5_b_re, s5_b_im, s5_c_re, s5_c_im, s5_d, s5_w_glu, na_rpb, gqa_q_norm, gqa_k_norm, lru_conv_w, lru_conv_b, lru_w_a, lru_b_a, lru_w_x, lru_b_x, lru_lambda, g_ffn2, w_ffn2_in, w_ffn2_out, g_final):
    n_batch, t_lat, d = x.shape
    t_ctx = ctx.shape[1]
    depth = w_ada.shape[0]
    assert d == D_MODEL and t_lat % TOKEN_TILE == 0 and (n_batch * t_ctx) % TOKEN_TILE == 0
    assert t_lat % t_ctx == 0 and t_ctx % SEQ_TILE == 0 and t_lat % FLASH_TQ == 0
    n_lat = n_batch * t_lat
    n_all = n_lat + n_batch * t_ctx
    tiles_per_type = t_lat // TOKEN_TILE
    assert (n_batch * t_ctx) // TOKEN_TILE <= tiles_per_type

    c8 = jnp.zeros((8, D_MODEL), F32).at[:n_batch].set(c.astype(F32)).at[n_batch].set(c_ctx.astype(F32))
    mods_all = ada_mods(c8, w_ada, b_ada).reshape(depth, 8, N_MOD, D_MODEL)

    s_len = t_lat + t_ctx
    n_lat_tiles = n_lat // TOKEN_TILE
    n_ctx_tiles = (n_all - n_lat) // TOKEN_TILE
    hs = [x.reshape(n_lat, D_MODEL), ctx.reshape(n_batch * t_ctx, D_MODEL)]
    w1i, w1o = w_ffn1_in.astype(BF16), w_ffn1_out.astype(BF16)
    w2i, w2o = w_ffn2_in.astype(BF16), w_ffn2_out.astype(BF16)
    w_in_b, w_out_b, w_glu_b = w_in.astype(BF16), w_out.astype(BF16), s5_w_glu.astype(BF16)

    for l in range(depth):
        need_ctx = l < depth - 1
        mods = mods_all[l]
        h, pscan, pattn = ffn_in_projection(hs, mods, g_ffn1[l], w1i, w1o, g_mix[l], w_in_b, l,
                                            n_lat_tiles, n_ctx_tiles, tiles_per_type)

        tables = _s5_tables(s5_lambda_re[l], s5_lambda_im[l], s5_log_step[l], s5_b_re[l], s5_b_im[l],
                            s5_c_re[l], s5_c_im[l])
        ys5 = list(s5_scan(pscan, tables, n_batch, t_lat, t_ctx))

        hf, hr = lru_scan(pscan, lru_conv_w[l], lru_conv_b[l], lru_w_a[l], lru_b_a[l], lru_w_x[l], lru_b_x[l],
                          lru_lambda[l], n_batch, t_lat, t_ctx)

        yb = [natten_latent(pattn, _natten_bias(na_rpb[l]), n_batch, t_lat, t_ctx)]

        qh, kth, vh = gqa_prep(pattn, gqa_q_norm[l], gqa_k_norm[l], n_batch, t_lat, t_ctx)
        yc = [gqa_attention(qh, kth, vh, FLASH_TQ, 0, t_lat // FLASH_TQ, 0, s_len // KEY_CHUNK)]

        if need_ctx:
            yb.append(natten_context(pattn, n_batch, t_lat, t_ctx))
            yc.append(gqa_attention(qh, kth, vh, t_ctx, t_lat // t_ctx, 1, t_lat // t_ctx, t_ctx // KEY_CHUNK))
        else:
            ys5, hf, hr = ys5[:1], hf[:1], hr[:1]

        h = mix_out_ffn(h, mods, ys5, yb, yc, pscan, hf, hr, s5_d[l], w_glu_b, w_out_b, g_ffn2[l], w2i, w2o, l,
                        n_lat_tiles,
                        n_ctx_tiles if need_ctx else 0, tiles_per_type, g_final=None if need_ctx else g_final)
        hs = [h]
    return h.reshape(n_batch, t_lat, D_MODEL)
```

```python
import functools
import math

import numpy as np
import jax
import jax.numpy as jnp
from jax import lax
from jax.experimental import pallas as pl
from jax.experimental.pallas import tpu as pltpu

F32 = jnp.float32
BF16 = jnp.bfloat16

D_MODEL = 1024
GRID_W = 64
HEAD_DIM = 64
GROUP_WIDTH = D_MODEL // 4
S5_CH = 16
S5_GROUPS = GROUP_WIDTH // S5_CH
S5_STATE = 64
NA_HEADS = GROUP_WIDTH // HEAD_DIM
NA_KR = 8
NA_KC = 16
GQA_Q_HEADS = GROUP_WIDTH // HEAD_DIM
GQA_KV_HEADS = GQA_Q_HEADS // 2
GQA_GROUP = GQA_Q_HEADS // GQA_KV_HEADS
LRU_BLOCKS = GROUP_WIDTH // HEAD_DIM
LRU_CONV = 4
LRU_C = 8.0
D_FF = ((8 * D_MODEL // 3 + 127) // 128) * 128
FFN_RES = 0.5
ROPE_BASE = 10000.0
EPS = 1e-6
N_MOD = 9

LANES = 128
TOKEN_TILE = 512
SEQ_TILE = 256
LRU_SEG_PITCH = SEQ_TILE // 8 + 4
S5_CHUNK = 32
S5_OCT = 8
MXU_DEPTH = 256
FF_CHUNKS = ((0, 6 * MXU_DEPTH), (6 * MXU_DEPTH, D_FF))
FLASH_TQ = 512
KEY_CHUNK = 256
NA_QROWS = NA_KR // 2
NA_BAND = NA_KR + NA_QROWS
NA_STEP_BLOCKS = 4
LOG2E = math.log2(math.e)
NEG_BIG = -1e30
V7X_VMEM_BYTES = 64 * 1024 * 1024
VMEM_LIMIT = V7X_VMEM_BYTES - 8 * 1024 * 1024


def _cparams(sem):
    return pltpu.CompilerParams(dimension_semantics=sem, vmem_limit_bytes=VMEM_LIMIT)


def _dot(a, b):
    return jnp.dot(a, b, preferred_element_type=F32)


def _dot_nt(a, b):
    return lax.dot_general(a, b, (((1,), (1,)), ((), ())), preferred_element_type=F32)


def _ada_kernel(c_ref, w_ref, b_ref, o_ref):
    c = c_ref[...]
    s = c * jax.nn.sigmoid(c)
    o_ref[...] = _dot(s.astype(BF16), w_ref[...].astype(BF16)) + b_ref[...]


def ada_mods(c8, w_ada, b_ada):
    L = w_ada.shape[0]
    tn = D_MODEL
    return pl.pallas_call(
        _ada_kernel,
        grid=(L, N_MOD * D_MODEL // tn),
        in_specs=[
            pl.BlockSpec((8, D_MODEL), lambda l, j: (0, 0)),
            pl.BlockSpec((None, D_MODEL, tn), lambda l, j: (l, 0, j)),
            pl.BlockSpec((None, 1, tn), lambda l, j: (l, 0, j)),
        ],
        out_specs=pl.BlockSpec((None, 8, tn), lambda l, j: (l, 0, j)),
        out_shape=jax.ShapeDtypeStruct((L, 8, N_MOD * D_MODEL), F32),
        compiler_params=_cparams(("parallel", "parallel")),
        name="ada_mods",
    )(c8, w_ada, b_ada.reshape(L, 1, N_MOD * D_MODEL))


def _norm_mod(x, g, shift, scale):
    ms = jnp.mean(x * x, axis=-1, keepdims=True)
    y = x * lax.rsqrt(ms + EPS) * g
    return y * (1.0 + scale) + shift


def _ffn_core(x, mod_ref, g, wi_ref, wo_ref, k0):
    y = _norm_mod(x, g, mod_ref[k0:k0 + 1, :], mod_ref[k0 + 1:k0 + 2, :]).astype(BF16)
    acc = jnp.zeros(x.shape, F32)
    for lo, hi in FF_CHUNKS:
        a = _dot(y, wi_ref[:, lo:hi])
        b = _dot(y, wi_ref[:, D_FF + lo:D_FF + hi])
        h = (a * jax.nn.sigmoid(a) * b).astype(BF16)
        acc = acc + _dot(h, wo_ref[lo:hi, :])
    return x + FFN_RES * mod_ref[k0 + 2:k0 + 3, :] * acc


def _resident(shape, layer):
    return pl.BlockSpec((None,) + shape, lambda i: (layer,) + (0,) * len(shape), pipeline_mode=pl.Buffered(1))


def _row_specs(n_lat_tiles, n_ctx_tiles, width, col=0):
    lat = pl.BlockSpec((TOKEN_TILE, width), lambda i: (jnp.minimum(i, n_lat_tiles - 1), col))
    if n_ctx_tiles == 0:
        return [lat]
    ctx = pl.BlockSpec((TOKEN_TILE, width), lambda i: (jnp.clip(i - n_lat_tiles, 0, n_ctx_tiles - 1), col))
    return [lat, ctx]


def _pick(refs, n_lat_tiles):
    if len(refs) == 1:
        return refs[0][...]
    return jnp.where(pl.program_id(0) >= n_lat_tiles, refs[1][...], refs[0][...])


def _ffn_in_kernel(*refs, n_src, n_lat_tiles):
    x_refs = refs[:n_src]
    mod_ref, g_ref, wi_ref, wo_ref, gm_ref, win_ref, h_ref, ps_ref, pa_ref = refs[n_src:]
    h = _ffn_core(_pick(x_refs, n_lat_tiles), mod_ref, g_ref[...], wi_ref, wo_ref, 0)
    h_ref[...] = h
    y = _norm_mod(h, gm_ref[...], mod_ref[3:4, :], mod_ref[4:5, :]).astype(BF16)
    p = _dot(y, win_ref[...])
    gw = GROUP_WIDTH
    ps_ref[...] = jnp.concatenate([p[:, 0:gw], p[:, 6 * gw:8 * gw]], axis=1)
    pa_ref[...] = jnp.concatenate([p[:, 4 * gw:6 * gw], p[:, gw:4 * gw]], axis=1).astype(BF16)


def ffn_in_projection(xs, mods, g, w_i, w_o, g_mix, w_in_l, layer, n_lat_tiles, n_ctx_tiles, tiles_per_type):
    nt = n_lat_tiles + n_ctx_tiles
    n = nt * TOKEN_TILE
    n_scan = 3 * GROUP_WIDTH
    n_attn = w_in_l.shape[-1] - n_scan
    x_specs = (_row_specs(n_lat_tiles, n_ctx_tiles, D_MODEL) if len(xs) == 2
               else [pl.BlockSpec((TOKEN_TILE, D_MODEL), lambda i: (i, 0))])
    vec = pl.BlockSpec((1, D_MODEL), lambda i: (0, 0))
    return pl.pallas_call(
        functools.partial(_ffn_in_kernel, n_src=len(xs), n_lat_tiles=n_lat_tiles),
        grid=(nt,),
        in_specs=x_specs + [
            pl.BlockSpec((None, N_MOD, D_MODEL), lambda i: (i // tiles_per_type, 0, 0)),
            vec, _resident((D_MODEL, 2 * D_FF), layer), _resident((D_FF, D_MODEL), layer),
            vec, _resident((D_MODEL, n_scan + n_attn), layer),
        ],
        out_specs=[
            pl.BlockSpec((TOKEN_TILE, D_MODEL), lambda i: (i, 0)),
            pl.BlockSpec((TOKEN_TILE, n_scan), lambda i: (i, 0)),
            pl.BlockSpec((TOKEN_TILE, n_attn), lambda i: (i, 0)),
        ],
        out_shape=[
            jax.ShapeDtypeStruct((n, D_MODEL), F32),
            jax.ShapeDtypeStruct((n, n_scan), F32),
            jax.ShapeDtypeStruct((n, n_attn), BF16),
        ],
        compiler_params=_cparams(("parallel",)),
        name="ffn_in_projection",
    )(*xs, mods, g.reshape(1, D_MODEL), w_i, w_o, g_mix.reshape(1, D_MODEL), w_in_l)


def _mix_ffn_kernel(*refs, n_src, n_lat_tiles, final):
    gw = GROUP_WIDTH
    h_ref, mod_ref = refs[:2]
    ys5_refs, yb_refs, yc_refs, hf_refs, hr_refs = (refs[2 + k * n_src:2 + (k + 1) * n_src] for k in range(5))
    rest = refs[2 + 5 * n_src:]
    u_ref, gl_ref, dsk_ref, wglu_ref, wout_ref, g_ref, wi_ref, wo_ref = rest[:8]
    o_ref = rest[-1]
    ya = jax.nn.gelu(_pick(ys5_refs, n_lat_tiles) + dsk_ref[...] * u_ref[...])
    ya = ya * jax.nn.sigmoid(_dot(ya.astype(BF16), wglu_ref[...]))
    yd = (_pick(hf_refs, n_lat_tiles) + _pick(hr_refs, n_lat_tiles)) * jax.nn.gelu(gl_ref[...])
    acc = _dot(ya.astype(BF16), wout_ref[0:gw, :])
    acc = acc + _dot(_pick(yb_refs, n_lat_tiles).astype(BF16), wout_ref[gw:2 * gw, :])
    acc = acc + _dot(_pick(yc_refs, n_lat_tiles).astype(BF16), wout_ref[2 * gw:3 * gw, :])
    acc = acc + _dot(yd.astype(BF16), wout_ref[3 * gw:4 * gw, :])
    h = h_ref[...] + mod_ref[5:6, :] * acc
    out = _ffn_core(h, mod_ref, g_ref[...], wi_ref, wo_ref, 6)
    if final:
        ms = jnp.mean(out * out, axis=-1, keepdims=True)
        out = out * lax.rsqrt(ms + EPS) * rest[8][...]
    o_ref[...] = out


def mix_out_ffn(h, mods, ys5, yb, yc, pscan, hf, hr, d_skip, w_glu, w_out, g, w_i, w_o, layer,
                n_lat_tiles, n_ctx_tiles, tiles_per_type, g_final=None):
    gw = GROUP_WIDTH
    final = g_final is not None
    nt = n_lat_tiles + n_ctx_tiles
    row = lambda c: pl.BlockSpec((TOKEN_TILE, gw), lambda i: (i, c))
    vec = pl.BlockSpec((1, D_MODEL), lambda i: (0, 0))
    mixer_specs = _row_specs(n_lat_tiles, n_ctx_tiles, gw)
    in_specs = ([pl.BlockSpec((TOKEN_TILE, D_MODEL), lambda i: (i, 0)),
                 pl.BlockSpec((None, N_MOD, D_MODEL), lambda i: (i // tiles_per_type, 0, 0))]
                + mixer_specs * 5
                + [row(0), row(2),
                   pl.BlockSpec((1, gw), lambda i: (0, 0)), _resident((gw, gw), layer),
                   _resident((D_MODEL, D_MODEL), layer),
                   vec, _resident((D_MODEL, 2 * D_FF), layer), _resident((D_FF, D_MODEL), layer)])
    args = [h, mods, *ys5, *yb, *yc, *hf, *hr, pscan, pscan, d_skip.reshape(1, gw), w_glu, w_out,
            g.reshape(1, D_MODEL), w_i, w_o]
    if final:
        in_specs.append(vec)
        args.append(g_final.reshape(1, D_MODEL))
    return pl.pallas_call(
        functools.partial(_mix_ffn_kernel, n_src=len(ys5), n_lat_tiles=n_lat_tiles, final=final),
        grid=(nt,),
        in_specs=in_specs,
        out_specs=pl.BlockSpec((TOKEN_TILE, D_MODEL), lambda i: (i, 0)),
        out_shape=jax.ShapeDtypeStruct((nt * TOKEN_TILE, D_MODEL), F32),
        compiler_params=_cparams(("parallel",)),
        name="mix_out_ffn",
    )(*args)


def _s5_tables(lam_re, lam_im, log_step, b_re, b_im, c_re, c_im):
    L = S5_CHUNK
    G, P, H = S5_GROUPS, S5_STATE, S5_CH
    hp = lax.Precision.HIGHEST
    W = L * H
    lr, li = lam_re.astype(F32), lam_im.astype(F32)
    dt = jnp.exp(log_step.astype(F32))[..., None]
    mag = jnp.exp(lr * dt)
    ar, ai = mag * jnp.cos(li * dt), mag * jnp.sin(li * dt)
    den = lr * lr + li * li
    fr = ((ar - 1) * lr + ai * li) / den
    fi = (ai * lr - (ar - 1) * li) / den
    br, bi = b_re.astype(F32), b_im.astype(F32)
    bbr = jnp.swapaxes(fr[..., None] * br - fi[..., None] * bi, 2, 3)
    bbi = jnp.swapaxes(fr[..., None] * bi + fi[..., None] * br, 2, 3)
    cr, ci = c_re.astype(F32), c_im.astype(F32)
    tau = jnp.arange(L + 1, dtype=F32)[None, None, :, None]
    pmag = jnp.exp(tau * (lr * dt)[:, :, None, :])
    pang = tau * (li * dt)[:, :, None, :]
    pw_r, pw_i = pmag * jnp.cos(pang), pmag * jnp.sin(pang)
    abr = pw_r[:, :, :L, None, :] * bbr[:, :, None] - pw_i[:, :, :L, None, :] * bbi[:, :, None]
    abi = pw_r[:, :, :L, None, :] * bbi[:, :, None] + pw_i[:, :, :L, None, :] * bbr[:, :, None]
    mt = _s5_toeplitz(abr.reshape(2, G, W, P), abi.reshape(2, G, W, P), cr, ci)

    def rows(x, flip):
        return (x[:, ::-1] if flip else x).reshape(G, W, P)

    bs = jnp.concatenate([rows(abr[0], True), rows(abi[0], True),
                          rows(abr[1], False), rows(abi[1], False)], axis=-1).astype(BF16)
    eo = np.tile(np.eye(H, dtype=np.float32), (1, L))
    et = np.repeat(np.eye(L, dtype=np.float32), H, axis=1)
    qr = jnp.stack([pw_r[0, :, 1:], pw_r[1, :, :0:-1]])
    qi = jnp.stack([pw_i[0, :, 1:], pw_i[1, :, :0:-1]])
    crx = jnp.einsum('dgop,ox->dgpx', cr, eo, precision=hp)
    cix = jnp.einsum('dgop,ox->dgpx', ci, eo, precision=hp)
    qrx = jnp.einsum('dgtp,tx->dgpx', qr, et, precision=hp)
    qix = jnp.einsum('dgtp,tx->dgpx', qi, et, precision=hp)
    ccr = crx * qrx - cix * qix
    cci = -crx * qix - cix * qrx
    cc = jnp.stack([ccr[0], cci[0], ccr[1], cci[1]], axis=1).astype(BF16)
    al = jnp.stack([pw_r[0, :, L], pw_i[0, :, L], pw_r[1, :, L], pw_i[1, :, L]])
    return mt, bs, cc, al


def _s5_toeplitz_kernel(ar_ref, ai_ref, cr_ref, ci_ref, o_ref):
    L, H = S5_CHUNK, S5_CH
    W = L * H

    def split(x):
        hi = x.astype(BF16)
        return hi, (x - hi.astype(F32)).astype(BF16)

    panels = []
    for d in range(2):
        kk = None
        for a_ref, c_ref in ((ar_ref, cr_ref), (ai_ref, ci_ref)):
            a_hi, a_lo = split(a_ref[d])
            c_hi, c_lo = split(c_ref[d])
            t = _dot_nt(a_hi, c_hi) + _dot_nt(a_hi, c_lo) + _dot_nt(a_lo, c_hi)
            kk = t if kk is None else kk - t
        order = range(L) if d == 0 else range(L - 1, -1, -1)
        panels.append(jnp.concatenate([kk[l * H:(l + 1) * H, :] for l in order], axis=1))
    pf, pr = panels
    lane = lax.broadcasted_iota(jnp.int32, (H, W), 1)
    for s in range(L):
        f = pf if s == 0 else jnp.where(lane >= s * H, pltpu.roll(pf, s * H, 1), 0.0)
        back = (L - 1 - s) * H
        r = pr if back == 0 else jnp.where(lane < (s + 1) * H, pltpu.roll(pr, W - back, 1), 0.0)
        o_ref[s * H:(s + 1) * H, :] = (f + r).astype(BF16)


def _s5_toeplitz(abr, abi, cr, ci):
    _, G, W, P = abr.shape
    H = cr.shape[2]
    a_spec = pl.BlockSpec((2, None, W, P), lambda g: (0, g, 0, 0))
    c_spec = pl.BlockSpec((2, None, H, P), lambda g: (0, g, 0, 0))
    return pl.pallas_call(
        _s5_toeplitz_kernel,
        grid=(G,),
        in_specs=[a_spec, a_spec, c_spec, c_spec],
        out_specs=pl.BlockSpec((None, W, W), lambda g: (g, 0, 0)),
        out_shape=jax.ShapeDtypeStruct((G, W, W), BF16),
        compiler_params=_cparams(("parallel",)),
        name="s5_toeplitz",
    )(abr, abi, cr, ci)


def _s5_select_tables():
    H, O = S5_CH, S5_OCT
    pack = np.zeros((O // 2, O * LANES, 2 * LANES), np.float32)
    unpack = np.zeros((O // 2, O * LANES, 2 * LANES), np.float32)
    for j in range(8):
        for g in range(O):
            for h in range(H):
                pack[g // 2, j * LANES + g * H + h, (g % 2) * LANES + j * H + h] = 1.0
                unpack[j // 2, g * LANES + j * H + h, (j % 2) * LANES + g * H + h] = 1.0
    return jnp.asarray(pack, BF16), jnp.asarray(unpack, BF16)


def _s5_kernel(ul_ref, uc_ref, pack_ref, unpack_ref, mt_ref, bs_ref, cc_ref, al_ref, ol_ref, oc_ref,
               ug_s, y_s, s_s, h_s, *, ncl, ncc):
    L, P, O = S5_CHUNK, S5_STATE, S5_OCT
    nc = ncl + ncc
    nblk = L // 8

    for tb in range(nblk):
        zs = [jnp.concatenate([ul_ref[pl.ds(8 * tb + j, ncl, stride=L), :],
                               uc_ref[pl.ds(8 * tb + j, ncc, stride=L), :]], axis=0) for j in range(8)]
        zc = jnp.concatenate(zs, axis=1).astype(BF16)
        for gp in range(O // 2):
            r = _dot(zc, pack_ref[gp])
            ug_s[2 * gp, :, tb * LANES:(tb + 1) * LANES] = r[:, :LANES]
            ug_s[2 * gp + 1, :, tb * LANES:(tb + 1) * LANES] = r[:, LANES:]

    for g in range(O):
        ug = ug_s[g].astype(BF16)
        y_s[g] = _dot(ug, mt_ref[g])
        st = _dot(ug, bs_ref[g])
        for k in range(4):
            s_s[k, :, g, :] = st[:, k * P:(k + 1) * P]

    afr, afi, arr, ari = al_ref[0], al_ref[1], al_ref[2], al_ref[3]

    def step(j, carry):
        hfr, hfi, hrr, hri = carry
        cf = jnp.where(j < ncc, ncl + j, j - ncc)
        cr = nc - 1 - j
        h_s[0, cf] = hfr
        h_s[1, cf] = hfi
        h_s[2, cr] = hrr
        h_s[3, cr] = hri
        return (afr * hfr - afi * hfi + s_s[0, cf], afr * hfi + afi * hfr + s_s[1, cf],
                arr * hrr - ari * hri + s_s[2, cr], arr * hri + ari * hrr + s_s[3, cr])

    zero = jnp.zeros((O, P), F32)
    lax.fori_loop(0, nc, step, (zero, zero, zero, zero))

    for g in range(O):
        y = y_s[g]
        for k in range(4):
            y = y + _dot(h_s[k, :, g, :].astype(BF16), cc_ref[g, k])
        y_s[g] = y

    for tb in range(nblk):
        yc = jnp.concatenate([y_s[g, :, tb * LANES:(tb + 1) * LANES] for g in range(O)], axis=1)
        hi = yc.astype(BF16)
        lo = (yc - hi.astype(F32)).astype(BF16)
        for jp in range(4):
            z = _dot(hi, unpack_ref[jp]) + _dot(lo, unpack_ref[jp])
            for e in range(2):
                t = 8 * tb + 2 * jp + e
                zt = z[:, e * LANES:(e + 1) * LANES]
                ol_ref[pl.ds(t, ncl, stride=L), :] = zt[:ncl]
                oc_ref[pl.ds(t, ncc, stride=L), :] = zt[ncl:]


def s5_scan(pscan, tables, n_batch, t_lat, t_ctx):
    mt, bs, cc, al = tables
    pack, unpack = _s5_select_tables()
    L, P, O = S5_CHUNK, S5_STATE, S5_OCT
    W = L * S5_CH
    ncl, ncc = t_lat // L, t_ctx // L
    nc = ncl + ncc
    ctx0 = n_batch * t_lat // t_ctx
    once = pl.Buffered(1)
    return pl.pallas_call(
        functools.partial(_s5_kernel, ncl=ncl, ncc=ncc),
        grid=(S5_GROUPS // O, n_batch),
        in_specs=[
            pl.BlockSpec((t_lat, O * S5_CH), lambda o, b: (b, o)),
            pl.BlockSpec((t_ctx, O * S5_CH), lambda o, b: (ctx0 + b, o)),
            pl.BlockSpec(pack.shape, lambda o, b: (0, 0, 0), pipeline_mode=once),
            pl.BlockSpec(unpack.shape, lambda o, b: (0, 0, 0), pipeline_mode=once),
            pl.BlockSpec((O, W, W), lambda o, b: (o, 0, 0), pipeline_mode=once),
            pl.BlockSpec((O, W, 4 * P), lambda o, b: (o, 0, 0), pipeline_mode=once),
            pl.BlockSpec((O, 4, P, W), lambda o, b: (o, 0, 0, 0), pipeline_mode=once),
            pl.BlockSpec((4, O, P), lambda o, b: (0, o, 0)),
        ],
        out_specs=[
            pl.BlockSpec((t_lat, O * S5_CH), lambda o, b: (b, o)),
            pl.BlockSpec((t_ctx, O * S5_CH), lambda o, b: (b, o)),
        ],
        out_shape=[
            jax.ShapeDtypeStruct((n_batch * t_lat, GROUP_WIDTH), F32),
            jax.ShapeDtypeStruct((n_batch * t_ctx, GROUP_WIDTH), F32),
        ],
        scratch_shapes=[
            pltpu.VMEM((O, nc, W), F32),
            pltpu.VMEM((O, nc, W), F32),
            pltpu.VMEM((4, nc, O, P), F32),
            pltpu.VMEM((4, nc, O, P), F32),
        ],
        compiler_params=_cparams(("arbitrary", "arbitrary")),
        name="s5_scan",
    )(pscan, pscan, pack, unpack, mt, bs, cc, al)


def _lru_tile(x_ref, o_ref, row0, n_rows, carry, dr, w, xs_s, a_s, b_s):
    cw_ref, cb_ref, wa_ref, ba_ref, wx_ref, bx_ref, lam_ref = w
    tt = SEQ_TILE
    reverse = dr == 1
    row0 = pl.multiple_of(row0, tt)
    has_prev = (row0 > 0).astype(F32)
    has_next = (row0 + tt < n_rows).astype(F32)
    prev0 = pl.multiple_of(jnp.maximum(row0 - 8, 0), 8)
    next0 = pl.multiple_of(jnp.minimum(row0 + tt, n_rows - 8), 8)
    xs_s[dr, 0:8, :] = x_ref[pl.ds(prev0, 8), :] * has_prev
    xs_s[dr, 8:8 + tt, :] = x_ref[pl.ds(row0, tt), :]
    xs_s[dr, 8 + tt:16 + tt, :] = x_ref[pl.ds(next0, 8), :] * has_next
    xc = cb_ref[...] + jnp.zeros((tt, LANES), F32)
    for tap in range(LRU_CONV):
        xc = xc + cw_ref[tap:tap + 1, :] * xs_s[dr, pl.ds(6 + tap, tt), :]
    xb = xc.astype(BF16)
    r = jax.nn.sigmoid(_dot(xb, wa_ref[dr]) + ba_ref[dr])
    i = jax.nn.sigmoid(_dot(xb, wx_ref[dr]) + bx_ref[dr])
    z = -lam_ref[dr]
    softplus = jnp.maximum(z, 0.0) + jnp.log1p(jnp.exp(-jnp.abs(z)))
    log_a = -LRU_C * r * softplus
    a_all = jnp.exp(log_a)
    th = jnp.tanh(log_a)
    b_all = jnp.sqrt(-2.0 * th / (1.0 - th)) * (i * xc)

    nseg = 8
    slen = tt // nseg
    pitch = LRU_SEG_PITCH
    for sgm in range(nseg):
        a_s[dr, sgm * pitch:sgm * pitch + slen, :] = a_all[sgm * slen:(sgm + 1) * slen, :]
        b_s[dr, sgm * pitch:sgm * pitch + slen, :] = b_all[sgm * slen:(sgm + 1) * slen, :]
    h = jnp.zeros((nseg, LANES), F32)
    p = jnp.ones((nseg, LANES), F32)
    for i in (range(slen - 1, -1, -1) if reverse else range(slen)):
        pos = pl.ds(i, nseg, stride=pitch)
        a = a_s[dr, pos, :]
        h = a * h + b_s[dr, pos, :]
        p = a * p
        b_s[dr, pos, :] = h
        a_s[dr, pos, :] = p
    e = carry
    ins = [None] * nseg
    for sgm in (range(nseg - 1, -1, -1) if reverse else range(nseg)):
        ins[sgm] = e
        e = h[sgm:sgm + 1, :] + p[sgm:sgm + 1, :] * e
    e_in = jnp.concatenate(ins, axis=0)
    for i in range(slen):
        pos = pl.ds(i, nseg, stride=pitch)
        b_s[dr, pos, :] = b_s[dr, pos, :] + a_s[dr, pos, :] * e_in
    for sgm in range(nseg):
        o_ref[pl.ds(row0 + sgm * slen, slen), :] = b_s[dr, sgm * pitch:sgm * pitch + slen, :]
    return e


def _lru_kernel(xl_ref, xc_ref, cw_ref, cb_ref, wa_ref, ba_ref, wx_ref, bx_ref, lam_ref,
                hfl_ref, hfc_ref, hrl_ref, hrc_ref, xs_s, a_s, b_s, *, ntl, ntc):
    w = (cw_ref, cb_ref, wa_ref, ba_ref, wx_ref, bx_ref, lam_ref)
    tt = SEQ_TILE

    def run(x_ref, of_ref, or_ref, nt, carry):
        def body(j, c):
            cf = _lru_tile(x_ref, of_ref, j * tt, nt * tt, c[0], 0, w, xs_s, a_s, b_s)
            cr = _lru_tile(x_ref, or_ref, (nt - 1 - j) * tt, nt * tt, c[1], 1, w, xs_s, a_s, b_s)
            return cf, cr
        return lax.fori_loop(0, nt, body, carry)

    zero = jnp.zeros((1, LANES), F32)
    carry = run(xc_ref, hfc_ref, hrc_ref, ntc, (zero, zero))
    run(xl_ref, hfl_ref, hrl_ref, ntl, carry)


def lru_scan(pscan, conv_w, conv_b, w_a, b_a, w_x, b_x, lam, n_batch, t_lat, t_ctx):
    tt = SEQ_TILE
    gw = GROUP_WIDTH
    nh = gw // LANES
    ntl, ntc = t_lat // tt, t_ctx // tt
    ctx0 = n_batch * t_lat // t_ctx
    per_half = LRU_BLOCKS // nh

    def halves(w):
        return jnp.stack([jnp.stack([_block_diag(w[dr, k * per_half:(k + 1) * per_half]) for k in range(nh)])
                          for dr in range(2)]).astype(BF16)

    vec = lambda a: a.reshape(2, 1, gw)
    vspec = pl.BlockSpec((2, 1, LANES), lambda b, k: (0, 0, k))
    mspec = pl.BlockSpec((2, None, LANES, LANES), lambda b, k: (0, k, 0, 0))
    lat = pl.BlockSpec((t_lat, LANES), lambda b, k: (b, k))
    ctx = pl.BlockSpec((t_ctx, LANES), lambda b, k: (b, k))
    lat_shape = jax.ShapeDtypeStruct((n_batch * t_lat, gw), F32)
    ctx_shape = jax.ShapeDtypeStruct((n_batch * t_ctx, gw), F32)
    hfl, hfc, hrl, hrc = pl.pallas_call(
        functools.partial(_lru_kernel, ntl=ntl, ntc=ntc),
        grid=(n_batch, nh),
        in_specs=[
            pl.BlockSpec((t_lat, LANES), lambda b, k: (b, nh + k)),
            pl.BlockSpec((t_ctx, LANES), lambda b, k: (ctx0 + b, nh + k)),
            pl.BlockSpec((LRU_CONV, LANES), lambda b, k: (0, k)),
            pl.BlockSpec((1, LANES), lambda b, k: (0, k)),
            mspec, vspec, mspec, vspec, vspec,
        ],
        out_specs=[lat, ctx, lat, ctx],
        out_shape=[lat_shape, ctx_shape, lat_shape, ctx_shape],
        scratch_shapes=[
            pltpu.VMEM((2, tt + 16, LANES), F32),
            pltpu.VMEM((2, 8 * LRU_SEG_PITCH, LANES), F32),
            pltpu.VMEM((2, 8 * LRU_SEG_PITCH, LANES), F32),
        ],
        compiler_params=_cparams(("parallel", "parallel")),
        name="lru_scan",
    )(pscan, pscan, conv_w, conv_b.reshape(1, gw), halves(w_a), vec(b_a), halves(w_x), vec(b_x), vec(lam))
    return [hfl, hfc], [hrl, hrc]


def _block_diag(w):
    n, d, e = w.shape
    eye = jnp.eye(n, dtype=w.dtype)
    return (eye[:, None, :, None] * w[:, :, None, :]).reshape(n * d, n * e)


def _natten_bias(rpb):
    W, KR, KC, QR, NB = GRID_W, NA_KR, NA_KC, NA_QROWS, NA_BAND
    col = np.arange(W)
    cs = np.clip(col - KC // 2, 0, W - KC)
    inwin = (col[None, :] >= cs[:, None]) & (col[None, :] < cs[:, None] + KC)
    coff = np.clip(col[None, :] - col[:, None] + (KC - 1), 0, 2 * KC - 2)
    a = np.arange(QR)[:, None]
    i = np.arange(NB)[None, :]
    first = (i - a, (i < KR) & (a >= 0))
    mid = (i - a - KR // 2, (i >= a) & (i < a + KR))
    last = (i - a - NB + QR, (i >= NB - KR) & (a >= 0))
    coh = (coff[:, :, None] == np.arange(2 * KC - 1)).astype(np.float32)
    t = jnp.einsum('hrc,qkc->hrqk', rpb.astype(F32), coh, precision=lax.Precision.HIGHEST)
    t = t * LOG2E + np.where(inwin, 0.0, NEG_BIG).astype(np.float32)
    plan = tuple(
        tuple(tuple(int(np.clip(delta[ai, ii] + KR - 1, 0, 2 * KR - 2)) if valid[ai, ii] else -1
                    for ii in range(NB)) for ai in range(QR))
        for delta, valid in (first, mid, last))

    def expand(t_ref, o_ref):
        for v in range(3):
            for ai in range(QR):
                for ii in range(NB):
                    r = plan[v][ai][ii]
                    blk = t_ref[r] if r >= 0 else jnp.full((W, W), NEG_BIG, F32)
                    o_ref[v, ai * W:(ai + 1) * W, ii * W:(ii + 1) * W] = blk

    nh = rpb.shape[0]
    return pl.pallas_call(
        expand,
        grid=(nh,),
        in_specs=[pl.BlockSpec((None, 2 * KR - 1, W, W), lambda h: (h, 0, 0, 0))],
        out_specs=pl.BlockSpec((3, None, QR * W, NB * W), lambda h: (0, h, 0, 0)),
        out_shape=jax.ShapeDtypeStruct((3, nh, QR * W, NB * W), F32),
        compiler_params=_cparams(("parallel",)),
        name="natten_bias",
    )(t)


def _natten_kernel(q_ref, k_ref, v_ref, kc_ref, vc_ref, bias_ref, o_ref, *, rows):
    W, KR, hd = GRID_W, NA_KR, HEAD_DIM
    nblk = rows // NA_QROWS
    nb = NA_BAND * W
    tq = NA_QROWS * W
    ones_col = (lax.broadcasted_iota(jnp.int32, (nb + kc_ref.shape[0], hd), 1) == 0).astype(BF16)
    for u in range(NA_STEP_BLOCKS):
        blk = pl.program_id(1) * NA_STEP_BLOCKS + u
        bs = jnp.clip(blk * NA_QROWS - KR // 2, 0, rows - NA_BAND)
        var = jnp.where(blk == 0, 0, jnp.where(blk == nblk - 1, 2, 1))
        start = pl.multiple_of(bs * W, W)
        q = (q_ref[u * tq:(u + 1) * tq, :].astype(F32) * ((hd ** -0.5) * LOG2E)).astype(BF16)
        k_all = jnp.concatenate([k_ref[pl.ds(start, nb), :], kc_ref[...]], axis=0)
        v_all = jnp.concatenate([v_ref[pl.ds(start, nb), :], vc_ref[...]], axis=0)
        for h in range(NA_HEADS):
            sl = slice(h * hd, (h + 1) * hd)
            s = _dot_nt(q[:, sl], k_all[:, sl])
            sb = s[:, :nb] + bias_ref[var, h]
            sc = s[:, nb:]
            m = jnp.maximum(jnp.max(sb, axis=-1, keepdims=True), jnp.max(sc, axis=-1, keepdims=True))
            p = jnp.concatenate([jnp.exp2(sb - m), jnp.exp2(sc - m)], axis=1).astype(BF16)
            o = _dot(p, jnp.concatenate([v_all[:, sl], ones_col], axis=1))
            o_ref[u * tq:(u + 1) * tq, sl] = o[:, :hd] / o[:, hd:hd + 1]


def natten_latent(pattn, bias, n_batch, t_lat, t_ctx):
    gw = GROUP_WIDTH
    rows = t_lat // GRID_W
    assert NA_QROWS == NA_KR // 2 and NA_BAND == NA_KR + NA_QROWS and rows % NA_QROWS == 0 and rows >= NA_BAND
    assert (rows // NA_QROWS) % NA_STEP_BLOCKS == 0
    nblk = rows // NA_QROWS // NA_STEP_BLOCKS
    tq = NA_STEP_BLOCKS * NA_QROWS * GRID_W
    ctx0 = n_batch * t_lat // t_ctx
    return pl.pallas_call(
        functools.partial(_natten_kernel, rows=rows),
        grid=(n_batch, nblk),
        in_specs=[
            pl.BlockSpec((tq, gw), lambda b, r: (b * nblk + r, 2)),
            pl.BlockSpec((t_lat, gw), lambda b, r: (b, 3)),
            pl.BlockSpec((t_lat, gw), lambda b, r: (b, 4)),
            pl.BlockSpec((t_ctx, gw), lambda b, r: (ctx0 + b, 3)),
            pl.BlockSpec((t_ctx, gw), lambda b, r: (ctx0 + b, 4)),
            pl.BlockSpec(bias.shape, lambda b, r: (0, 0, 0, 0)),
        ],
        out_specs=pl.BlockSpec((tq, gw), lambda b, r: (b * nblk + r, 0)),
        out_shape=jax.ShapeDtypeStruct((n_batch * t_lat, gw), F32),
        compiler_params=_cparams(("parallel", "arbitrary")),
        name="natten_latent",
    )(pattn, pattn, pattn, pattn, pattn, bias)


def _ctx_attn_kernel(q_ref, k_ref, v_ref, o_ref):
    hd = HEAD_DIM
    q, k, v = q_ref[...], k_ref[...], v_ref[...]
    for h in range(NA_HEADS):
        sl = slice(h * hd, (h + 1) * hd)
        s = _dot_nt(q[:, sl], k[:, sl]) * (hd ** -0.5)
        m = jnp.max(s, axis=-1, keepdims=True)
        p = jnp.exp(s - m)
        l = jnp.sum(p, axis=-1, keepdims=True)
        o_ref[:, sl] = _dot(p.astype(BF16), v[:, sl]) / l


def natten_context(pattn, n_batch, t_lat, t_ctx):
    gw = GROUP_WIDTH
    ctx0 = n_batch * t_lat // t_ctx
    return pl.pallas_call(
        _ctx_attn_kernel,
        grid=(n_batch,),
        in_specs=[pl.BlockSpec((t_ctx, gw), lambda b, c=c: (ctx0 + b, c)) for c in (2, 3, 4)],
        out_specs=pl.BlockSpec((t_ctx, gw), lambda b: (b, 0)),
        out_shape=jax.ShapeDtypeStruct((n_batch * t_ctx, gw), F32),
        compiler_params=_cparams(("parallel",)),
        name="natten_context",
    )(pattn, pattn, pattn)


def _rope_tables(t_lat, t_ctx):
    half = HEAD_DIM // 4
    freqs = ROPE_BASE ** (-np.arange(half, dtype=np.float64) / half)
    t = np.arange(t_lat)
    ang_r = (t // GRID_W)[:, None] * freqs[None, :]
    ang_c = (t % GRID_W)[:, None] * freqs[None, :]
    ang = np.concatenate([ang_r, ang_r, ang_c, ang_c], axis=1)
    cos = np.concatenate([np.cos(ang), np.ones((t_ctx, HEAD_DIM))], axis=0)
    sin = np.concatenate([np.sin(ang), np.zeros((t_ctx, HEAD_DIM))], axis=0)
    return (np.tile(cos, (1, GQA_Q_HEADS)).astype(np.float32), np.tile(sin, (1, GQA_Q_HEADS)).astype(np.float32))


def _gqa_prep_kernel(x_ref, cos_ref, sin_ref, gq_ref, gk_ref, gm_ref, q_ref, k_ref, v_ref):
    hd = HEAD_DIM
    nq, nk = GQA_Q_HEADS * hd, GQA_KV_HEADS * hd
    x = x_ref[...].astype(F32)
    cos, sin = cos_ref[...], sin_ref[...]
    gm = gm_ref[...]

    def norm_rope(t, gain, width):
        ms = _dot((t * t).astype(BF16), gm[:width, :width])
        tn = t * lax.rsqrt(ms + EPS) * gain
        lane = lax.broadcasted_iota(jnp.int32, tn.shape, 1)
        first = (lane % (hd // 2)) < (hd // 4)
        rot = jnp.where(first, -pltpu.roll(tn, width - hd // 4, 1), pltpu.roll(tn, hd // 4, 1))
        return tn * cos[:, :width] + rot * sin[:, :width]

    q = norm_rope(x[:, :nq], gq_ref[...], nq) * ((hd ** -0.5) * LOG2E)
    kt = norm_rope(x[:, nq:nq + nk], gk_ref[...], nk).T
    v = x_ref[:, nq + nk:nq + 2 * nk]
    for h in range(GQA_Q_HEADS):
        q_ref[h] = q[:, h * hd:(h + 1) * hd].astype(BF16)
    ones_col = (lax.broadcasted_iota(jnp.int32, (v.shape[0], hd), 1) == 0).astype(BF16)
    for h in range(GQA_KV_HEADS):
        k_ref[h] = kt[h * hd:(h + 1) * hd, :].astype(BF16)
        v_ref[h] = jnp.concatenate([v[:, h * hd:(h + 1) * hd], ones_col], axis=1)


def gqa_prep(pattn, gq, gk, n_batch, t_lat, t_ctx):
    tt = SEQ_TILE
    assert tt == KEY_CHUNK
    hd = HEAD_DIM
    ntl, ntc = t_lat // tt, t_ctx // tt
    s_len = t_lat + t_ctx
    cos, sin = _rope_tables(t_lat, t_ctx)
    nq = GQA_Q_HEADS * hd
    gmat = _block_diag(jnp.full((GQA_Q_HEADS, hd, hd), 1.0 / hd, F32)).astype(BF16)

    def pos(i):
        is_ctx = i >= n_batch * ntl
        c = i - n_batch * ntl
        return jnp.where(is_ctx, ntl + c % ntc, i % ntl)

    def bat(i):
        is_ctx = i >= n_batch * ntl
        return jnp.where(is_ctx, (i - n_batch * ntl) // ntc, i // ntl)

    return pl.pallas_call(
        _gqa_prep_kernel,
        grid=(n_batch * (ntl + ntc),),
        in_specs=[
            pl.BlockSpec((tt, 2 * nq), lambda i: (i, 0)),
            pl.BlockSpec((tt, nq), lambda i: (pos(i), 0)),
            pl.BlockSpec((tt, nq), lambda i: (pos(i), 0)),
            pl.BlockSpec((1, nq), lambda i: (0, 0)),
            pl.BlockSpec((1, nq // 2), lambda i: (0, 0)),
            pl.BlockSpec((nq, nq), lambda i: (0, 0)),
        ],
        out_specs=[
            pl.BlockSpec((None, GQA_Q_HEADS, tt, hd), lambda i: (bat(i), 0, pos(i), 0)),
            pl.BlockSpec((None, GQA_KV_HEADS, None, hd, tt), lambda i: (bat(i), 0, pos(i), 0, 0)),
            pl.BlockSpec((None, GQA_KV_HEADS, tt, 2 * hd), lambda i: (bat(i), 0, pos(i), 0)),
        ],
        out_shape=[
            jax.ShapeDtypeStruct((n_batch, GQA_Q_HEADS, s_len, hd), BF16),
            jax.ShapeDtypeStruct((n_batch, GQA_KV_HEADS, s_len // tt, hd, tt), BF16),
            jax.ShapeDtypeStruct((n_batch, GQA_KV_HEADS, s_len, 2 * hd), BF16),
        ],
        compiler_params=_cparams(("parallel",)),
        name="gqa_prep",
    )(pattn, cos, sin, jnp.tile(gq.astype(F32), GQA_Q_HEADS).reshape(1, nq),
      jnp.tile(gk.astype(F32), GQA_KV_HEADS).reshape(1, nq // 2), gmat)


def _gqa_kernel(q_ref, kt_ref, v_ref, o_ref, s_s, m_s, acc_s, *, unroll):
    g, tq, hd = q_ref.shape
    nch, _, ck = kt_ref.shape
    nl = ck // LANES
    q = q_ref[...].reshape(g * tq, hd)
    m_s[...] = jnp.full_like(m_s, -jnp.inf)

    def scores(c, carry):
        s = _dot(q, kt_ref[c])
        s_s[c] = s
        fold = s[:, 0:LANES]
        for j in range(1, nl):
            fold = jnp.maximum(fold, s[:, j * LANES:(j + 1) * LANES])
        m_s[...] = jnp.maximum(m_s[...], fold)
        return carry

    lax.fori_loop(0, nch, scores, 0, unroll=True)
    m_s[...] = jnp.broadcast_to(jnp.max(m_s[...], axis=-1, keepdims=True), m_s.shape)
    acc_s[...] = jnp.zeros_like(acc_s)

    def values(grp, carry):
        m = m_s[...]
        ps = []
        for u in range(unroll):
            s = s_s[grp * unroll + u]
            ps += [jnp.exp2(s[:, j * LANES:(j + 1) * LANES] - m).astype(BF16) for j in range(nl)]
        off = pl.multiple_of(grp * (unroll * ck), unroll * ck)
        acc_s[...] += _dot(jnp.concatenate(ps, axis=1), v_ref[pl.ds(off, unroll * ck), :])
        return carry

    lax.fori_loop(0, nch // unroll, values, 0)
    acc = acc_s[...]
    o = acc[:, :hd] / acc[:, hd:hd + 1]
    o_ref[...] = jnp.concatenate([o[i * tq:(i + 1) * tq] for i in range(g)], axis=1)


def gqa_attention(q, kt, v, tq, q_blk0, nq, c_blk, nch):
    n_batch = q.shape[0]
    hd = HEAD_DIM
    g = GQA_GROUP
    ck = KEY_CHUNK
    unroll = next(u for u in (11, 3, 1) if nch % u == 0)
    return pl.pallas_call(
        functools.partial(_gqa_kernel, unroll=unroll),
        grid=(n_batch, GQA_KV_HEADS, nq),
        in_specs=[
            pl.BlockSpec((None, g, tq, hd), lambda b, h, i: (b, h, q_blk0 + i, 0)),
            pl.BlockSpec((None, None, nch, hd, ck), lambda b, h, i: (b, h, c_blk, 0, 0)),
            pl.BlockSpec((None, None, nch * ck, 2 * hd), lambda b, h, i: (b, h, c_blk, 0)),
        ],
        out_specs=pl.BlockSpec((tq, g * hd), lambda b, h, i: (b * nq + i, h)),
        out_shape=jax.ShapeDtypeStruct((n_batch * nq * tq, GQA_Q_HEADS * hd), F32),
        scratch_shapes=[
            pltpu.VMEM((nch, g * tq, ck), F32),
            pltpu.VMEM((g * tq, LANES), F32),
            pltpu.VMEM((g * tq, 2 * hd), F32),
        ],
        compiler_params=_cparams(("parallel", "parallel", "arbitrary")),
        name="gqa_attention",
    )(q, kt, v)


def kernel(x, c, ctx, c_ctx, w_ada, b_ada, g_ffn1, w_ffn1_in, w_ffn1_out, g_mix, w_in, w_out, s5_lambda_re, s5_lambda_im, s5_log_step, s5_b_re, s5_b_im, s5_c_re, s5_c_im, s5_d, s5_w_glu, na_rpb, gqa_q_norm, gqa_k_norm, lru_conv_w, lru_conv_b, lru_w_a, lru_b_a, lru_w_x, lru_b_x, lru_lambda, g_ffn2, w_ffn2_in, w_ffn2_out, g_final):
    n_batch, t_lat, d = x.shape
    t_ctx = ctx.shape[1]
    depth = w_ada.shape[0]
    assert d == D_MODEL and t_lat % TOKEN_TILE == 0 and (n_batch * t_ctx) % TOKEN_TILE == 0
    assert t_lat % t_ctx == 0 and t_ctx % SEQ_TILE == 0 and t_lat % FLASH_TQ == 0
    n_lat = n_batch * t_lat
    n_all = n_lat + n_batch * t_ctx
    tiles_per_type = t_lat // TOKEN_TILE
    assert (n_batch * t_ctx) // TOKEN_TILE <= tiles_per_type

    c8 = jnp.zeros((8, D_MODEL), F32).at[:n_batch].set(c.astype(F32)).at[n_batch].set(c_ctx.astype(F32))
    mods_all = ada_mods(c8, w_ada, b_ada).reshape(depth, 8, N_MOD, D_MODEL)

    s_len = t_lat + t_ctx
    n_lat_tiles = n_lat // TOKEN_TILE
    n_ctx_tiles = (n_all - n_lat) // TOKEN_TILE
    hs = [x.reshape(n_lat, D_MODEL), ctx.reshape(n_batch * t_ctx, D_MODEL)]
    w1i, w1o = w_ffn1_in.astype(BF16), w_ffn1_out.astype(BF16)
    w2i, w2o = w_ffn2_in.astype(BF16), w_ffn2_out.astype(BF16)
    w_in_b, w_out_b, w_glu_b = w_in.astype(BF16), w_out.astype(BF16), s5_w_glu.astype(BF16)

    for l in range(depth):
        need_ctx = l < depth - 1
        mods = mods_all[l]
        h, pscan, pattn = ffn_in_projection(hs, mods, g_ffn1[l], w1i, w1o, g_mix[l], w_in_b, l,
                                            n_lat_tiles, n_ctx_tiles, tiles_per_type)

        tables = _s5_tables(s5_lambda_re[l], s5_lambda_im[l], s5_log_step[l], s5_b_re[l], s5_b_im[l],
                            s5_c_re[l], s5_c_im[l])
        ys5 = list(s5_scan(pscan, tables, n_batch, t_lat, t_ctx))

        hf, hr = lru_scan(pscan, lru_conv_w[l], lru_conv_b[l], lru_w_a[l], lru_b_a[l], lru_w_x[l], lru_b_x[l],
                          lru_lambda[l], n_batch, t_lat, t_ctx)

        yb = [natten_latent(pattn, _natten_bias(na_rpb[l]), n_batch, t_lat, t_ctx)]

        qh, kth, vh = gqa_prep(pattn, gqa_q_norm[l], gqa_k_norm[l], n_batch, t_lat, t_ctx)
        yc = [gqa_attention(qh, kth, vh, FLASH_TQ, 0, t_lat // FLASH_TQ, 0, s_len // KEY_CHUNK)]

        if need_ctx:
            yb.append(natten_context(pattn, n_batch, t_lat, t_ctx))
            yc.append(gqa_attention(qh, kth, vh, t_ctx, t_lat // t_ctx, 1, t_lat // t_ctx, t_ctx // KEY_CHUNK))
        else:
            ys5, hf, hr = ys5[:1], hf[:1], hr[:1]

        h = mix_out_ffn(h, mods, ys5, yb, yc, pscan, hf, hr, s5_d[l], w_glu_b, w_out_b, g_ffn2[l], w2i, w2o, l,
                        n_lat_tiles,
                        n_ctx_tiles if need_ctx else 0, tiles_per_type, g_final=None if need_ctx else g_final)
        hs = [h]
    return h.reshape(n_batch, t_lat, D_MODEL)
```

```python
import functools
import math

import numpy as np
import jax
import jax.numpy as jnp
from jax import lax
from jax.experimental import pallas as pl
from jax.experimental.pallas import tpu as pltpu

F32 = jnp.float32
BF16 = jnp.bfloat16

D_MODEL = 1024
GRID_W = 64
HEAD_DIM = 64
GROUP_WIDTH = D_MODEL // 4
S5_CH = 16
S5_GROUPS = GROUP_WIDTH // S5_CH
S5_STATE = 64
NA_HEADS = GROUP_WIDTH // HEAD_DIM
NA_KR = 8
NA_KC = 16
GQA_Q_HEADS = GROUP_WIDTH // HEAD_DIM
GQA_KV_HEADS = GQA_Q_HEADS // 2
GQA_GROUP = GQA_Q_HEADS // GQA_KV_HEADS
LRU_BLOCKS = GROUP_WIDTH // HEAD_DIM
LRU_CONV = 4
LRU_C = 8.0
D_FF = ((8 * D_MODEL // 3 + 127) // 128) * 128
FFN_RES = 0.5
ROPE_BASE = 10000.0
EPS = 1e-6
N_MOD = 9

LANES = 128
TOKEN_TILE = 512
SEQ_TILE = 256
LRU_SEG_PITCH = SEQ_TILE // 8 + 4
S5_CHUNK = 32
S5_OCT = 8
MXU_DEPTH = 256
FF_CHUNKS = ((0, 6 * MXU_DEPTH), (6 * MXU_DEPTH, D_FF))
FLASH_TQ = 512
KEY_CHUNK = 256
NA_QROWS = NA_KR // 2
NA_BAND = NA_KR + NA_QROWS
NA_STEP_BLOCKS = 4
LOG2E = math.log2(math.e)
NEG_BIG = -1e30
V7X_VMEM_BYTES = 64 * 1024 * 1024
VMEM_LIMIT = V7X_VMEM_BYTES - 8 * 1024 * 1024


def _cparams(sem):
    return pltpu.CompilerParams(dimension_semantics=sem, vmem_limit_bytes=VMEM_LIMIT)


def _dot(a, b):
    return jnp.dot(a, b, preferred_element_type=F32)


def _dot_nt(a, b):
    return lax.dot_general(a, b, (((1,), (1,)), ((), ())), preferred_element_type=F32)


def _ada_kernel(c_ref, w_ref, b_ref, o_ref):
    c = c_ref[...]
    s = c * jax.nn.sigmoid(c)
    o_ref[...] = _dot(s.astype(BF16), w_ref[...].astype(BF16)) + b_ref[...]


def ada_mods(c8, w_ada, b_ada):
    L = w_ada.shape[0]
    tn = D_MODEL
    return pl.pallas_call(
        _ada_kernel,
        grid=(L, N_MOD * D_MODEL // tn),
        in_specs=[
            pl.BlockSpec((8, D_MODEL), lambda l, j: (0, 0)),
            pl.BlockSpec((None, D_MODEL, tn), lambda l, j: (l, 0, j)),
            pl.BlockSpec((None, 1, tn), lambda l, j: (l, 0, j)),
        ],
        out_specs=pl.BlockSpec((None, 8, tn), lambda l, j: (l, 0, j)),
        out_shape=jax.ShapeDtypeStruct((L, 8, N_MOD * D_MODEL), F32),
        compiler_params=_cparams(("parallel", "parallel")),
        name="ada_mods",
    )(c8, w_ada, b_ada.reshape(L, 1, N_MOD * D_MODEL))


def _norm_mod(x, g, shift, scale):
    ms = jnp.mean(x * x, axis=-1, keepdims=True)
    y = x * lax.rsqrt(ms + EPS) * g
    return y * (1.0 + scale) + shift


def _ffn_core(x, mod_ref, g, wi_ref, wo_ref, k0):
    y = _norm_mod(x, g, mod_ref[k0:k0 + 1, :], mod_ref[k0 + 1:k0 + 2, :]).astype(BF16)
    acc = jnp.zeros(x.shape, F32)
    for lo, hi in FF_CHUNKS:
        a = _dot(y, wi_ref[:, lo:hi])
        b = _dot(y, wi_ref[:, D_FF + lo:D_FF + hi])
        h = (a * jax.nn.sigmoid(a) * b).astype(BF16)
        acc = acc + _dot(h, wo_ref[lo:hi, :])
    return x + FFN_RES * mod_ref[k0 + 2:k0 + 3, :] * acc


def _resident(shape, layer):
    return pl.BlockSpec((None,) + shape, lambda i: (layer,) + (0,) * len(shape), pipeline_mode=pl.Buffered(1))


def _row_specs(n_lat_tiles, n_ctx_tiles, width, col=0):
    lat = pl.BlockSpec((TOKEN_TILE, width), lambda i: (jnp.minimum(i, n_lat_tiles - 1), col))
    if n_ctx_tiles == 0:
        return [lat]
    ctx = pl.BlockSpec((TOKEN_TILE, width), lambda i: (jnp.clip(i - n_lat_tiles, 0, n_ctx_tiles - 1), col))
    return [lat, ctx]


def _pick(refs, n_lat_tiles):
    if len(refs) == 1:
        return refs[0][...]
    return jnp.where(pl.program_id(0) >= n_lat_tiles, refs[1][...], refs[0][...])


def _ffn_in_kernel(*refs, n_src, n_lat_tiles):
    x_refs = refs[:n_src]
    mod_ref, g_ref, wi_ref, wo_ref, gm_ref, win_ref, h_ref, ps_ref, pa_ref = refs[n_src:]
    h = _ffn_core(_pick(x_refs, n_lat_tiles), mod_ref, g_ref[...], wi_ref, wo_ref, 0)
    h_ref[...] = h
    y = _norm_mod(h, gm_ref[...], mod_ref[3:4, :], mod_ref[4:5, :]).astype(BF16)
    p = _dot(y, win_ref[...])
    gw = GROUP_WIDTH
    ps_ref[...] = jnp.concatenate([p[:, 0:gw], p[:, 6 * gw:8 * gw]], axis=1)
    pa_ref[...] = jnp.concatenate([p[:, 4 * gw:6 * gw], p[:, gw:4 * gw]], axis=1).astype(BF16)


def ffn_in_projection(xs, mods, g, w_i, w_o, g_mix, w_in_l, layer, n_lat_tiles, n_ctx_tiles, tiles_per_type):
    nt = n_lat_tiles + n_ctx_tiles
    n = nt * TOKEN_TILE
    n_scan = 3 * GROUP_WIDTH
    n_attn = w_in_l.shape[-1] - n_scan
    x_specs = (_row_specs(n_lat_tiles, n_ctx_tiles, D_MODEL) if len(xs) == 2
               else [pl.BlockSpec((TOKEN_TILE, D_MODEL), lambda i: (i, 0))])
    vec = pl.BlockSpec((1, D_MODEL), lambda i: (0, 0))
    return pl.pallas_call(
        functools.partial(_ffn_in_kernel, n_src=len(xs), n_lat_tiles=n_lat_tiles),
        grid=(nt,),
        in_specs=x_specs + [
            pl.BlockSpec((None, N_MOD, D_MODEL), lambda i: (i // tiles_per_type, 0, 0)),
            vec, _resident((D_MODEL, 2 * D_FF), layer), _resident((D_FF, D_MODEL), layer),
            vec, _resident((D_MODEL, n_scan + n_attn), layer),
        ],
        out_specs=[
            pl.BlockSpec((TOKEN_TILE, D_MODEL), lambda i: (i, 0)),
            pl.BlockSpec((TOKEN_TILE, n_scan), lambda i: (i, 0)),
            pl.BlockSpec((TOKEN_TILE, n_attn), lambda i: (i, 0)),
        ],
        out_shape=[
            jax.ShapeDtypeStruct((n, D_MODEL), F32),
            jax.ShapeDtypeStruct((n, n_scan), F32),
            jax.ShapeDtypeStruct((n, n_attn), BF16),
        ],
        compiler_params=_cparams(("parallel",)),
        name="ffn_in_projection",
    )(*xs, mods, g.reshape(1, D_MODEL), w_i, w_o, g_mix.reshape(1, D_MODEL), w_in_l)


def _mix_ffn_kernel(*refs, n_src, n_lat_tiles, final):
    gw = GROUP_WIDTH
    h_ref, mod_ref = refs[:2]
    ys5_refs, yb_refs, yc_refs, hf_refs, hr_refs = (refs[2 + k * n_src:2 + (k + 1) * n_src] for k in range(5))
    rest = refs[2 + 5 * n_src:]
    u_ref, gl_ref, dsk_ref, wglu_ref, wout_ref, g_ref, wi_ref, wo_ref = rest[:8]
    o_ref = rest[-1]
    ya = jax.nn.gelu(_pick(ys5_refs, n_lat_tiles) + dsk_ref[...] * u_ref[...])
    ya = ya * jax.nn.sigmoid(_dot(ya.astype(BF16), wglu_ref[...]))
    yd = (_pick(hf_refs, n_lat_tiles) + _pick(hr_refs, n_lat_tiles)) * jax.nn.gelu(gl_ref[...])
    acc = _dot(ya.astype(BF16), wout_ref[0:gw, :])
    acc = acc + _dot(_pick(yb_refs, n_lat_tiles).astype(BF16), wout_ref[gw:2 * gw, :])
    acc = acc + _dot(_pick(yc_refs, n_lat_tiles).astype(BF16), wout_ref[2 * gw:3 * gw, :])
    acc = acc + _dot(yd.astype(BF16), wout_ref[3 * gw:4 * gw, :])
    h = h_ref[...] + mod_ref[5:6, :] * acc
    out = _ffn_core(h, mod_ref, g_ref[...], wi_ref, wo_ref, 6)
    if final:
        ms = jnp.mean(out * out, axis=-1, keepdims=True)
        out = out * lax.rsqrt(ms + EPS) * rest[8][...]
    o_ref[...] = out


def mix_out_ffn(h, mods, ys5, yb, yc, pscan, hf, hr, d_skip, w_glu, w_out, g, w_i, w_o, layer,
                n_lat_tiles, n_ctx_tiles, tiles_per_type, g_final=None):
    gw = GROUP_WIDTH
    final = g_final is not None
    nt = n_lat_tiles + n_ctx_tiles
    row = lambda c: pl.BlockSpec((TOKEN_TILE, gw), lambda i: (i, c))
    vec = pl.BlockSpec((1, D_MODEL), lambda i: (0, 0))
    mixer_specs = _row_specs(n_lat_tiles, n_ctx_tiles, gw)
    in_specs = ([pl.BlockSpec((TOKEN_TILE, D_MODEL), lambda i: (i, 0)),
                 pl.BlockSpec((None, N_MOD, D_MODEL), lambda i: (i // tiles_per_type, 0, 0))]
                + mixer_specs * 5
                + [row(0), row(2),
                   pl.BlockSpec((1, gw), lambda i: (0, 0)), _resident((gw, gw), layer),
                   _resident((D_MODEL, D_MODEL), layer),
                   vec, _resident((D_MODEL, 2 * D_FF), layer), _resident((D_FF, D_MODEL), layer)])
    args = [h, mods, *ys5, *yb, *yc, *hf, *hr, pscan, pscan, d_skip.reshape(1, gw), w_glu, w_out,
            g.reshape(1, D_MODEL), w_i, w_o]
    if final:
        in_specs.append(vec)
        args.append(g_final.reshape(1, D_MODEL))
    return pl.pallas_call(
        functools.partial(_mix_ffn_kernel, n_src=len(ys5), n_lat_tiles=n_lat_tiles, final=final),
        grid=(nt,),
        in_specs=in_specs,
        out_specs=pl.BlockSpec((TOKEN_TILE, D_MODEL), lambda i: (i, 0)),
        out_shape=jax.ShapeDtypeStruct((nt * TOKEN_TILE, D_MODEL), F32),
        compiler_params=_cparams(("parallel",)),
        name="mix_out_ffn",
    )(*args)


def _s5_tables(lam_re, lam_im, log_step, b_re, b_im, c_re, c_im):
    L = S5_CHUNK
    G, P, H = S5_GROUPS, S5_STATE, S5_CH
    hp = lax.Precision.HIGHEST
    W = L * H
    lr, li = lam_re.astype(F32), lam_im.astype(F32)
    dt = jnp.exp(log_step.astype(F32))[..., None]
    mag = jnp.exp(lr * dt)
    ar, ai = mag * jnp.cos(li * dt), mag * jnp.sin(li * dt)
    den = lr * lr + li * li
    fr = ((ar - 1) * lr + ai * li) / den
    fi = (ai * lr - (ar - 1) * li) / den
    br, bi = b_re.astype(F32), b_im.astype(F32)
    bbr = jnp.swapaxes(fr[..., None] * br - fi[..., None] * bi, 2, 3)
    bbi = jnp.swapaxes(fr[..., None] * bi + fi[..., None] * br, 2, 3)
    cr, ci = c_re.astype(F32), c_im.astype(F32)
    tau = jnp.arange(L + 1, dtype=F32)[None, None, :, None]
    pmag = jnp.exp(tau * (lr * dt)[:, :, None, :])
    pang = tau * (li * dt)[:, :, None, :]
    pw_r, pw_i = pmag * jnp.cos(pang), pmag * jnp.sin(pang)
    abr = pw_r[:, :, :L, None, :] * bbr[:, :, None] - pw_i[:, :, :L, None, :] * bbi[:, :, None]
    abi = pw_r[:, :, :L, None, :] * bbi[:, :, None] + pw_i[:, :, :L, None, :] * bbr[:, :, None]
    mt = _s5_toeplitz(abr.reshape(2, G, W, P), abi.reshape(2, G, W, P), cr, ci)

    def rows(x, flip):
        return (x[:, ::-1] if flip else x).reshape(G, W, P)

    bs = jnp.concatenate([rows(abr[0], True), rows(abi[0], True),
                          rows(abr[1], False), rows(abi[1], False)], axis=-1).astype(BF16)
    eo = np.tile(np.eye(H, dtype=np.float32), (1, L))
    et = np.repeat(np.eye(L, dtype=np.float32), H, axis=1)
    qr = jnp.stack([pw_r[0, :, 1:], pw_r[1, :, :0:-1]])
    qi = jnp.stack([pw_i[0, :, 1:], pw_i[1, :, :0:-1]])
    crx = jnp.einsum('dgop,ox->dgpx', cr, eo, precision=hp)
    cix = jnp.einsum('dgop,ox->dgpx', ci, eo, precision=hp)
    qrx = jnp.einsum('dgtp,tx->dgpx', qr, et, precision=hp)
    qix = jnp.einsum('dgtp,tx->dgpx', qi, et, precision=hp)
    ccr = crx * qrx - cix * qix
    cci = -crx * qix - cix * qrx
    cc = jnp.stack([ccr[0], cci[0], ccr[1], cci[1]], axis=1).astype(BF16)
    al = jnp.stack([pw_r[0, :, L], pw_i[0, :, L], pw_r[1, :, L], pw_i[1, :, L]])
    return mt, bs, cc, al


def _s5_toeplitz_kernel(ar_ref, ai_ref, cr_ref, ci_ref, o_ref):
    L, H = S5_CHUNK, S5_CH
    W = L * H

    def split(x):
        hi = x.astype(BF16)
        return hi, (x - hi.astype(F32)).astype(BF16)

    panels = []
    for d in range(2):
        kk = None
        for a_ref, c_ref in ((ar_ref, cr_ref), (ai_ref, ci_ref)):
            a_hi, a_lo = split(a_ref[d])
            c_hi, c_lo = split(c_ref[d])
            t = _dot_nt(a_hi, c_hi) + _dot_nt(a_hi, c_lo) + _dot_nt(a_lo, c_hi)
            kk = t if kk is None else kk - t
        order = range(L) if d == 0 else range(L - 1, -1, -1)
        panels.append(jnp.concatenate([kk[l * H:(l + 1) * H, :] for l in order], axis=1))
    pf, pr = panels
    lane = lax.broadcasted_iota(jnp.int32, (H, W), 1)
    for s in range(L):
        f = pf if s == 0 else jnp.where(lane >= s * H, pltpu.roll(pf, s * H, 1), 0.0)
        back = (L - 1 - s) * H
        r = pr if back == 0 else jnp.where(lane < (s + 1) * H, pltpu.roll(pr, W - back, 1), 0.0)
        o_ref[s * H:(s + 1) * H, :] = (f + r).astype(BF16)


def _s5_toeplitz(abr, abi, cr, ci):
    _, G, W, P = abr.shape
    H = cr.shape[2]
    a_spec = pl.BlockSpec((2, None, W, P), lambda g: (0, g, 0, 0))
    c_spec = pl.BlockSpec((2, None, H, P), lambda g: (0, g, 0, 0))
    return pl.pallas_call(
        _s5_toeplitz_kernel,
        grid=(G,),
        in_specs=[a_spec, a_spec, c_spec, c_spec],
        out_specs=pl.BlockSpec((None, W, W), lambda g: (g, 0, 0)),
        out_shape=jax.ShapeDtypeStruct((G, W, W), BF16),
        compiler_params=_cparams(("parallel",)),
        name="s5_toeplitz",
    )(abr, abi, cr, ci)


def _s5_select_tables():
    H, O = S5_CH, S5_OCT
    pack = np.zeros((O // 2, O * LANES, 2 * LANES), np.float32)
    unpack = np.zeros((O // 2, O * LANES, 2 * LANES), np.float32)
    for j in range(8):
        for g in range(O):
            for h in range(H):
                pack[g // 2, j * LANES + g * H + h, (g % 2) * LANES + j * H + h] = 1.0
                unpack[j // 2, g * LANES + j * H + h, (j % 2) * LANES + g * H + h] = 1.0
    return jnp.asarray(pack, BF16), jnp.asarray(unpack, BF16)


def _s5_kernel(ul_ref, uc_ref, pack_ref, unpack_ref, mt_ref, bs_ref, cc_ref, al_ref, ol_ref, oc_ref,
               ug_s, y_s, s_s, h_s, *, ncl, ncc):
    L, P, O = S5_CHUNK, S5_STATE, S5_OCT
    nc = ncl + ncc
    nblk = L // 8

    for tb in range(nblk):
        zs = [jnp.concatenate([ul_ref[pl.ds(8 * tb + j, ncl, stride=L), :],
                               uc_ref[pl.ds(8 * tb + j, ncc, stride=L), :]], axis=0) for j in range(8)]
        zc = jnp.concatenate(zs, axis=1).astype(BF16)
        for gp in range(O // 2):
            r = _dot(zc, pack_ref[gp])
            ug_s[2 * gp, :, tb * LANES:(tb + 1) * LANES] = r[:, :LANES]
            ug_s[2 * gp + 1, :, tb * LANES:(tb + 1) * LANES] = r[:, LANES:]

    for g in range(O):
        ug = ug_s[g].astype(BF16)
        y_s[g] = _dot(ug, mt_ref[g])
        st = _dot(ug, bs_ref[g])
        for k in range(4):
            s_s[k, :, g, :] = st[:, k * P:(k + 1) * P]

    afr, afi, arr, ari = al_ref[0], al_ref[1], al_ref[2], al_ref[3]

    def step(j, carry):
        hfr, hfi, hrr, hri = carry
        cf = jnp.where(j < ncc, ncl + j, j - ncc)
        cr = nc - 1 - j
        h_s[0, cf] = hfr
        h_s[1, cf] = hfi
        h_s[2, cr] = hrr
        h_s[3, cr] = hri
        return (afr * hfr - afi * hfi + s_s[0, cf], afr * hfi + afi * hfr + s_s[1, cf],
                arr * hrr - ari * hri + s_s[2, cr], arr * hri + ari * hrr + s_s[3, cr])

    zero = jnp.zeros((O, P), F32)
    lax.fori_loop(0, nc, step, (zero, zero, zero, zero))

    for g in range(O):
        y = y_s[g]
        for k in range(4):
            y = y + _dot(h_s[k, :, g, :].astype(BF16), cc_ref[g, k])
        y_s[g] = y

    for tb in range(nblk):
        yc = jnp.concatenate([y_s[g, :, tb * LANES:(tb + 1) * LANES] for g in range(O)], axis=1)
        hi = yc.astype(BF16)
        lo = (yc - hi.astype(F32)).astype(BF16)
        for jp in range(4):
            z = _dot(hi, unpack_ref[jp]) + _dot(lo, unpack_ref[jp])
            for e in range(2):
                t = 8 * tb + 2 * jp + e
                zt = z[:, e * LANES:(e + 1) * LANES]
                ol_ref[pl.ds(t, ncl, stride=L), :] = zt[:ncl]
                oc_ref[pl.ds(t, ncc, stride=L), :] = zt[ncl:]


def s5_scan(pscan, tables, n_batch, t_lat, t_ctx):
    mt, bs, cc, al = tables
    pack, unpack = _s5_select_tables()
    L, P, O = S5_CHUNK, S5_STATE, S5_OCT
    W = L * S5_CH
    ncl, ncc = t_lat // L, t_ctx // L
    nc = ncl + ncc
    ctx0 = n_batch * t_lat // t_ctx
    once = pl.Buffered(1)
    return pl.pallas_call(
        functools.partial(_s5_kernel, ncl=ncl, ncc=ncc),
        grid=(S5_GROUPS // O, n_batch),
        in_specs=[
            pl.BlockSpec((t_lat, O * S5_CH), lambda o, b: (b, o)),
            pl.BlockSpec((t_ctx, O * S5_CH), lambda o, b: (ctx0 + b, o)),
            pl.BlockSpec(pack.shape, lambda o, b: (0, 0, 0), pipeline_mode=once),
            pl.BlockSpec(unpack.shape, lambda o, b: (0, 0, 0), pipeline_mode=once),
            pl.BlockSpec((O, W, W), lambda o, b: (o, 0, 0), pipeline_mode=once),
            pl.BlockSpec((O, W, 4 * P), lambda o, b: (o, 0, 0), pipeline_mode=once),
            pl.BlockSpec((O, 4, P, W), lambda o, b: (o, 0, 0, 0), pipeline_mode=once),
            pl.BlockSpec((4, O, P), lambda o, b: (0, o, 0)),
        ],
        out_specs=[
            pl.BlockSpec((t_lat, O * S5_CH), lambda o, b: (b, o)),
            pl.BlockSpec((t_ctx, O * S5_CH), lambda o, b: (b, o)),
        ],
        out_shape=[
            jax.ShapeDtypeStruct((n_batch * t_lat, GROUP_WIDTH), F32),
            jax.ShapeDtypeStruct((n_batch * t_ctx, GROUP_WIDTH), F32),
        ],
        scratch_shapes=[
            pltpu.VMEM((O, nc, W), F32),
            pltpu.VMEM((O, nc, W), F32),
            pltpu.VMEM((4, nc, O, P), F32),
            pltpu.VMEM((4, nc, O, P), F32),
        ],
        compiler_params=_cparams(("arbitrary", "arbitrary")),
        name="s5_scan",
    )(pscan, pscan, pack, unpack, mt, bs, cc, al)


def _lru_tile(x_ref, o_ref, row0, n_rows, carry, dr, w, xs_s, a_s, b_s):
    cw_ref, cb_ref, wa_ref, ba_ref, wx_ref, bx_ref, lam_ref = w
    tt = SEQ_TILE
    reverse = dr == 1
    row0 = pl.multiple_of(row0, tt)
    has_prev = (row0 > 0).astype(F32)
    has_next = (row0 + tt < n_rows).astype(F32)
    prev0 = pl.multiple_of(jnp.maximum(row0 - 8, 0), 8)
    next0 = pl.multiple_of(jnp.minimum(row0 + tt, n_rows - 8), 8)
    xs_s[dr, 0:8, :] = x_ref[pl.ds(prev0, 8), :] * has_prev
    xs_s[dr, 8:8 + tt, :] = x_ref[pl.ds(row0, tt), :]
    xs_s[dr, 8 + tt:16 + tt, :] = x_ref[pl.ds(next0, 8), :] * has_next
    xc = cb_ref[...] + jnp.zeros((tt, LANES), F32)
    for tap in range(LRU_CONV):
        xc = xc + cw_ref[tap:tap + 1, :] * xs_s[dr, pl.ds(6 + tap, tt), :]
    xb = xc.astype(BF16)
    r = jax.nn.sigmoid(_dot(xb, wa_ref[dr]) + ba_ref[dr])
    i = jax.nn.sigmoid(_dot(xb, wx_ref[dr]) + bx_ref[dr])
    z = -lam_ref[dr]
    softplus = jnp.maximum(z, 0.0) + jnp.log1p(jnp.exp(-jnp.abs(z)))
    log_a = -LRU_C * r * softplus
    a_all = jnp.exp(log_a)
    th = jnp.tanh(log_a)
    b_all = jnp.sqrt(-2.0 * th / (1.0 - th)) * (i * xc)

    nseg = 8
    slen = tt // nseg
    pitch = LRU_SEG_PITCH
    for sgm in range(nseg):
        a_s[dr, sgm * pitch:sgm * pitch + slen, :] = a_all[sgm * slen:(sgm + 1) * slen, :]
        b_s[dr, sgm * pitch:sgm * pitch + slen, :] = b_all[sgm * slen:(sgm + 1) * slen, :]
    h = jnp.zeros((nseg, LANES), F32)
    p = jnp.ones((nseg, LANES), F32)
    for i in (range(slen - 1, -1, -1) if reverse else range(slen)):
        pos = pl.ds(i, nseg, stride=pitch)
        a = a_s[dr, pos, :]
        h = a * h + b_s[dr, pos, :]
        p = a * p
        b_s[dr, pos, :] = h
        a_s[dr, pos, :] = p
    e = carry
    ins = [None] * nseg
    for sgm in (range(nseg - 1, -1, -1) if reverse else range(nseg)):
        ins[sgm] = e
        e = h[sgm:sgm + 1, :] + p[sgm:sgm + 1, :] * e
    e_in = jnp.concatenate(ins, axis=0)
    for i in range(slen):
        pos = pl.ds(i, nseg, stride=pitch)
        b_s[dr, pos, :] = b_s[dr, pos, :] + a_s[dr, pos, :] * e_in
    for sgm in range(nseg):
        o_ref[pl.ds(row0 + sgm * slen, slen), :] = b_s[dr, sgm * pitch:sgm * pitch + slen, :]
    return e


def _lru_kernel(xl_ref, xc_ref, cw_ref, cb_ref, wa_ref, ba_ref, wx_ref, bx_ref, lam_ref,
                hfl_ref, hfc_ref, hrl_ref, hrc_ref, xs_s, a_s, b_s, *, ntl, ntc):
    w = (cw_ref, cb_ref, wa_ref, ba_ref, wx_ref, bx_ref, lam_ref)
    tt = SEQ_TILE

    def run(x_ref, of_ref, or_ref, nt, carry):
        def body(j, c):
            cf = _lru_tile(x_ref, of_ref, j * tt, nt * tt, c[0], 0, w, xs_s, a_s, b_s)
            cr = _lru_tile(x_ref, or_ref, (nt - 1 - j) * tt, nt * tt, c[1], 1, w, xs_s, a_s, b_s)
            return cf, cr
        return lax.fori_loop(0, nt, body, carry, unroll=2 if nt % 2 == 0 else 1)

    zero = jnp.zeros((1, LANES), F32)
    carry = run(xc_ref, hfc_ref, hrc_ref, ntc, (zero, zero))
    run(xl_ref, hfl_ref, hrl_ref, ntl, carry)


def lru_scan(pscan, conv_w, conv_b, w_a, b_a, w_x, b_x, lam, n_batch, t_lat, t_ctx):
    tt = SEQ_TILE
    gw = GROUP_WIDTH
    nh = gw // LANES
    ntl, ntc = t_lat // tt, t_ctx // tt
    ctx0 = n_batch * t_lat // t_ctx
    per_half = LRU_BLOCKS // nh

    def halves(w):
        return jnp.stack([jnp.stack([_block_diag(w[dr, k * per_half:(k + 1) * per_half]) for k in range(nh)])
                          for dr in range(2)]).astype(BF16)

    vec = lambda a: a.reshape(2, 1, gw)
    vspec = pl.BlockSpec((2, 1, LANES), lambda b, k: (0, 0, k))
    mspec = pl.BlockSpec((2, None, LANES, LANES), lambda b, k: (0, k, 0, 0))
    lat = pl.BlockSpec((t_lat, LANES), lambda b, k: (b, k))
    ctx = pl.BlockSpec((t_ctx, LANES), lambda b, k: (b, k))
    lat_shape = jax.ShapeDtypeStruct((n_batch * t_lat, gw), F32)
    ctx_shape = jax.ShapeDtypeStruct((n_batch * t_ctx, gw), F32)
    hfl, hfc, hrl, hrc = pl.pallas_call(
        functools.partial(_lru_kernel, ntl=ntl, ntc=ntc),
        grid=(n_batch, nh),
        in_specs=[
            pl.BlockSpec((t_lat, LANES), lambda b, k: (b, nh + k)),
            pl.BlockSpec((t_ctx, LANES), lambda b, k: (ctx0 + b, nh + k)),
            pl.BlockSpec((LRU_CONV, LANES), lambda b, k: (0, k)),
            pl.BlockSpec((1, LANES), lambda b, k: (0, k)),
            mspec, vspec, mspec, vspec, vspec,
        ],
        out_specs=[lat, ctx, lat, ctx],
        out_shape=[lat_shape, ctx_shape, lat_shape, ctx_shape],
        scratch_shapes=[
            pltpu.VMEM((2, tt + 16, LANES), F32),
            pltpu.VMEM((2, 8 * LRU_SEG_PITCH, LANES), F32),
            pltpu.VMEM((2, 8 * LRU_SEG_PITCH, LANES), F32),
        ],
        compiler_params=_cparams(("parallel", "parallel")),
        name="lru_scan",
    )(pscan, pscan, conv_w, conv_b.reshape(1, gw), halves(w_a), vec(b_a), halves(w_x), vec(b_x), vec(lam))
    return [hfl, hfc], [hrl, hrc]


def _block_diag(w):
    n, d, e = w.shape
    eye = jnp.eye(n, dtype=w.dtype)
    return (eye[:, None, :, None] * w[:, :, None, :]).reshape(n * d, n * e)


def _natten_bias(rpb):
    W, KR, KC, QR, NB = GRID_W, NA_KR, NA_KC, NA_QROWS, NA_BAND
    col = np.arange(W)
    cs = np.clip(col - KC // 2, 0, W - KC)
    inwin = (col[None, :] >= cs[:, None]) & (col[None, :] < cs[:, None] + KC)
    coff = np.clip(col[None, :] - col[:, None] + (KC - 1), 0, 2 * KC - 2)
    a = np.arange(QR)[:, None]
    i = np.arange(NB)[None, :]
    first = (i - a, (i < KR) & (a >= 0))
    mid = (i - a - KR // 2, (i >= a) & (i < a + KR))
    last = (i - a - NB + QR, (i >= NB - KR) & (a >= 0))
    coh = (coff[:, :, None] == np.arange(2 * KC - 1)).astype(np.float32)
    t = jnp.einsum('hrc,qkc->hrqk', rpb.astype(F32), coh, precision=lax.Precision.HIGHEST)
    t = t * LOG2E + np.where(inwin, 0.0, NEG_BIG).astype(np.float32)
    plan = tuple(
        tuple(tuple(int(np.clip(delta[ai, ii] + KR - 1, 0, 2 * KR - 2)) if valid[ai, ii] else -1
                    for ii in range(NB)) for ai in range(QR))
        for delta, valid in (first, mid, last))

    def expand(t_ref, o_ref):
        for v in range(3):
            for ai in range(QR):
                for ii in range(NB):
                    r = plan[v][ai][ii]
                    blk = t_ref[r] if r >= 0 else jnp.full((W, W), NEG_BIG, F32)
                    o_ref[v, ai * W:(ai + 1) * W, ii * W:(ii + 1) * W] = blk

    nh = rpb.shape[0]
    return pl.pallas_call(
        expand,
        grid=(nh,),
        in_specs=[pl.BlockSpec((None, 2 * KR - 1, W, W), lambda h: (h, 0, 0, 0))],
        out_specs=pl.BlockSpec((3, None, QR * W, NB * W), lambda h: (0, h, 0, 0)),
        out_shape=jax.ShapeDtypeStruct((3, nh, QR * W, NB * W), F32),
        compiler_params=_cparams(("parallel",)),
        name="natten_bias",
    )(t)


def _natten_kernel(q_ref, k_ref, v_ref, kc_ref, vc_ref, bias_ref, o_ref, *, rows):
    W, KR, hd = GRID_W, NA_KR, HEAD_DIM
    nblk = rows // NA_QROWS
    nb = NA_BAND * W
    tq = NA_QROWS * W
    ones_col = (lax.broadcasted_iota(jnp.int32, (nb + kc_ref.shape[0], hd), 1) == 0).astype(BF16)
    for u in range(NA_STEP_BLOCKS):
        blk = pl.program_id(1) * NA_STEP_BLOCKS + u
        bs = jnp.clip(blk * NA_QROWS - KR // 2, 0, rows - NA_BAND)
        var = jnp.where(blk == 0, 0, jnp.where(blk == nblk - 1, 2, 1))
        start = pl.multiple_of(bs * W, W)
        q = (q_ref[u * tq:(u + 1) * tq, :].astype(F32) * ((hd ** -0.5) * LOG2E)).astype(BF16)
        k_all = jnp.concatenate([k_ref[pl.ds(start, nb), :], kc_ref[...]], axis=0)
        v_all = jnp.concatenate([v_ref[pl.ds(start, nb), :], vc_ref[...]], axis=0)
        for h in range(NA_HEADS):
            sl = slice(h * hd, (h + 1) * hd)
            s = _dot_nt(q[:, sl], k_all[:, sl])
            sb = s[:, :nb] + bias_ref[var, h]
            sc = s[:, nb:]
            m = jnp.maximum(jnp.max(sb, axis=-1, keepdims=True), jnp.max(sc, axis=-1, keepdims=True))
            p = jnp.concatenate([jnp.exp2(sb - m), jnp.exp2(sc - m)], axis=1).astype(BF16)
            o = _dot(p, jnp.concatenate([v_all[:, sl], ones_col], axis=1))
            o_ref[u * tq:(u + 1) * tq, sl] = o[:, :hd] / o[:, hd:hd + 1]


def natten_latent(pattn, bias, n_batch, t_lat, t_ctx):
    gw = GROUP_WIDTH
    rows = t_lat // GRID_W
    assert NA_QROWS == NA_KR // 2 and NA_BAND == NA_KR + NA_QROWS and rows % NA_QROWS == 0 and rows >= NA_BAND
    assert (rows // NA_QROWS) % NA_STEP_BLOCKS == 0
    nblk = rows // NA_QROWS // NA_STEP_BLOCKS
    tq = NA_STEP_BLOCKS * NA_QROWS * GRID_W
    ctx0 = n_batch * t_lat // t_ctx
    return pl.pallas_call(
        functools.partial(_natten_kernel, rows=rows),
        grid=(n_batch, nblk),
        in_specs=[
            pl.BlockSpec((tq, gw), lambda b, r: (b * nblk + r, 2)),
            pl.BlockSpec((t_lat, gw), lambda b, r: (b, 3)),
            pl.BlockSpec((t_lat, gw), lambda b, r: (b, 4)),
            pl.BlockSpec((t_ctx, gw), lambda b, r: (ctx0 + b, 3)),
            pl.BlockSpec((t_ctx, gw), lambda b, r: (ctx0 + b, 4)),
            pl.BlockSpec(bias.shape, lambda b, r: (0, 0, 0, 0)),
        ],
        out_specs=pl.BlockSpec((tq, gw), lambda b, r: (b * nblk + r, 0)),
        out_shape=jax.ShapeDtypeStruct((n_batch * t_lat, gw), F32),
        compiler_params=_cparams(("parallel", "arbitrary")),
        name="natten_latent",
    )(pattn, pattn, pattn, pattn, pattn, bias)


def _ctx_attn_kernel(q_ref, k_ref, v_ref, o_ref):
    hd = HEAD_DIM
    q, k, v = q_ref[...], k_ref[...], v_ref[...]
    for h in range(NA_HEADS):
        sl = slice(h * hd, (h + 1) * hd)
        s = _dot_nt(q[:, sl], k[:, sl]) * (hd ** -0.5)
        m = jnp.max(s, axis=-1, keepdims=True)
        p = jnp.exp(s - m)
        l = jnp.sum(p, axis=-1, keepdims=True)
        o_ref[:, sl] = _dot(p.astype(BF16), v[:, sl]) / l


def natten_context(pattn, n_batch, t_lat, t_ctx):
    gw = GROUP_WIDTH
    ctx0 = n_batch * t_lat // t_ctx
    return pl.pallas_call(
        _ctx_attn_kernel,
        grid=(n_batch,),
        in_specs=[pl.BlockSpec((t_ctx, gw), lambda b, c=c: (ctx0 + b, c)) for c in (2, 3, 4)],
        out_specs=pl.BlockSpec((t_ctx, gw), lambda b: (b, 0)),
        out_shape=jax.ShapeDtypeStruct((n_batch * t_ctx, gw), F32),
        compiler_params=_cparams(("parallel",)),
        name="natten_context",
    )(pattn, pattn, pattn)


def _rope_tables(t_lat, t_ctx):
    half = HEAD_DIM // 4
    freqs = ROPE_BASE ** (-np.arange(half, dtype=np.float64) / half)
    t = np.arange(t_lat)
    ang_r = (t // GRID_W)[:, None] * freqs[None, :]
    ang_c = (t % GRID_W)[:, None] * freqs[None, :]
    ang = np.concatenate([ang_r, ang_r, ang_c, ang_c], axis=1)
    cos = np.concatenate([np.cos(ang), np.ones((t_ctx, HEAD_DIM))], axis=0)
    sin = np.concatenate([np.sin(ang), np.zeros((t_ctx, HEAD_DIM))], axis=0)
    return (np.tile(cos, (1, GQA_Q_HEADS)).astype(np.float32), np.tile(sin, (1, GQA_Q_HEADS)).astype(np.float32))


def _gqa_prep_kernel(x_ref, cos_ref, sin_ref, gq_ref, gk_ref, gm_ref, q_ref, k_ref, v_ref):
    hd = HEAD_DIM
    nq, nk = GQA_Q_HEADS * hd, GQA_KV_HEADS * hd
    x = x_ref[...].astype(F32)
    cos, sin = cos_ref[...], sin_ref[...]
    gm = gm_ref[...]

    def norm_rope(t, gain, width):
        ms = _dot((t * t).astype(BF16), gm[:width, :width])
        tn = t * lax.rsqrt(ms + EPS) * gain
        lane = lax.broadcasted_iota(jnp.int32, tn.shape, 1)
        first = (lane % (hd // 2)) < (hd // 4)
        rot = jnp.where(first, -pltpu.roll(tn, width - hd // 4, 1), pltpu.roll(tn, hd // 4, 1))
        return tn * cos[:, :width] + rot * sin[:, :width]

    q = norm_rope(x[:, :nq], gq_ref[...], nq) * ((hd ** -0.5) * LOG2E)
    kt = norm_rope(x[:, nq:nq + nk], gk_ref[...], nk).T
    v = x_ref[:, nq + nk:nq + 2 * nk]
    for h in range(GQA_Q_HEADS):
        q_ref[h] = q[:, h * hd:(h + 1) * hd].astype(BF16)
    ones_col = (lax.broadcasted_iota(jnp.int32, (v.shape[0], hd), 1) == 0).astype(BF16)
    for h in range(GQA_KV_HEADS):
        k_ref[h] = kt[h * hd:(h + 1) * hd, :].astype(BF16)
        v_ref[h] = jnp.concatenate([v[:, h * hd:(h + 1) * hd], ones_col], axis=1)


def gqa_prep(pattn, gq, gk, n_batch, t_lat, t_ctx):
    tt = SEQ_TILE
    assert tt == KEY_CHUNK
    hd = HEAD_DIM
    ntl, ntc = t_lat // tt, t_ctx // tt
    s_len = t_lat + t_ctx
    cos, sin = _rope_tables(t_lat, t_ctx)
    nq = GQA_Q_HEADS * hd
    gmat = _block_diag(jnp.full((GQA_Q_HEADS, hd, hd), 1.0 / hd, F32)).astype(BF16)

    def pos(i):
        is_ctx = i >= n_batch * ntl
        c = i - n_batch * ntl
        return jnp.where(is_ctx, ntl + c % ntc, i % ntl)

    def bat(i):
        is_ctx = i >= n_batch * ntl
        return jnp.where(is_ctx, (i - n_batch * ntl) // ntc, i // ntl)

    return pl.pallas_call(
        _gqa_prep_kernel,
        grid=(n_batch * (ntl + ntc),),
        in_specs=[
            pl.BlockSpec((tt, 2 * nq), lambda i: (i, 0)),
            pl.BlockSpec((tt, nq), lambda i: (pos(i), 0)),
            pl.BlockSpec((tt, nq), lambda i: (pos(i), 0)),
            pl.BlockSpec((1, nq), lambda i: (0, 0)),
            pl.BlockSpec((1, nq // 2), lambda i: (0, 0)),
            pl.BlockSpec((nq, nq), lambda i: (0, 0)),
        ],
        out_specs=[
            pl.BlockSpec((None, GQA_Q_HEADS, tt, hd), lambda i: (bat(i), 0, pos(i), 0)),
            pl.BlockSpec((None, GQA_KV_HEADS, None, hd, tt), lambda i: (bat(i), 0, pos(i), 0, 0)),
            pl.BlockSpec((None, GQA_KV_HEADS, tt, 2 * hd), lambda i: (bat(i), 0, pos(i), 0)),
        ],
        out_shape=[
            jax.ShapeDtypeStruct((n_batch, GQA_Q_HEADS, s_len, hd), BF16),
            jax.ShapeDtypeStruct((n_batch, GQA_KV_HEADS, s_len // tt, hd, tt), BF16),
            jax.ShapeDtypeStruct((n_batch, GQA_KV_HEADS, s_len, 2 * hd), BF16),
        ],
        compiler_params=_cparams(("parallel",)),
        name="gqa_prep",
    )(pattn, cos, sin, jnp.tile(gq.astype(F32), GQA_Q_HEADS).reshape(1, nq),
      jnp.tile(gk.astype(F32), GQA_KV_HEADS).reshape(1, nq // 2), gmat)


def _gqa_kernel(q_ref, kt_ref, v_ref, o_ref, s_s, m_s, acc_s, *, unroll):
    g, tq, hd = q_ref.shape
    nch, _, ck = kt_ref.shape
    nl = ck // LANES
    q = q_ref[...].reshape(g * tq, hd)
    m_s[...] = jnp.full_like(m_s, -jnp.inf)

    def scores(c, carry):
        s = _dot(q, kt_ref[c])
        s_s[c] = s
        fold = s[:, 0:LANES]
        for j in range(1, nl):
            fold = jnp.maximum(fold, s[:, j * LANES:(j + 1) * LANES])
        m_s[...] = jnp.maximum(m_s[...], fold)
        return carry

    lax.fori_loop(0, nch, scores, 0, unroll=True)
    m_s[...] = jnp.broadcast_to(jnp.max(m_s[...], axis=-1, keepdims=True), m_s.shape)
    acc_s[...] = jnp.zeros_like(acc_s)

    def values(grp, carry):
        m = m_s[...]
        ps = []
        for u in range(unroll):
            s = s_s[grp * unroll + u]
            ps += [jnp.exp2(s[:, j * LANES:(j + 1) * LANES] - m).astype(BF16) for j in range(nl)]
        off = pl.multiple_of(grp * (unroll * ck), unroll * ck)
        acc_s[...] += _dot(jnp.concatenate(ps, axis=1), v_ref[pl.ds(off, unroll * ck), :])
        return carry

    lax.fori_loop(0, nch // unroll, values, 0)
    acc = acc_s[...]
    o = acc[:, :hd] / acc[:, hd:hd + 1]
    o_ref[...] = jnp.concatenate([o[i * tq:(i + 1) * tq] for i in range(g)], axis=1)


def gqa_attention(q, kt, v, tq, q_blk0, nq, c_blk, nch):
    n_batch = q.shape[0]
    hd = HEAD_DIM
    g = GQA_GROUP
    ck = KEY_CHUNK
    unroll = next(u for u in (11, 3, 1) if nch % u == 0)
    return pl.pallas_call(
        functools.partial(_gqa_kernel, unroll=unroll),
        grid=(n_batch, GQA_KV_HEADS, nq),
        in_specs=[
            pl.BlockSpec((None, g, tq, hd), lambda b, h, i: (b, h, q_blk0 + i, 0)),
            pl.BlockSpec((None, None, nch, hd, ck), lambda b, h, i: (b, h, c_blk, 0, 0)),
            pl.BlockSpec((None, None, nch * ck, 2 * hd), lambda b, h, i: (b, h, c_blk, 0)),
        ],
        out_specs=pl.BlockSpec((tq, g * hd), lambda b, h, i: (b * nq + i, h)),
        out_shape=jax.ShapeDtypeStruct((n_batch * nq * tq, GQA_Q_HEADS * hd), F32),
        scratch_shapes=[
            pltpu.VMEM((nch, g * tq, ck), F32),
            pltpu.VMEM((g * tq, LANES), F32),
            pltpu.VMEM((g * tq, 2 * hd), F32),
        ],
        compiler_params=_cparams(("parallel", "parallel", "arbitrary")),
        name="gqa_attention",
    )(q, kt, v)


def kernel(x, c, ctx, c_ctx, w_ada, b_ada, g_ffn1, w_ffn1_in, w_ffn1_out, g_mix, w_in, w_out, s5_lambda_re, s5_lambda_im, s5_log_step, s5_b_re, s5_b_im, s5_c_re, s5_c_im, s5_d, s5_w_glu, na_rpb, gqa_q_norm, gqa_k_norm, lru_conv_w, lru_conv_b, lru_w_a, lru_b_a, lru_w_x, lru_b_x, lru_lambda, g_ffn2, w_ffn2_in, w_ffn2_out, g_final):
    n_batch, t_lat, d = x.shape
    t_ctx = ctx.shape[1]
    depth = w_ada.shape[0]
    assert d == D_MODEL and t_lat % TOKEN_TILE == 0 and (n_batch * t_ctx) % TOKEN_TILE == 0
    assert t_lat % t_ctx == 0 and t_ctx % SEQ_TILE == 0 and t_lat % FLASH_TQ == 0
    n_lat = n_batch * t_lat
    n_all = n_lat + n_batch * t_ctx
    tiles_per_type = t_lat // TOKEN_TILE
    assert (n_batch * t_ctx) // TOKEN_TILE <= tiles_per_type

    c8 = jnp.zeros((8, D_MODEL), F32).at[:n_batch].set(c.astype(F32)).at[n_batch].set(c_ctx.astype(F32))
    mods_all = ada_mods(c8, w_ada, b_ada).reshape(depth, 8, N_MOD, D_MODEL)

    s_len = t_lat + t_ctx
    n_lat_tiles = n_lat // TOKEN_TILE
    n_ctx_tiles = (n_all - n_lat) // TOKEN_TILE
    hs = [x.reshape(n_lat, D_MODEL), ctx.reshape(n_batch * t_ctx, D_MODEL)]
    w1i, w1o = w_ffn1_in.astype(BF16), w_ffn1_out.astype(BF16)
    w2i, w2o = w_ffn2_in.astype(BF16), w_ffn2_out.astype(BF16)
    w_in_b, w_out_b, w_glu_b = w_in.astype(BF16), w_out.astype(BF16), s5_w_glu.astype(BF16)

    for l in range(depth):
        need_ctx = l < depth - 1
        mods = mods_all[l]
        h, pscan, pattn = ffn_in_projection(hs, mods, g_ffn1[l], w1i, w1o, g_mix[l], w_in_b, l,
                                            n_lat_tiles, n_ctx_tiles, tiles_per_type)

        tables = _s5_tables(s5_lambda_re[l], s5_lambda_im[l], s5_log_step[l], s5_b_re[l], s5_b_im[l],
                            s5_c_re[l], s5_c_im[l])
        ys5 = list(s5_scan(pscan, tables, n_batch, t_lat, t_ctx))

        hf, hr = lru_scan(pscan, lru_conv_w[l], lru_conv_b[l], lru_w_a[l], lru_b_a[l], lru_w_x[l], lru_b_x[l],
                          lru_lambda[l], n_batch, t_lat, t_ctx)

        yb = [natten_latent(pattn, _natten_bias(na_rpb[l]), n_batch, t_lat, t_ctx)]

        qh, kth, vh = gqa_prep(pattn, gqa_q_norm[l], gqa_k_norm[l], n_batch, t_lat, t_ctx)
        yc = [gqa_attention(qh, kth, vh, FLASH_TQ, 0, t_lat // FLASH_TQ, 0, s_len // KEY_CHUNK)]

        if need_ctx:
            yb.append(natten_context(pattn, n_batch, t_lat, t_ctx))
            yc.append(gqa_attention(qh, kth, vh, t_ctx, t_lat // t_ctx, 1, t_lat // t_ctx, t_ctx // KEY_CHUNK))
        else:
            ys5, hf, hr = ys5[:1], hf[:1], hr[:1]

        h = mix_out_ffn(h, mods, ys5, yb, yc, pscan, hf, hr, s5_d[l], w_glu_b, w_out_b, g_ffn2[l], w2i, w2o, l,
                        n_lat_tiles,
                        n_ctx_tiles if need_ctx else 0, tiles_per_type, g_final=None if need_ctx else g_final)
        hs = [h]
    return h.reshape(n_batch, t_lat, D_MODEL)
```
